```python
import math
import jax, jax.numpy as jnp
from jax import lax
import numpy as np

D_MODEL = 2048
BATCH = 2
SEQ = 8192
DEPTH = 4

N_MIXERS = 3
ATTN_HEADS = 16
ATTN_HEAD_DIM = 128
DILATION_PATTERNS = ((128, 1), (512, 4), (2048, 16))
N_DIL = len(DILATION_PATTERNS)
ATTN_BLOCK = 128
SSM_CH = 16
SSM_GROUPS = D_MODEL // SSM_CH
SSM_STATE = 64
SSM_DT_MIN = 0.001
SSM_DT_MAX = 0.1
RWKV_HEAD_DIM = 64
RWKV_HEADS = D_MODEL // RWKV_HEAD_DIM
RWKV_DECAY_LORA = 96
RWKV_AAA_LORA = 96
RWKV_GATE_LORA = 256
RWKV_GN_EPS = RWKV_HEAD_DIM * 1e-5
D_FF = 4 * D_MODEL
NORM_EPS = 1e-5
N_ATTN_LAYERS = (DEPTH + 2) // 3
N_SSM_LAYERS = (DEPTH + 1) // 3
N_RWKV_LAYERS = DEPTH // 3

kernel_name = 'hybrid_dilated_attn_s5_rwkv7_trunk'


def rms_norm(x, g):
    xf = x.astype(jnp.float32)
    y = xf * lax.rsqrt(jnp.mean(xf * xf, axis=-1, keepdims=True) + NORM_EPS)
    return (y * g.astype(jnp.float32)).astype(x.dtype)


def alibi_slopes(n):
    return 2.0 ** (-8.0 * jnp.arange(1, n + 1, dtype=jnp.float32) / n)


def dilated_window_branch(q, k, v, slopes, dilation, lookback):
    b, s, h, e = q.shape
    L = s // dilation
    n = b * dilation
    nb = -(-L // ATTN_BLOCK)
    lp = nb * ATTN_BLOCK

    def to_sub(a):
        return a.reshape(b, L, dilation, h, e).transpose(0, 2, 1, 3, 4).reshape(n, L, h, e)

    qs = jnp.pad(to_sub(q), ((0, 0), (0, lp - L), (0, 0), (0, 0))).reshape(n, nb, ATTN_BLOCK, h, e)

    def windows(a):
        a = jnp.pad(to_sub(a), ((0, 0), (ATTN_BLOCK, lp - L), (0, 0), (0, 0)))
        a = a.reshape(n, nb + 1, ATTN_BLOCK, h, e)
        return jnp.concatenate([a[:, :-1], a[:, 1:]], axis=2)

    kw, vw = windows(k), windows(v)
    qi = jnp.arange(ATTN_BLOCK)[:, None]
    kj = jnp.arange(2 * ATTN_BLOCK)[None, :]
    dist = ATTN_BLOCK + qi - kj
    key_idx = jnp.arange(nb)[:, None, None] * ATTN_BLOCK + kj[None] - ATTN_BLOCK
    valid = (dist >= 0) & (dist <= lookback) & (key_idx >= 0)
    bias = -(slopes[:, None, None] * dilation) * dist.astype(jnp.float32)
    scores = jnp.einsum('nbqhe,nbkhe->nbhqk', qs, kw).astype(jnp.float32) * (e ** -0.5)
    scores = jnp.where(valid[None, :, None], scores + bias, -jnp.inf)
    m = jnp.max(scores, axis=-1, keepdims=True)
    p = jnp.exp(scores - m)
    denom = jnp.sum(p, axis=-1, keepdims=True)
    out = jnp.einsum('nbhqk,nbkhe->nbqhe', p / denom, vw.astype(jnp.float32))
    lse = jnp.transpose((m + jnp.log(denom))[..., 0], (0, 1, 3, 2))

    def from_sub(a):
        a = a.reshape((n, lp) + a.shape[3:])[:, :L]
        a = a.reshape((b, dilation, L) + a.shape[2:])
        a = jnp.swapaxes(a, 1, 2)
        return a.reshape((b, s) + a.shape[3:])

    return from_sub(out), from_sub(lse)


def dilated_attention(h, w_qkv, w_o):
    b, s, _ = h.shape
    w = w_qkv.reshape(D_MODEL, N_DIL, 3, ATTN_HEADS, ATTN_HEAD_DIM)
    slopes = alibi_slopes(N_DIL * ATTN_HEADS).reshape(N_DIL, ATTN_HEADS)
    outs, lses = [], []
    for g, (window, dilation) in enumerate(DILATION_PATTERNS):
        qkv = jnp.einsum('bsd,dthe->tbshe', h, w[:, g])
        o, l = dilated_window_branch(qkv[0], qkv[1], qkv[2], slopes[g], dilation, window // dilation)
        outs.append(o)
        lses.append(l)
    wts = jax.nn.softmax(jnp.stack(lses), axis=0)
    o = jnp.sum(wts[..., None] * jnp.stack(outs), axis=0)
    return o.reshape(b, s, ATTN_HEADS * ATTN_HEAD_DIM).astype(h.dtype) @ w_o


def s5_mixer(h, w_in, log_dt, a_re, a_im, b_re, b_im, c_re, c_im, d_skip, w_out):
    bsz, s, _ = h.shape
    f32 = jnp.float32
    u = (h @ w_in).reshape(bsz, s, SSM_GROUPS, SSM_CH).astype(f32)
    a_re = a_re.astype(f32)
    a_im = a_im.astype(f32)
    dt = jnp.exp(log_dt.astype(f32))[:, None]
    mag = jnp.exp(dt * a_re)
    ab_re = mag * jnp.cos(dt * a_im)
    ab_im = mag * jnp.sin(dt * a_im)
    den = a_re * a_re + a_im * a_im
    zr = ab_re - 1.0
    cr = (zr * a_re + ab_im * a_im) / den
    ci = (ab_im * a_re - zr * a_im) / den
    bb_re = cr[..., None] * b_re - ci[..., None] * b_im
    bb_im = cr[..., None] * b_im + ci[..., None] * b_re
    bu_re = jnp.einsum('bsgc,gpc->bsgp', u, bb_re)
    bu_im = jnp.einsum('bsgc,gpc->bsgp', u, bb_im)
    a_seq_re = jnp.broadcast_to(ab_re, (1, s, SSM_GROUPS, SSM_STATE))
    a_seq_im = jnp.broadcast_to(ab_im, (1, s, SSM_GROUPS, SSM_STATE))

    def combine(e1, e2):
        a1r, a1i, x1r, x1i = e1
        a2r, a2i, x2r, x2i = e2
        return (a2r * a1r - a2i * a1i, a2r * a1i + a2i * a1r,
                a2r * x1r - a2i * x1i + x2r, a2r * x1i + a2i * x1r + x2i)

    _, _, st_re, st_im = lax.associative_scan(combine, (a_seq_re, a_seq_im, bu_re, bu_im), axis=1)
    y = (jnp.einsum('bsgp,gcp->bsgc', st_re, c_re) - jnp.einsum('bsgp,gcp->bsgc', st_im, c_im)
         + d_skip * u)
    y = jax.nn.gelu(y.reshape(bsz, s, SSM_GROUPS * SSM_CH)).astype(h.dtype)
    z = y @ w_out
    return z[..., :D_MODEL] * jax.nn.sigmoid(z[..., D_MODEL:])


def rwkv7_mixer(h, mu, w_rkv, w0, w1, w2, a0, a1, a2, g1, g2, k_k, k_a, r_k, ln_w, ln_b, w_o):
    b, s, d = h.shape
    f32 = jnp.float32
    H, N = RWKV_HEADS, RWKV_HEAD_DIM
    xx = jnp.pad(h, ((0, 0), (1, 0), (0, 0)))[:, :-1] - h
    x_rkv = h[:, :, None, :] + xx[:, :, None, :] * mu[:3]
    xw = h + xx * mu[3]
    xa = h + xx * mu[4]
    xg = h + xx * mu[5]
    rkv = jnp.einsum('bsjd,jde->bsje', x_rkv, w_rkv)
    r, k, v = rkv[:, :, 0], rkv[:, :, 1], rkv[:, :, 2]
    w = -jax.nn.softplus(-(w0 + jnp.tanh(xw @ w1) @ w2).astype(f32)) - 0.5
    decay = jnp.exp(-jnp.exp(w))
    a = jax.nn.sigmoid((a0 + (xa @ a1) @ a2).astype(f32))
    g = jax.nn.sigmoid(xg @ g1) @ g2
    heads = lambda t: t.astype(f32).reshape(b, s, H, N)
    r, k, v, decay, a = heads(r), heads(k), heads(v), heads(decay), heads(a)
    kk = k * k_k.astype(f32).reshape(H, N)
    kk = kk / jnp.maximum(jnp.sqrt(jnp.sum(kk * kk, axis=-1, keepdims=True)), 1e-12)
    k = k * (1.0 + (a - 1.0) * k_a.astype(f32).reshape(H, N))

    def step(state, inp):
        r_t, w_t, k_t, v_t, a_t, b_t = inp
        sa = jnp.einsum('bhvk,bhk->bhv', state, a_t)
        state = (state * w_t[:, :, None, :] + sa[..., None] * b_t[:, :, None, :]
                 + v_t[..., None] * k_t[:, :, None, :])
        return state, jnp.einsum('bhvk,bhk->bhv', state, r_t)

    tm = lambda t: jnp.moveaxis(t, 1, 0)
    s0 = jnp.zeros((b, H, N, N), f32)
    _, y = lax.scan(step, s0, (tm(r), tm(decay), tm(k), tm(v), tm(-kk), tm(kk * a)))
    y = jnp.moveaxis(y, 0, 1)
    mean = jnp.mean(y, axis=-1, keepdims=True)
    var = jnp.mean(jnp.square(y - mean), axis=-1, keepdims=True)
    y = ((y - mean) * lax.rsqrt(var + RWKV_GN_EPS)).reshape(b, s, d) * ln_w + ln_b
    bonus = jnp.sum(r * k * r_k.astype(f32), axis=-1, keepdims=True) * v
    y = y + bonus.reshape(b, s, d)
    return (y * g).astype(h.dtype) @ w_o


def squared_relu_mlp(h, w1, w2):
    return jnp.square(jax.nn.relu(h @ w1)) @ w2


def setup_inputs(seed: int = 0) -> dict:
    key = jax.random.key(seed)
    ks = jax.random.split(key, 40)
    f32 = jnp.float32

    def nrm(i, shape, scale=1.0):
        return jax.random.normal(ks[i], shape, f32) * scale

    D = D_MODEL
    HE = ATTN_HEADS * ATTN_HEAD_DIM
    G, C, P = SSM_GROUPS, SSM_CH, SSM_STATE
    NA, NB, NC = N_ATTN_LAYERS, N_SSM_LAYERS, N_RWKV_LAYERS
    return {
        'x': nrm(0, (BATCH, SEQ, D)),
        'norm_mix': 1.0 + nrm(1, (DEPTH, D), 0.02),
        'norm_mlp': 1.0 + nrm(2, (DEPTH, D), 0.02),
        'norm_f': 1.0 + nrm(3, (D,), 0.02),
        'attn_w_qkv': nrm(4, (NA, D, N_DIL * 3 * HE), D ** -0.5),
        'attn_w_o': nrm(5, (NA, HE, D), HE ** -0.5),
        'ssm_w_in': nrm(6, (NB, D, G * C), D ** -0.5),
        'ssm_log_dt': jax.random.uniform(ks[7], (NB, G), f32, math.log(SSM_DT_MIN), math.log(SSM_DT_MAX)),
        'ssm_a_re': -0.5 + nrm(8, (NB, G, P), 0.01),
        'ssm_a_im': jnp.pi * jnp.arange(P, dtype=f32) + nrm(9, (NB, G, P), 0.01),
        'ssm_b_re': nrm(10, (NB, G, P, C), (2 * C) ** -0.5),
        'ssm_b_im': nrm(11, (NB, G, P, C), (2 * C) ** -0.5),
        'ssm_c_re': nrm(12, (NB, G, C, P), 1.0),
        'ssm_c_im': nrm(13, (NB, G, C, P), 1.0),
        'ssm_d': nrm(14, (NB, G, C)),
        'ssm_w_out': nrm(15, (NB, G * C, 2 * D), (G * C) ** -0.5),
        'rwkv_mu': jax.random.uniform(ks[16], (NC, 6, D), f32),
        'rwkv_w_rkv': nrm(17, (NC, 3, D, D), D ** -0.5),
        'rwkv_w0': jax.random.uniform(ks[18], (NC, D), f32, -6.0, 1.0),
        'rwkv_w1': nrm(19, (NC, D, RWKV_DECAY_LORA), D ** -0.5),
        'rwkv_w2': nrm(20, (NC, RWKV_DECAY_LORA, D), 0.1 * RWKV_DECAY_LORA ** -0.5),
        'rwkv_a0': nrm(21, (NC, D), 0.1),
        'rwkv_a1': nrm(22, (NC, D, RWKV_AAA_LORA), D ** -0.5),
        'rwkv_a2': nrm(23, (NC, RWKV_AAA_LORA, D), 0.1 * RWKV_AAA_LORA ** -0.5),
        'rwkv_g1': nrm(24, (NC, D, RWKV_GATE_LORA), D ** -0.5),
        'rwkv_g2': nrm(25, (NC, RWKV_GATE_LORA, D), RWKV_GATE_LORA ** -0.5),
        'rwkv_k_k': 0.85 + nrm(26, (NC, D), 0.02),
        'rwkv_k_a': 1.0 + nrm(27, (NC, D), 0.02),
        'rwkv_r_k': nrm(28, (NC, RWKV_HEADS, RWKV_HEAD_DIM), 0.1),
        'rwkv_ln_w': 1.0 + nrm(29, (NC, D), 0.02),
        'rwkv_ln_b': nrm(30, (NC, D), 0.02),
        'rwkv_w_o': nrm(31, (NC, D, D), D ** -0.5),
        'mlp_w1': nrm(32, (DEPTH, D, D_FF), D ** -0.5),
        'mlp_w2': nrm(33, (DEPTH, D_FF, D), D_FF ** -0.5),
    }


def reference(x, norm_mix, norm_mlp, norm_f, attn_w_qkv, attn_w_o, ssm_w_in, ssm_log_dt,
              ssm_a_re, ssm_a_im, ssm_b_re, ssm_b_im, ssm_c_re, ssm_c_im, ssm_d, ssm_w_out,
              rwkv_mu, rwkv_w_rkv, rwkv_w0, rwkv_w1, rwkv_w2, rwkv_a0, rwkv_a1, rwkv_a2,
              rwkv_g1, rwkv_g2, rwkv_k_k, rwkv_k_a, rwkv_r_k, rwkv_ln_w, rwkv_ln_b, rwkv_w_o,
              mlp_w1, mlp_w2):
    ia = ib = ic = 0
    for layer in range(DEPTH):
        h = rms_norm(x, norm_mix[layer])
        kind = layer % N_MIXERS
        if kind == 0:
            mix = dilated_attention(h, attn_w_qkv[ia], attn_w_o[ia])
            ia += 1
        elif kind == 1:
            mix = s5_mixer(h, ssm_w_in[ib], ssm_log_dt[ib], ssm_a_re[ib], ssm_a_im[ib],
                           ssm_b_re[ib], ssm_b_im[ib], ssm_c_re[ib], ssm_c_im[ib],
                           ssm_d[ib], ssm_w_out[ib])
            ib += 1
        else:
            mix = rwkv7_mixer(h, rwkv_mu[ic], rwkv_w_rkv[ic], rwkv_w0[ic], rwkv_w1[ic], rwkv_w2[ic],
                              rwkv_a0[ic], rwkv_a1[ic], rwkv_a2[ic], rwkv_g1[ic], rwkv_g2[ic],
                              rwkv_k_k[ic], rwkv_k_a[ic], rwkv_r_k[ic], rwkv_ln_w[ic],
                              rwkv_ln_b[ic], rwkv_w_o[ic])
            ic += 1
        x = x + mix
        x = x + squared_relu_mlp(rms_norm(x, norm_mlp[layer]), mlp_w1[layer], mlp_w2[layer])
    return rms_norm(x, norm_f)
```

```python
import functools

import jax
import jax.numpy as jnp
from jax import lax
from jax.experimental import pallas as pl
from jax.experimental.pallas import tpu as pltpu

F32 = jnp.float32
BF16 = jnp.bfloat16
HIGHEST = lax.Precision.HIGHEST

NORM_EPS = 1e-5
LANES = 128
VMEM_LIMIT_BYTES = 56 * 2**20
MASK_VALUE = -1e30

ATTN_PATTERNS = ((128, 1), (512, 4), (2048, 16))
ATTN_BLOCK = 128
ATTN_HEAD_DIM = 128
SSM_CH = 16
SSM_CHUNK = 16
SSM_DT_MIN = 0.001
SSM_DT_MAX = 0.1
RWKV_HEAD_DIM = 64
RWKV_CHUNK = 64
RWKV_GN_EPS = RWKV_HEAD_DIM * 1e-5


def _cparams(*sem):
    return pltpu.CompilerParams(dimension_semantics=sem, vmem_limit_bytes=VMEM_LIMIT_BYTES)


def _tile(n, pref):
    t = min(n, pref)
    while n % t:
        t //= 2
    return t


def _rms(x, g):
    ms = jnp.mean(x * x, axis=-1, keepdims=True)
    return x * lax.rsqrt(ms + NORM_EPS) * g


def _dot(a, b):
    return jnp.dot(a, b, preferred_element_type=F32)


def _dot_nt(a, b):
    return lax.dot_general(a, b, (((1,), (1,)), ((), ())), preferred_element_type=F32)


def _dot_tn(a, b):
    return lax.dot_general(a, b, (((0,), (0,)), ((), ())), preferred_element_type=F32)


def _dot_f32(a, b):
    return jnp.dot(a, b, precision=HIGHEST, preferred_element_type=F32)


def _split3(x):
    hi = x.astype(BF16)
    r1 = x - hi.astype(F32)
    mid = r1.astype(BF16)
    lo = (r1 - mid.astype(F32)).astype(BF16)
    return hi, mid, lo


def _norm_matmul_kernel(x_ref, g_ref, w_ref, o_ref, h_ref):
    @pl.when(pl.program_id(1) == 0)
    def _():
        h_ref[...] = _rms(x_ref[...], g_ref[...]).astype(BF16)

    o_ref[...] = _dot(h_ref[...], w_ref[...]).astype(o_ref.dtype)


def norm_matmul(x, g, w, out_dtype, tm=512, tn=1024):
    t, d = x.shape
    n = w.shape[1]
    tm, tn = _tile(t, tm), _tile(n, tn)
    return pl.pallas_call(
        _norm_matmul_kernel,
        out_shape=jax.ShapeDtypeStruct((t, n), out_dtype),
        grid=(t // tm, n // tn),
        in_specs=[pl.BlockSpec((tm, d), lambda i, j: (i, 0)),
                  pl.BlockSpec((1, d), lambda i, j: (0, 0)),
                  pl.BlockSpec((d, tn), lambda i, j: (0, j))],
        out_specs=pl.BlockSpec((tm, tn), lambda i, j: (i, j)),
        scratch_shapes=[pltpu.VMEM((tm, d), BF16)],
        compiler_params=_cparams("parallel", "arbitrary"),
        name="norm_matmul",
    )(x, g.reshape(1, d), w)


def _matmul_res_kernel(a_ref, w_ref, r_ref, o_ref):
    o_ref[...] = r_ref[...] + _dot(a_ref[...], w_ref[...])


def matmul_residual(a, w, res, tm=512, tn=1024):
    t, k = a.shape
    n = w.shape[1]
    tm, tn = _tile(t, tm), _tile(n, tn)
    return pl.pallas_call(
        _matmul_res_kernel,
        out_shape=jax.ShapeDtypeStruct((t, n), F32),
        grid=(t // tm, n // tn),
        in_specs=[pl.BlockSpec((tm, k), lambda i, j: (i, 0)),
                  pl.BlockSpec((k, tn), lambda i, j: (0, j)),
                  pl.BlockSpec((tm, tn), lambda i, j: (i, j))],
        out_specs=pl.BlockSpec((tm, tn), lambda i, j: (i, j)),
        compiler_params=_cparams("parallel", "parallel"),
        name="matmul_residual",
    )(a, w, res)


def _mlp_kernel(x_ref, g_ref, w1_ref, w2_ref, gf_ref, o_ref, h_ref, *, final_norm):
    f = pl.program_id(1)

    @pl.when(f == 0)
    def _():
        x = x_ref[...]
        h_ref[...] = _rms(x, g_ref[...]).astype(BF16)
        o_ref[...] = x

    a = _dot(h_ref[...], w1_ref[...])
    a = jnp.square(jnp.maximum(a, 0.0)).astype(BF16)
    o_ref[...] += _dot(a, w2_ref[...])

    if final_norm:
        @pl.when(f == pl.num_programs(1) - 1)
        def _():
            o_ref[...] = _rms(o_ref[...], gf_ref[...])


def mlp_residual(x, g, w1, w2, g_final=None, tm=512, tf=512):
    t, d = x.shape
    ff = w1.shape[1]
    tm, tf = _tile(t, tm), _tile(ff, tf)
    final_norm = g_final is not None
    gf = (g_final if final_norm else g).reshape(1, d)
    return pl.pallas_call(
        functools.partial(_mlp_kernel, final_norm=final_norm),
        out_shape=jax.ShapeDtypeStruct((t, d), F32),
        grid=(t // tm, ff // tf),
        in_specs=[pl.BlockSpec((tm, d), lambda i, f: (i, 0)),
                  pl.BlockSpec((1, d), lambda i, f: (0, 0)),
                  pl.BlockSpec((d, tf), lambda i, f: (0, f)),
                  pl.BlockSpec((tf, d), lambda i, f: (f, 0)),
                  pl.BlockSpec((1, d), lambda i, f: (0, 0))],
        out_specs=pl.BlockSpec((tm, d), lambda i, f: (i, 0)),
        scratch_shapes=[pltpu.VMEM((tm, d), BF16)],
        compiler_params=_cparams("parallel", "arbitrary"),
        name="mlp_residual",
    )(x, g.reshape(1, d), w1, w2, gf)


def _attn_kernel(slope_ref, q_ref, kp_ref, kc_ref, vp_ref, vc_ref, o_ref, lse_ref, *, heads, scale):
    blk, e = ATTN_BLOCK, ATTN_HEAD_DIM
    j = pl.program_id(2)
    qi = lax.broadcasted_iota(jnp.int32, (blk, blk), 0)
    kj = lax.broadcasted_iota(jnp.int32, (blk, blk), 1)
    dist_c = (qi - kj).astype(F32)
    dist_p = dist_c + float(blk)
    valid_c = kj <= qi
    valid_p = (kj >= qi) & (j > 0)
    lane = lax.broadcasted_iota(jnp.int32, (blk, LANES), 1)
    lse_tile = jnp.zeros((blk, LANES), F32)
    for h in range(heads):
        sl = slice(h * e, (h + 1) * e)
        q = q_ref[:, sl]
        slope = slope_ref[h]
        sc = jnp.where(valid_c, _dot_nt(q, kc_ref[:, sl]) * scale - slope * dist_c, MASK_VALUE)
        sp = jnp.where(valid_p, _dot_nt(q, kp_ref[:, sl]) * scale - slope * dist_p, MASK_VALUE)
        m = jnp.maximum(jnp.max(sc, axis=-1, keepdims=True), jnp.max(sp, axis=-1, keepdims=True))
        pc = jnp.exp(sc - m)
        pp = jnp.exp(sp - m)
        den = jnp.sum(pc, axis=-1, keepdims=True) + jnp.sum(pp, axis=-1, keepdims=True)
        o = _dot(pc.astype(BF16), vc_ref[:, sl]) + _dot(pp.astype(BF16), vp_ref[:, sl])
        o_ref[:, sl] = (o / den).astype(o_ref.dtype)
        lse_tile = jnp.where(lane == h, m + jnp.log(den), lse_tile)
    lse_ref[...] = lse_tile


def _attn_group(qkv, slopes, group, dilation, batch, seq, heads):
    e, blk = ATTN_HEAD_DIM, ATTN_BLOCK
    he = heads * e
    ncol = qkv.shape[1] // he
    sub = seq // dilation
    nb = sub // blk
    qkv3 = qkv.reshape(batch, sub, dilation * ncol * he)

    def col(kind):
        return lambda b, r, j: (b, j, r * ncol + group * 3 + kind)

    def col_prev(kind):
        return lambda b, r, j: (b, jnp.maximum(j - 1, 0), r * ncol + group * 3 + kind)

    blockspec = lambda im: pl.BlockSpec((None, blk, he), im)
    out, lse = pl.pallas_call(
        functools.partial(_attn_kernel, heads=heads, scale=e ** -0.5),
        out_shape=(jax.ShapeDtypeStruct((batch, sub, dilation * he), BF16),
                   jax.ShapeDtypeStruct((batch, sub, dilation * LANES), F32)),
        grid=(batch, dilation, nb),
        in_specs=[pl.BlockSpec(memory_space=pltpu.SMEM),
                  blockspec(col(0)), blockspec(col_prev(1)), blockspec(col(1)),
                  blockspec(col_prev(2)), blockspec(col(2))],
        out_specs=(pl.BlockSpec((None, blk, he), lambda b, r, j: (b, j, r)),
                   pl.BlockSpec((None, blk, LANES), lambda b, r, j: (b, j, r))),
        compiler_params=_cparams("parallel", "parallel", "arbitrary"),
        name=f"dilated_attn_g{group}",
    )(slopes, qkv3, qkv3, qkv3, qkv3, qkv3)
    return out.reshape(batch * seq, he), lse.reshape(batch * seq, LANES)


def _attn_out_kernel(o0_ref, o1_ref, o2_ref, l0_ref, l1_ref, l2_ref, w_ref, r_ref, out_ref, m_ref, *, heads):
    e = ATTN_HEAD_DIM

    @pl.when(pl.program_id(1) == 0)
    def _():
        l0, l1, l2 = l0_ref[...], l1_ref[...], l2_ref[...]
        mx = jnp.maximum(jnp.maximum(l0, l1), l2)
        e0, e1, e2 = jnp.exp(l0 - mx), jnp.exp(l1 - mx), jnp.exp(l2 - mx)
        inv = 1.0 / (e0 + e1 + e2)
        w0, w1, w2 = e0 * inv, e1 * inv, e2 * inv
        for h in range(heads):
            sl = slice(h * e, (h + 1) * e)
            acc = (w0[:, h:h + 1] * o0_ref[:, sl].astype(F32)
                   + w1[:, h:h + 1] * o1_ref[:, sl].astype(F32)
                   + w2[:, h:h + 1] * o2_ref[:, sl].astype(F32))
            m_ref[:, sl] = acc.astype(BF16)

    out_ref[...] = r_ref[...] + _dot(m_ref[...], w_ref[...])


def attention_layer(x, g, w_qkv, w_o, batch, seq):
    t, d = x.shape
    n_dil = len(ATTN_PATTERNS)
    he = w_o.shape[0]
    heads = he // ATTN_HEAD_DIM
    qkv = norm_matmul(x, g, w_qkv, BF16)
    n_sl = n_dil * heads
    slopes = (2.0 ** (-8.0 * jnp.arange(1, n_sl + 1, dtype=F32) / n_sl)).reshape(n_dil, heads)
    outs, lses = [], []
    for grp, (window, dilation) in enumerate(ATTN_PATTERNS):
        assert window // dilation == ATTN_BLOCK and (seq // dilation) % ATTN_BLOCK == 0
        o, l = _attn_group(qkv, slopes[grp] * dilation, grp, dilation, batch, seq, heads)
        outs.append(o)
        lses.append(l)
    tm, tn = _tile(t, 512), _tile(d, 1024)
    ospec = pl.BlockSpec((tm, he), lambda i, j: (i, 0))
    lspec = pl.BlockSpec((tm, LANES), lambda i, j: (i, 0))
    return pl.pallas_call(
        functools.partial(_attn_out_kernel, heads=heads),
        out_shape=jax.ShapeDtypeStruct((t, d), F32),
        grid=(t // tm, d // tn),
        in_specs=[ospec, ospec, ospec, lspec, lspec, lspec,
                  pl.BlockSpec((he, tn), lambda i, j: (0, j)),
                  pl.BlockSpec((tm, tn), lambda i, j: (i, j))],
        out_specs=pl.BlockSpec((tm, tn), lambda i, j: (i, j)),
        scratch_shapes=[pltpu.VMEM((tm, he), BF16)],
        compiler_params=_cparams("parallel", "arbitrary"),
        name="attn_merge_out_proj",
    )(*outs, *lses, w_o, x)


def _s5_chunk_operators(log_dt, a_re, a_im, b_re, b_im, c_re, c_im, d_skip, chunk):
    hp = dict(precision=HIGHEST)
    n_g, n_p = a_re.shape
    n_c = b_re.shape[-1]
    dt = jnp.exp(log_dt)[:, None]
    mag = jnp.exp(dt * a_re)
    ab_re = mag * jnp.cos(dt * a_im)
    ab_im = mag * jnp.sin(dt * a_im)
    den = a_re * a_re + a_im * a_im
    zr = ab_re - 1.0
    cr = (zr * a_re + ab_im * a_im) / den
    ci = (ab_im * a_re - zr * a_im) / den
    bb_re = cr[..., None] * b_re - ci[..., None] * b_im
    bb_im = cr[..., None] * b_im + ci[..., None] * b_re
    pr, pi = [jnp.ones_like(ab_re)], [jnp.zeros_like(ab_re)]
    for _ in range(chunk):
        pr, pi = pr + [pr[-1] * ab_re - pi[-1] * ab_im], pi + [pr[-1] * ab_im + pi[-1] * ab_re]
    pr, pi = jnp.stack(pr), jnp.stack(pi)
    ce_re = c_re[None] * pr[:, :, None, :] - c_im[None] * pi[:, :, None, :]
    ce_im = c_re[None] * pi[:, :, None, :] + c_im[None] * pr[:, :, None, :]
    kern = (jnp.einsum('tgcp,gpd->tgcd', ce_re[:chunk], bb_re, **hp)
            - jnp.einsum('tgcp,gpd->tgcd', ce_im[:chunk], bb_im, **hp))
    kern = kern.at[0].add(jax.vmap(jnp.diag)(d_skip))
    s_idx = jnp.arange(chunk)[:, None]
    t_idx = jnp.arange(chunk)[None, :]
    lag = t_idx - s_idx
    m_op = jnp.where((lag >= 0)[:, :, None, None, None], kern[jnp.maximum(lag, 0)], 0.0)
    m_op = m_op.transpose(2, 0, 4, 1, 3).reshape(n_g, chunk * n_c, chunk * n_c)
    qr, qi = pr[chunk - 1::-1][:chunk], pi[chunk - 1::-1][:chunk]
    bo_re = qr[..., None] * bb_re[None] - qi[..., None] * bb_im[None]
    bo_im = qr[..., None] * bb_im[None] + qi[..., None] * bb_re[None]
    flat_b = lambda a: a.transpose(1, 0, 3, 2).reshape(n_g, chunk * n_c, n_p)
    bo_re, bo_im = flat_b(bo_re), flat_b(bo_im)
    b_op = jnp.concatenate([bo_re, bo_im, bo_im, bo_re], axis=-1)
    flat_c = lambda a: a.transpose(1, 3, 0, 2).reshape(n_g, n_p, chunk * n_c)
    c_op = jnp.concatenate([flat_c(ce_re[1:]), -flat_c(ce_im[1:])], axis=1)
    al_re, al_im = pr[chunk], pi[chunk]
    coef_same = jnp.concatenate([al_re, al_re, al_re, al_re], axis=-1)
    coef_cross = jnp.concatenate([-al_im, al_im, al_im, -al_im], axis=-1)
    return m_op, b_op, c_op, coef_same, coef_cross


def _s5_in_kernel(u_ref, b_ref, o_ref):
    o_ref[...] = jnp.einsum('gnk,gkp->gnp', u_ref[...], b_ref[...], preferred_element_type=F32)


def _s5_scan_kernel(xin_ref, cs_ref, cc_ref, o_ref, st_ref, *, half):
    @pl.when(pl.program_id(0) == 0)
    def _():
        st_ref[...] = jnp.zeros_like(st_ref)

    cs, cc = cs_ref[...], cc_ref[...]

    def step(n, st):
        o_ref[:, n] = st[:, :, :half].astype(o_ref.dtype)
        st_sw = jnp.concatenate([st[:, :, half:], st[:, :, :half]], axis=-1)
        return cs * st + cc * st_sw + xin_ref[:, n]

    st_ref[...] = lax.fori_loop(0, xin_ref.shape[1], step, st_ref[...])


def _s5_out_kernel(u_ref, m_ref, xp_ref, c_ref, o_ref):
    y = (jnp.einsum('gnk,gkm->gnm', u_ref[...], m_ref[...], preferred_element_type=F32)
         + jnp.einsum('gnp,gpm->gnm', xp_ref[...], c_ref[...], preferred_element_type=F32))
    o_ref[...] = jax.nn.gelu(y).astype(o_ref.dtype)


def _glu_out_kernel(y_ref, wa_ref, wb_ref, r_ref, o_ref):
    y = y_ref[...]
    o_ref[...] = r_ref[...] + _dot(y, wa_ref[...]) * jax.nn.sigmoid(_dot(y, wb_ref[...]))


def s5_layer(x, g, w_in, log_dt, a_re, a_im, b_re, b_im, c_re, c_im, d_skip, w_out, batch, seq):
    t, d = x.shape
    n_g, n_p = a_re.shape
    n_c = SSM_CH
    ck = SSM_CHUNK
    kdim = ck * n_c
    n_rows = t // ck
    m_op, b_op, c_op, coef_same, coef_cross = _s5_chunk_operators(
        log_dt, a_re, a_im, b_re, b_im, c_re, c_im, d_skip, ck)
    u = norm_matmul(x, g, w_in, BF16)
    ug = u.reshape(n_rows, ck, n_g, n_c).transpose(2, 0, 1, 3).reshape(n_g, n_rows, kdim)
    gb = _tile(n_g, 8)
    tr = _tile(n_rows, 1024)
    xin = pl.pallas_call(
        _s5_in_kernel,
        out_shape=jax.ShapeDtypeStruct((n_g, n_rows, 4 * n_p), F32),
        grid=(n_g // gb, n_rows // tr),
        in_specs=[pl.BlockSpec((gb, tr, kdim), lambda i, j: (i, j, 0)),
                  pl.BlockSpec((gb, kdim, 4 * n_p), lambda i, j: (i, 0, 0))],
        out_specs=pl.BlockSpec((gb, tr, 4 * n_p), lambda i, j: (i, j, 0)),
        compiler_params=_cparams("parallel", "parallel"),
        name="s5_chunk_inputs",
    )(ug, b_op.astype(BF16))
    n_chunks = seq // ck
    xin_t = xin.transpose(1, 0, 2).reshape(batch, n_chunks, n_g, 4 * n_p)
    tn = _tile(n_chunks, 32)
    xprev = pl.pallas_call(
        functools.partial(_s5_scan_kernel, half=2 * n_p),
        out_shape=jax.ShapeDtypeStruct((batch, n_chunks, n_g, 2 * n_p), BF16),
        grid=(n_chunks // tn,),
        in_specs=[pl.BlockSpec((batch, tn, n_g, 4 * n_p), lambda i: (0, i, 0, 0)),
                  pl.BlockSpec((n_g, 4 * n_p), lambda i: (0, 0)),
                  pl.BlockSpec((n_g, 4 * n_p), lambda i: (0, 0))],
        out_specs=pl.BlockSpec((batch, tn, n_g, 2 * n_p), lambda i: (0, i, 0, 0)),
        scratch_shapes=[pltpu.VMEM((batch, n_g, 4 * n_p), F32)],
        compiler_params=_cparams("arbitrary"),
        name="s5_chunk_scan",
    )(xin_t, coef_same, coef_cross)
    xprev_g = xprev.reshape(n_rows, n_g, 2 * n_p).transpose(1, 0, 2)
    yg = pl.pallas_call(
        _s5_out_kernel,
        out_shape=jax.ShapeDtypeStruct((n_g, n_rows, kdim), BF16),
        grid=(n_g // gb, n_rows // tr),
        in_specs=[pl.BlockSpec((gb, tr, kdim), lambda i, j: (i, j, 0)),
                  pl.BlockSpec((gb, kdim, kdim), lambda i, j: (i, 0, 0)),
                  pl.BlockSpec((gb, tr, 2 * n_p), lambda i, j: (i, j, 0)),
                  pl.BlockSpec((gb, 2 * n_p, kdim), lambda i, j: (i, 0, 0))],
        out_specs=pl.BlockSpec((gb, tr, kdim), lambda i, j: (i, j, 0)),
        compiler_params=_cparams("parallel", "parallel"),
        name="s5_chunk_outputs",
    )(ug, m_op.astype(BF16), xprev_g, c_op.astype(BF16))
    y = yg.reshape(n_g, n_rows, ck, n_c).transpose(1, 2, 0, 3).reshape(t, n_g * n_c)
    tm, tn2 = _tile(t, 512), _tile(d, 512)
    nj = d // tn2
    return pl.pallas_call(
        _glu_out_kernel,
        out_shape=jax.ShapeDtypeStruct((t, d), F32),
        grid=(t // tm, nj),
        in_specs=[pl.BlockSpec((tm, n_g * n_c), lambda i, j: (i, 0)),
                  pl.BlockSpec((n_g * n_c, tn2), lambda i, j: (0, j)),
                  pl.BlockSpec((n_g * n_c, tn2), lambda i, j: (0, j + nj)),
                  pl.BlockSpec((tm, tn2), lambda i, j: (i, j))],
        out_specs=pl.BlockSpec((tm, tn2), lambda i, j: (i, j)),
        compiler_params=_cparams("parallel", "parallel"),
        name="s5_glu_out_proj",
    )(y, w_out, w_out, x)


def _shift_norm(x_ref, xp_ref, g_ref, first):
    g = g_ref[...]
    h = _rms(x_ref[...], g)
    prev = _rms(xp_ref[7:8, :], g)
    prev = jnp.where(first, 0.0, prev)
    row = lax.broadcasted_iota(jnp.int32, h.shape, 0)
    hp = jnp.where(row == 0, prev, pltpu.roll(h, 1, 0))
    return h, hp


def _rwkv_proj_kernel(x_ref, xp_ref, g_ref, mu_ref, w_ref, o_ref, h_ref, d_ref, l_ref, *, tm, seq):
    i, j, n = pl.program_id(0), pl.program_id(1), pl.program_id(2)

    @pl.when((j == 0) & (n == 0))
    def _():
        h, hp = _shift_norm(x_ref, xp_ref, g_ref, (i * tm) % seq == 0)
        h_ref[...] = h
        d_ref[...] = hp - h

    @pl.when(n == 0)
    def _():
        l_ref[...] = (h_ref[...] + d_ref[...] * mu_ref[...]).astype(BF16)

    o_ref[...] = _dot(l_ref[...], w_ref[...])


def _softplus(z):
    return jnp.maximum(z, 0.0) + jnp.log(1.0 + jnp.exp(-jnp.abs(z)))


def _rwkv_lora_kernel(x_ref, xp_ref, g_ref, mu_ref, w0_ref, w1_ref, w2_ref, a0_ref, a1_ref, a2_ref,
                      g1_ref, g2_ref, lw_ref, a_ref, gate_ref, *, tm, seq):
    h, hp = _shift_norm(x_ref, xp_ref, g_ref, (pl.program_id(0) * tm) % seq == 0)
    dlt = hp - h
    xw = (h + dlt * mu_ref[0:1, :]).astype(BF16)
    xa = (h + dlt * mu_ref[1:2, :]).astype(BF16)
    xg = (h + dlt * mu_ref[2:3, :]).astype(BF16)
    wl = w0_ref[...] + _dot(jnp.tanh(_dot(xw, w1_ref[...])).astype(BF16), w2_ref[...])
    w = -_softplus(-wl) - 0.5
    lw_ref[...] = -jnp.exp(w)
    a_ref[...] = jax.nn.sigmoid(a0_ref[...] + _dot(_dot(xa, a1_ref[...]).astype(BF16), a2_ref[...]))
    gate_ref[...] = _dot(jax.nn.sigmoid(_dot(xg, g1_ref[...])).astype(BF16), g2_ref[...])


def _unit_lower_inverse(a, size):
    row = lax.broadcasted_iota(jnp.int32, (size, size), 0)
    col = lax.broadcasted_iota(jnp.int32, (size, size), 1)
    inv = jnp.where(row == col, 1.0, 0.0) + a
    apow = a
    span = 2
    while span < size:
        apow = _dot_f32(apow, apow)
        inv = inv + _dot_f32(inv, apow)
        span *= 2
    return inv


def _rwkv_core_kernel(r_ref, k_ref, v_ref, lw_ref, a_ref, gate_ref, kk_ref, ka_ref, rk_ref, lnw_ref, lnb_ref,
                      o_ref, s_ref, at_ref, b_ref, k2_ref, y_ref):
    ck, hd = RWKV_CHUNK, RWKV_HEAD_DIM
    tc = r_ref.shape[0]

    @pl.when(pl.program_id(2) == 0)
    def _():
        s_ref[...] = jnp.zeros_like(s_ref)

    lane = lax.broadcasted_iota(jnp.int32, (1, LANES), 1)
    head_a = lane < hd
    hrow = lax.broadcasted_iota(jnp.int32, (LANES, LANES), 0) // hd
    hcol = lax.broadcasted_iota(jnp.int32, (LANES, LANES), 1) // hd
    same_head = hrow == hcol
    head_ones = jnp.where(same_head, 1.0, 0.0).astype(BF16)

    def head_sum(x):
        hi, mid, lo = _split3(x)
        return _dot(hi, head_ones) + _dot(mid, head_ones) + _dot(lo, head_ones)

    r_all, k_all, v_all, a_all = r_ref[...], k_ref[...], v_ref[...], a_ref[...]
    kk = k_all * kk_ref[...]
    kk = kk / jnp.maximum(jnp.sqrt(head_sum(kk * kk)), 1e-12)
    k2 = k_all * (1.0 + (a_all - 1.0) * ka_ref[...])
    at_ref[...] = -kk
    b_ref[...] = kk * a_all
    k2_ref[...] = k2
    bonus = head_sum(r_all * k2 * rk_ref[...]) * v_all

    row = lax.broadcasted_iota(jnp.int32, (ck, ck), 0)
    col = lax.broadcasted_iota(jnp.int32, (ck, ck), 1)
    incl = col <= row
    strict = col < row
    tril = jnp.where(incl, 1.0, 0.0).astype(BF16)

    def chunk_step(ci, carry):
        sl = pl.ds(pl.multiple_of(ci * ck, ck), ck)
        r, v, lw = r_ref[sl, :], v_ref[sl, :], lw_ref[sl, :]
        at, b, k2c = at_ref[sl, :], b_ref[sl, :], k2_ref[sl, :]
        hi, mid, lo = _split3(lw)
        cs = _dot(tril, hi) + _dot(tril, mid) + _dot(tril, lo)
        tot = cs[ck - 1:ck, :]
        gam_inv = jnp.exp(-cs)
        gam_rem = jnp.exp(tot - cs)
        atm = at * jnp.exp(cs - lw)
        rm = r * jnp.exp(cs)
        lhs = jnp.concatenate([atm, rm], axis=0)
        bm = (b * gam_inv).astype(BF16)
        km = (k2c * gam_inv).astype(BF16)
        tinv, a_rb, a_ak, a_rk = [], [], [], []
        for sel in (head_a, ~head_a):
            lhs_h = jnp.where(sel, lhs, 0.0).astype(BF16)
            pb = _dot_nt(lhs_h, bm)
            pk = _dot_nt(lhs_h, km)
            tinv.append(_unit_lower_inverse(jnp.where(strict, pb[:ck], 0.0), ck))
            a_ak.append(jnp.where(strict, pk[:ck], 0.0))
            a_rb.append(jnp.where(incl, pb[ck:], 0.0))
            a_rk.append(jnp.where(incl, pk[ck:], 0.0))
        pick = lambda st: jnp.where(head_a, st[:ck], st[ck:])
        vb = v.astype(BF16)
        akv = pick(_dot(jnp.concatenate(a_ak, axis=0).astype(BF16), vb))
        arkv = pick(_dot(jnp.concatenate(a_rk, axis=0).astype(BF16), vb))
        s = s_ref[...]
        xs = _dot_nt(lhs.astype(BF16), s.astype(BF16))
        u = pick(_dot(jnp.concatenate(tinv, axis=0).astype(BF16), (xs[:ck] + akv).astype(BF16)))
        ub = u.astype(BF16)
        y_ref[sl, :] = xs[ck:] + pick(_dot(jnp.concatenate(a_rb, axis=0).astype(BF16), ub)) + arkv
        uv = jnp.concatenate([ub, vb], axis=0)
        bk = jnp.concatenate([b * gam_rem, k2c * gam_rem], axis=0).astype(BF16)
        s_ref[...] = s * jnp.exp(tot) + jnp.where(same_head, _dot_tn(uv, bk), 0.0)
        return carry

    lax.fori_loop(0, tc // ck, chunk_step, 0)

    y = y_ref[...]
    mean = head_sum(y) * (1.0 / hd)
    yc = y - mean
    var = head_sum(yc * yc) * (1.0 / hd)
    yn = yc * lax.rsqrt(var + RWKV_GN_EPS) * lnw_ref[...] + lnb_ref[...]
    o_ref[...] = ((yn + bonus) * gate_ref[...]).astype(o_ref.dtype)


def _pad_to(a, axis, size):
    pad = [(0, 0)] * a.ndim
    pad[axis] = (0, size - a.shape[axis])
    return jnp.pad(a, pad)


def rwkv_layer(x, g, mu, w_rkv, w0, w1, w2, a0, a1, a2, g1, g2, k_k, k_a, r_k, ln_w, ln_b, w_o, batch, seq):
    t, d = x.shape
    row = lambda p: p.reshape(1, d).astype(F32)
    g2d = g.reshape(1, d)
    tm = _tile(seq, 256)
    tn = _tile(d, 1024)
    prev_spec = lambda: pl.BlockSpec((8, d), (lambda i, *_: (jnp.maximum(i * (tm // 8) - 1, 0), 0)))
    rkv = pl.pallas_call(
        functools.partial(_rwkv_proj_kernel, tm=tm, seq=seq),
        out_shape=jax.ShapeDtypeStruct((3, t, d), F32),
        grid=(t // tm, 3, d // tn),
        in_specs=[pl.BlockSpec((tm, d), lambda i, j, n: (i, 0)),
                  prev_spec(),
                  pl.BlockSpec((1, d), lambda i, j, n: (0, 0)),
                  pl.BlockSpec((None, 1, d), lambda i, j, n: (j, 0, 0)),
                  pl.BlockSpec((None, d, tn), lambda i, j, n: (j, 0, n))],
        out_specs=pl.BlockSpec((None, tm, tn), lambda i, j, n: (j, i, n)),
        scratch_shapes=[pltpu.VMEM((tm, d), F32), pltpu.VMEM((tm, d), F32), pltpu.VMEM((tm, d), BF16)],
        compiler_params=_cparams("parallel", "arbitrary", "arbitrary"),
        name="rwkv_rkv_proj",
    )(x, x, g2d, mu[:3].reshape(3, 1, d), w_rkv)

    pad_rank = lambda w_a, w_b: (_pad_to(w_a, 1, -(-w_a.shape[1] // LANES) * LANES).astype(BF16),
                                 _pad_to(w_b, 0, -(-w_b.shape[0] // LANES) * LANES).astype(BF16))
    w1p, w2p = pad_rank(w1, w2)
    a1p, a2p = pad_rank(a1, a2)
    g1p, g2p = pad_rank(g1, g2)
    full = lambda a: pl.BlockSpec(a.shape, lambda i: (0,) * a.ndim)
    tok = pl.BlockSpec((tm, d), lambda i: (i, 0))
    lora_in = [x, x, g2d, mu[3:6], row(w0), w1p, w2p, row(a0), a1p, a2p, g1p, g2p]
    lw, a_gate, gate = pl.pallas_call(
        functools.partial(_rwkv_lora_kernel, tm=tm, seq=seq),
        out_shape=(jax.ShapeDtypeStruct((t, d), F32),) * 3,
        grid=(t // tm,),
        in_specs=[tok, prev_spec()] + [full(a) for a in lora_in[2:]],
        out_specs=(tok, tok, tok),
        compiler_params=_cparams("parallel"),
        name="rwkv_lora",
    )(*lora_in)

    tc = _tile(seq, 512)
    n_tiles = seq // tc
    tokc = lambda sel: pl.BlockSpec((tc, LANES), lambda b, p, c: (b * n_tiles + c, p))
    rkvc = lambda which: pl.BlockSpec((None, tc, LANES), lambda b, p, c: (which, b * n_tiles + c, p))
    par = pl.BlockSpec((1, LANES), lambda b, p, c: (0, p))
    mixed = pl.pallas_call(
        _rwkv_core_kernel,
        out_shape=jax.ShapeDtypeStruct((t, d), BF16),
        grid=(batch, d // LANES, n_tiles),
        in_specs=[rkvc(0), rkvc(1), rkvc(2), tokc(0), tokc(0), tokc(0), par, par, par, par, par],
        out_specs=tokc(0),
        scratch_shapes=[pltpu.VMEM((LANES, LANES), F32)] + [pltpu.VMEM((tc, LANES), F32)] * 4,
        compiler_params=_cparams("parallel", "parallel", "arbitrary"),
        name="rwkv_chunked_state",
    )(rkv, rkv, rkv, lw, a_gate, gate, row(k_k), row(k_a), row(r_k), row(ln_w), row(ln_b))
    return matmul_residual(mixed, w_o, x)


def kernel(x, norm_mix, norm_mlp, norm_f, attn_w_qkv, attn_w_o, ssm_w_in, ssm_log_dt, ssm_a_re, ssm_a_im,
           ssm_b_re, ssm_b_im, ssm_c_re, ssm_c_im, ssm_d, ssm_w_out, rwkv_mu, rwkv_w_rkv, rwkv_w0, rwkv_w1,
           rwkv_w2, rwkv_a0, rwkv_a1, rwkv_a2, rwkv_g1, rwkv_g2, rwkv_k_k, rwkv_k_a, rwkv_r_k, rwkv_ln_w,
           rwkv_ln_b, rwkv_w_o, mlp_w1, mlp_w2):
    batch, seq, d = x.shape
    depth = norm_mix.shape[0]
    bf = lambda w: w.astype(BF16)
    h = x.reshape(batch * seq, d)
    ia = ib = ic = 0
    for layer in range(depth):
        kind = layer % 3
        if kind == 0:
            h = attention_layer(h, norm_mix[layer], bf(attn_w_qkv[ia]), bf(attn_w_o[ia]), batch, seq)
            ia += 1
        elif kind == 1:
            h = s5_layer(h, norm_mix[layer], bf(ssm_w_in[ib]), ssm_log_dt[ib], ssm_a_re[ib], ssm_a_im[ib],
                         ssm_b_re[ib], ssm_b_im[ib], ssm_c_re[ib], ssm_c_im[ib], ssm_d[ib],
                         bf(ssm_w_out[ib]), batch, seq)
            ib += 1
        else:
            h = rwkv_layer(h, norm_mix[layer], rwkv_mu[ic], bf(rwkv_w_rkv[ic]), rwkv_w0[ic], rwkv_w1[ic],
                           rwkv_w2[ic], rwkv_a0[ic], rwkv_a1[ic], rwkv_a2[ic], rwkv_g1[ic], rwkv_g2[ic],
                           rwkv_k_k[ic], rwkv_k_a[ic], rwkv_r_k[ic], rwkv_ln_w[ic], rwkv_ln_b[ic],
                           bf(rwkv_w_o[ic]), batch, seq)
            ic += 1
        g_final = norm_f if layer == depth - 1 else None
        h = mlp_residual(h, norm_mlp[layer], bf(mlp_w1[layer]), bf(mlp_w2[layer]), g_final)
    return h.reshape(batch, seq, d)
```

```python
import functools

import jax
import jax.numpy as jnp
from jax import lax
from jax.experimental import pallas as pl
from jax.experimental.pallas import tpu as pltpu

F32 = jnp.float32
BF16 = jnp.bfloat16
HIGHEST = lax.Precision.HIGHEST

NORM_EPS = 1e-5
LANES = 128
VMEM_LIMIT_BYTES = 56 * 2**20
MASK_VALUE = -1e30

ATTN_PATTERNS = ((128, 1), (512, 4), (2048, 16))
ATTN_BLOCK = 128
ATTN_HEAD_DIM = 128
SSM_CH = 16
SSM_CHUNK = 16
SSM_DT_MIN = 0.001
SSM_DT_MAX = 0.1
RWKV_HEAD_DIM = 64
RWKV_CHUNK = 64
RWKV_GN_EPS = RWKV_HEAD_DIM * 1e-5


def _cparams(*sem):
    return pltpu.CompilerParams(dimension_semantics=sem, vmem_limit_bytes=VMEM_LIMIT_BYTES)


def _tile(n, pref):
    t = min(n, pref)
    while n % t:
        t //= 2
    return t


def _rms(x, g):
    ms = jnp.mean(x * x, axis=-1, keepdims=True)
    return x * lax.rsqrt(ms + NORM_EPS) * g


def _dot(a, b):
    return jnp.dot(a, b, preferred_element_type=F32)


def _dot_nt(a, b):
    return lax.dot_general(a, b, (((1,), (1,)), ((), ())), preferred_element_type=F32)


def _dot_tn(a, b):
    return lax.dot_general(a, b, (((0,), (0,)), ((), ())), preferred_element_type=F32)


def _dot_f32(a, b):
    return jnp.dot(a, b, precision=HIGHEST, preferred_element_type=F32)


def _split3(x):
    hi = x.astype(BF16)
    r1 = x - hi.astype(F32)
    mid = r1.astype(BF16)
    lo = (r1 - mid.astype(F32)).astype(BF16)
    return hi, mid, lo


def _norm_matmul_kernel(x_ref, g_ref, w_ref, o_ref, h_ref):
    @pl.when(pl.program_id(1) == 0)
    def _():
        h_ref[...] = _rms(x_ref[...], g_ref[...]).astype(BF16)

    o_ref[...] = _dot(h_ref[...], w_ref[...]).astype(o_ref.dtype)


def norm_matmul(x, g, w, out_dtype, tm=512, tn=1024):
    t, d = x.shape
    n = w.shape[1]
    tm, tn = _tile(t, tm), _tile(n, tn)
    return pl.pallas_call(
        _norm_matmul_kernel,
        out_shape=jax.ShapeDtypeStruct((t, n), out_dtype),
        grid=(t // tm, n // tn),
        in_specs=[pl.BlockSpec((tm, d), lambda i, j: (i, 0)),
                  pl.BlockSpec((1, d), lambda i, j: (0, 0)),
                  pl.BlockSpec((d, tn), lambda i, j: (0, j))],
        out_specs=pl.BlockSpec((tm, tn), lambda i, j: (i, j)),
        scratch_shapes=[pltpu.VMEM((tm, d), BF16)],
        compiler_params=_cparams("parallel", "arbitrary"),
        name="norm_matmul",
    )(x, g.reshape(1, d), w)


def _norm_matmul_strided_kernel(x_ref, g_ref, w_ref, o_ref, h_ref, *, dilation):
    @pl.when(pl.program_id(1) == 0)
    def _():
        h = _rms(x_ref[...], g_ref[...]).astype(BF16)
        if dilation > 1:
            tm = h.shape[0]
            per = tm // dilation
            new = lax.broadcasted_iota(jnp.int32, (tm, tm), 0)
            old = lax.broadcasted_iota(jnp.int32, (tm, tm), 1)
            perm = jnp.where(old == (new % per) * dilation + new // per, 1.0, 0.0).astype(BF16)
            h = _dot(perm, h).astype(BF16)
        h_ref[...] = h

    o_ref[...] = _dot(h_ref[...], w_ref[...]).astype(o_ref.dtype).reshape(o_ref.shape)


def norm_matmul_strided(x, g, w, col0, ncols, dilation, batch, seq, tm=512, tn=1024):
    t, d = x.shape
    tm, tn = _tile(seq, tm), _tile(ncols, tn)
    assert col0 % tn == 0 and (tm // dilation) % 16 == 0
    nt = seq // tm
    return pl.pallas_call(
        functools.partial(_norm_matmul_strided_kernel, dilation=dilation),
        out_shape=jax.ShapeDtypeStruct((batch, dilation, seq // dilation, ncols), BF16),
        grid=(t // tm, ncols // tn),
        in_specs=[pl.BlockSpec((tm, d), lambda i, j: (i, 0)),
                  pl.BlockSpec((1, d), lambda i, j: (0, 0)),
                  pl.BlockSpec((d, tn), lambda i, j: (0, col0 // tn + j))],
        out_specs=pl.BlockSpec((None, dilation, tm // dilation, tn), lambda i, j: (i // nt, 0, i % nt, j)),
        scratch_shapes=[pltpu.VMEM((tm, d), BF16)],
        compiler_params=_cparams("parallel", "arbitrary"),
        name=f"norm_matmul_stride{dilation}",
    )(x, g.reshape(1, d), w)


def _matmul_res_kernel(a_ref, w_ref, r_ref, o_ref):
    o_ref[...] = r_ref[...] + _dot(a_ref[...], w_ref[...])


def matmul_residual(a, w, res, tm=512, tn=1024):
    t, k = a.shape
    n = w.shape[1]
    tm, tn = _tile(t, tm), _tile(n, tn)
    return pl.pallas_call(
        _matmul_res_kernel,
        out_shape=jax.ShapeDtypeStruct((t, n), F32),
        grid=(t // tm, n // tn),
        in_specs=[pl.BlockSpec((tm, k), lambda i, j: (i, 0)),
                  pl.BlockSpec((k, tn), lambda i, j: (0, j)),
                  pl.BlockSpec((tm, tn), lambda i, j: (i, j))],
        out_specs=pl.BlockSpec((tm, tn), lambda i, j: (i, j)),
        compiler_params=_cparams("parallel", "parallel"),
        name="matmul_residual",
    )(a, w, res)


def _mlp_kernel(x_ref, g_ref, w1_ref, w2_ref, gf_ref, o_ref, h_ref, *, final_norm):
    f = pl.program_id(1)

    @pl.when(f == 0)
    def _():
        x = x_ref[...]
        h_ref[...] = _rms(x, g_ref[...]).astype(BF16)
        o_ref[...] = x

    a = _dot(h_ref[...], w1_ref[...])
    a = jnp.square(jnp.maximum(a, 0.0)).astype(BF16)
    o_ref[...] += _dot(a, w2_ref[...])

    if final_norm:
        @pl.when(f == pl.num_programs(1) - 1)
        def _():
            o_ref[...] = _rms(o_ref[...], gf_ref[...])


def mlp_residual(x, g, w1, w2, g_final=None, tm=512, tf=512):
    t, d = x.shape
    ff = w1.shape[1]
    tm, tf = _tile(t, tm), _tile(ff, tf)
    final_norm = g_final is not None
    gf = (g_final if final_norm else g).reshape(1, d)
    return pl.pallas_call(
        functools.partial(_mlp_kernel, final_norm=final_norm),
        out_shape=jax.ShapeDtypeStruct((t, d), F32),
        grid=(t // tm, ff // tf),
        in_specs=[pl.BlockSpec((tm, d), lambda i, f: (i, 0)),
                  pl.BlockSpec((1, d), lambda i, f: (0, 0)),
                  pl.BlockSpec((d, tf), lambda i, f: (0, f)),
                  pl.BlockSpec((tf, d), lambda i, f: (f, 0)),
                  pl.BlockSpec((1, d), lambda i, f: (0, 0))],
        out_specs=pl.BlockSpec((tm, d), lambda i, f: (i, 0)),
        scratch_shapes=[pltpu.VMEM((tm, d), BF16)],
        compiler_params=_cparams("parallel", "arbitrary"),
        name="mlp_residual",
    )(x, g.reshape(1, d), w1, w2, gf)


def _attn_kernel(slope_ref, q_ref, kp_ref, kc_ref, vp_ref, vc_ref, o_ref, lse_ref, *, heads, scale):
    blk, e = ATTN_BLOCK, ATTN_HEAD_DIM
    j = pl.program_id(2)
    qi = lax.broadcasted_iota(jnp.int32, (blk, blk), 0)
    kj = lax.broadcasted_iota(jnp.int32, (blk, blk), 1)
    dist_c = (qi - kj).astype(F32)
    dist_p = dist_c + float(blk)
    valid_c = kj <= qi
    valid_p = (kj >= qi) & (j > 0)
    lane = lax.broadcasted_iota(jnp.int32, (blk, LANES), 1)
    lse_tile = jnp.zeros((blk, LANES), F32)
    for h in range(heads):
        sl = slice(h * e, (h + 1) * e)
        q = q_ref[:, sl]
        slope = slope_ref[h]
        sc = jnp.where(valid_c, _dot_nt(q, kc_ref[:, sl]) * scale - slope * dist_c, MASK_VALUE)
        sp = jnp.where(valid_p, _dot_nt(q, kp_ref[:, sl]) * scale - slope * dist_p, MASK_VALUE)
        m = jnp.maximum(jnp.max(sc, axis=-1, keepdims=True), jnp.max(sp, axis=-1, keepdims=True))
        pc = jnp.exp(sc - m)
        pp = jnp.exp(sp - m)
        den = jnp.sum(pc, axis=-1, keepdims=True) + jnp.sum(pp, axis=-1, keepdims=True)
        o = _dot(pc.astype(BF16), vc_ref[:, sl]) + _dot(pp.astype(BF16), vp_ref[:, sl])
        o_ref[:, sl] = (o / den).astype(o_ref.dtype)
        lse_tile = jnp.where(lane == h, m + jnp.log(den), lse_tile)
    lse_ref[...] = lse_tile


def _attn_group(qkv, slopes, group, dilation, batch, seq, heads):
    e, blk = ATTN_HEAD_DIM, ATTN_BLOCK
    he = heads * e
    sub = seq // dilation
    nb = sub // blk
    cur = lambda kind: pl.BlockSpec((None, None, blk, he), lambda b, r, j: (b, r, j, kind))
    prev = lambda kind: pl.BlockSpec((None, None, blk, he), lambda b, r, j: (b, r, jnp.maximum(j - 1, 0), kind))
    out, lse = pl.pallas_call(
        functools.partial(_attn_kernel, heads=heads, scale=e ** -0.5),
        out_shape=(jax.ShapeDtypeStruct((batch, dilation, sub, he), BF16),
                   jax.ShapeDtypeStruct((batch, dilation, sub, LANES), F32)),
        grid=(batch, dilation, nb),
        in_specs=[pl.BlockSpec(memory_space=pltpu.SMEM), cur(0), prev(1), cur(1), prev(2), cur(2)],
        out_specs=(pl.BlockSpec((None, None, blk, he), lambda b, r, j: (b, r, j, 0)),
                   pl.BlockSpec((None, None, blk, LANES), lambda b, r, j: (b, r, j, 0))),
        compiler_params=_cparams("parallel", "parallel", "arbitrary"),
        name=f"dilated_attn_g{group}",
    )(slopes, qkv, qkv, qkv, qkv, qkv)
    natural = lambda a: a.transpose(0, 2, 1, 3).reshape(batch * seq, a.shape[-1])
    return natural(out), natural(lse)


def _attn_out_kernel(o0_ref, o1_ref, o2_ref, l0_ref, l1_ref, l2_ref, w_ref, r_ref, out_ref, m_ref, *, heads):
    e = ATTN_HEAD_DIM

    @pl.when(pl.program_id(1) == 0)
    def _():
        l0, l1, l2 = l0_ref[...], l1_ref[...], l2_ref[...]
        mx = jnp.maximum(jnp.maximum(l0, l1), l2)
        e0, e1, e2 = jnp.exp(l0 - mx), jnp.exp(l1 - mx), jnp.exp(l2 - mx)
        inv = 1.0 / (e0 + e1 + e2)
        w0, w1, w2 = e0 * inv, e1 * inv, e2 * inv
        for h in range(heads):
            sl = slice(h * e, (h + 1) * e)
            acc = (w0[:, h:h + 1] * o0_ref[:, sl].astype(F32)
                   + w1[:, h:h + 1] * o1_ref[:, sl].astype(F32)
                   + w2[:, h:h + 1] * o2_ref[:, sl].astype(F32))
            m_ref[:, sl] = acc.astype(BF16)

    out_ref[...] = r_ref[...] + _dot(m_ref[...], w_ref[...])


def attention_layer(x, g, w_qkv, w_o, batch, seq):
    t, d = x.shape
    n_dil = len(ATTN_PATTERNS)
    he = w_o.shape[0]
    heads = he // ATTN_HEAD_DIM
    n_sl = n_dil * heads
    slopes = (2.0 ** (-8.0 * jnp.arange(1, n_sl + 1, dtype=F32) / n_sl)).reshape(n_dil, heads)
    outs, lses = [], []
    for grp, (window, dilation) in enumerate(ATTN_PATTERNS):
        assert window // dilation == ATTN_BLOCK and (seq // dilation) % ATTN_BLOCK == 0
        qkv = norm_matmul_strided(x, g, w_qkv, grp * 3 * he, 3 * he, dilation, batch, seq)
        o, l = _attn_group(qkv, slopes[grp] * dilation, grp, dilation, batch, seq, heads)
        outs.append(o)
        lses.append(l)
    tm, tn = _tile(t, 512), _tile(d, 1024)
    ospec = pl.BlockSpec((tm, he), lambda i, j: (i, 0))
    lspec = pl.BlockSpec((tm, LANES), lambda i, j: (i, 0))
    return pl.pallas_call(
        functools.partial(_attn_out_kernel, heads=heads),
        out_shape=jax.ShapeDtypeStruct((t, d), F32),
        grid=(t // tm, d // tn),
        in_specs=[ospec, ospec, ospec, lspec, lspec, lspec,
                  pl.BlockSpec((he, tn), lambda i, j: (0, j)),
                  pl.BlockSpec((tm, tn), lambda i, j: (i, j))],
        out_specs=pl.BlockSpec((tm, tn), lambda i, j: (i, j)),
        scratch_shapes=[pltpu.VMEM((tm, he), BF16)],
        compiler_params=_cparams("parallel", "arbitrary"),
        name="attn_merge_out_proj",
    )(*outs, *lses, w_o, x)


def _s5_chunk_operators(log_dt, a_re, a_im, b_re, b_im, c_re, c_im, d_skip, chunk):
    hp = dict(precision=HIGHEST)
    n_g, n_p = a_re.shape
    n_c = b_re.shape[-1]
    dt = jnp.exp(log_dt)[:, None]
    mag = jnp.exp(dt * a_re)
    ab_re = mag * jnp.cos(dt * a_im)
    ab_im = mag * jnp.sin(dt * a_im)
    den = a_re * a_re + a_im * a_im
    zr = ab_re - 1.0
    cr = (zr * a_re + ab_im * a_im) / den
    ci = (ab_im * a_re - zr * a_im) / den
    bb_re = cr[..., None] * b_re - ci[..., None] * b_im
    bb_im = cr[..., None] * b_im + ci[..., None] * b_re
    pr, pi = [jnp.ones_like(ab_re)], [jnp.zeros_like(ab_re)]
    for _ in range(chunk):
        pr, pi = pr + [pr[-1] * ab_re - pi[-1] * ab_im], pi + [pr[-1] * ab_im + pi[-1] * ab_re]
    pr, pi = jnp.stack(pr), jnp.stack(pi)
    ce_re = c_re[None] * pr[:, :, None, :] - c_im[None] * pi[:, :, None, :]
    ce_im = c_re[None] * pi[:, :, None, :] + c_im[None] * pr[:, :, None, :]
    kern = (jnp.einsum('tgcp,gpd->tgcd', ce_re[:chunk], bb_re, **hp)
            - jnp.einsum('tgcp,gpd->tgcd', ce_im[:chunk], bb_im, **hp))
    kern = kern.at[0].add(jax.vmap(jnp.diag)(d_skip))
    s_idx = jnp.arange(chunk)[:, None]
    t_idx = jnp.arange(chunk)[None, :]
    lag = t_idx - s_idx
    m_op = jnp.where((lag >= 0)[:, :, None, None, None], kern[jnp.maximum(lag, 0)], 0.0)
    m_op = m_op.transpose(2, 0, 4, 1, 3).reshape(n_g, chunk * n_c, chunk * n_c)
    qr, qi = pr[chunk - 1::-1][:chunk], pi[chunk - 1::-1][:chunk]
    bo_re = qr[..., None] * bb_re[None] - qi[..., None] * bb_im[None]
    bo_im = qr[..., None] * bb_im[None] + qi[..., None] * bb_re[None]
    flat_b = lambda a: a.transpose(1, 0, 3, 2).reshape(n_g, chunk * n_c, n_p)
    bo_re, bo_im = flat_b(bo_re), flat_b(bo_im)
    b_op = jnp.concatenate([bo_re, bo_im, bo_im, bo_re], axis=-1)
    flat_c = lambda a: a.transpose(1, 3, 0, 2).reshape(n_g, n_p, chunk * n_c)
    c_op = jnp.concatenate([flat_c(ce_re[1:]), -flat_c(ce_im[1:])], axis=1)
    al_re, al_im = pr[chunk], pi[chunk]
    coef_same = jnp.concatenate([al_re, al_re, al_re, al_re], axis=-1)
    coef_cross = jnp.concatenate([-al_im, al_im, al_im, -al_im], axis=-1)
    return m_op, b_op, c_op, coef_same, coef_cross


def _s5_in_kernel(u_ref, b_ref, o_ref):
    o_ref[...] = jnp.einsum('gnk,gkp->gnp', u_ref[...], b_ref[...], preferred_element_type=F32)


def _s5_scan_kernel(xin_ref, cs_ref, cc_ref, o_ref, st_ref, *, half):
    @pl.when(pl.program_id(0) == 0)
    def _():
        st_ref[...] = jnp.zeros_like(st_ref)

    cs, cc = cs_ref[...], cc_ref[...]

    def step(n, st):
        o_ref[:, n] = st[:, :, :half].astype(o_ref.dtype)
        st_sw = jnp.concatenate([st[:, :, half:], st[:, :, :half]], axis=-1)
        return cs * st + cc * st_sw + xin_ref[:, n]

    st_ref[...] = lax.fori_loop(0, xin_ref.shape[1], step, st_ref[...])


def _s5_out_kernel(u_ref, m_ref, xp_ref, c_ref, o_ref):
    y = (jnp.einsum('gnk,gkm->gnm', u_ref[...], m_ref[...], preferred_element_type=F32)
         + jnp.einsum('gnp,gpm->gnm', xp_ref[...], c_ref[...], preferred_element_type=F32))
    o_ref[...] = jax.nn.gelu(y).astype(o_ref.dtype)


def _glu_out_kernel(y_ref, wa_ref, wb_ref, r_ref, o_ref):
    y = y_ref[...]
    o_ref[...] = r_ref[...] + _dot(y, wa_ref[...]) * jax.nn.sigmoid(_dot(y, wb_ref[...]))


def s5_layer(x, g, w_in, log_dt, a_re, a_im, b_re, b_im, c_re, c_im, d_skip, w_out, batch, seq):
    t, d = x.shape
    n_g, n_p = a_re.shape
    n_c = SSM_CH
    ck = SSM_CHUNK
    kdim = ck * n_c
    n_rows = t // ck
    m_op, b_op, c_op, coef_same, coef_cross = _s5_chunk_operators(
        log_dt, a_re, a_im, b_re, b_im, c_re, c_im, d_skip, ck)
    u = norm_matmul(x, g, w_in, BF16)
    ug = u.reshape(n_rows, ck, n_g, n_c).transpose(2, 0, 1, 3).reshape(n_g, n_rows, kdim)
    gb = _tile(n_g, 8)
    tr = _tile(n_rows, 1024)
    xin = pl.pallas_call(
        _s5_in_kernel,
        out_shape=jax.ShapeDtypeStruct((n_g, n_rows, 4 * n_p), F32),
        grid=(n_g // gb, n_rows // tr),
        in_specs=[pl.BlockSpec((gb, tr, kdim), lambda i, j: (i, j, 0)),
                  pl.BlockSpec((gb, kdim, 4 * n_p), lambda i, j: (i, 0, 0))],
        out_specs=pl.BlockSpec((gb, tr, 4 * n_p), lambda i, j: (i, j, 0)),
        compiler_params=_cparams("parallel", "parallel"),
        name="s5_chunk_inputs",
    )(ug, b_op.astype(BF16))
    n_chunks = seq // ck
    xin_t = xin.transpose(1, 0, 2).reshape(batch, n_chunks, n_g, 4 * n_p)
    tn = _tile(n_chunks, 32)
    xprev = pl.pallas_call(
        functools.partial(_s5_scan_kernel, half=2 * n_p),
        out_shape=jax.ShapeDtypeStruct((batch, n_chunks, n_g, 2 * n_p), BF16),
        grid=(n_chunks // tn,),
        in_specs=[pl.BlockSpec((batch, tn, n_g, 4 * n_p), lambda i: (0, i, 0, 0)),
                  pl.BlockSpec((n_g, 4 * n_p), lambda i: (0, 0)),
                  pl.BlockSpec((n_g, 4 * n_p), lambda i: (0, 0))],
        out_specs=pl.BlockSpec((batch, tn, n_g, 2 * n_p), lambda i: (0, i, 0, 0)),
        scratch_shapes=[pltpu.VMEM((batch, n_g, 4 * n_p), F32)],
        compiler_params=_cparams("arbitrary"),
        name="s5_chunk_scan",
    )(xin_t, coef_same, coef_cross)
    xprev_g = xprev.reshape(n_rows, n_g, 2 * n_p).transpose(1, 0, 2)
    yg = pl.pallas_call(
        _s5_out_kernel,
        out_shape=jax.ShapeDtypeStruct((n_g, n_rows, kdim), BF16),
        grid=(n_g // gb, n_rows // tr),
        in_specs=[pl.BlockSpec((gb, tr, kdim), lambda i, j: (i, j, 0)),
                  pl.BlockSpec((gb, kdim, kdim), lambda i, j: (i, 0, 0)),
                  pl.BlockSpec((gb, tr, 2 * n_p), lambda i, j: (i, j, 0)),
                  pl.BlockSpec((gb, 2 * n_p, kdim), lambda i, j: (i, 0, 0))],
        out_specs=pl.BlockSpec((gb, tr, kdim), lambda i, j: (i, j, 0)),
        compiler_params=_cparams("parallel", "parallel"),
        name="s5_chunk_outputs",
    )(ug, m_op.astype(BF16), xprev_g, c_op.astype(BF16))
    y = yg.reshape(n_g, n_rows, ck, n_c).transpose(1, 2, 0, 3).reshape(t, n_g * n_c)
    tm, tn2 = _tile(t, 512), _tile(d, 512)
    nj = d // tn2
    return pl.pallas_call(
        _glu_out_kernel,
        out_shape=jax.ShapeDtypeStruct((t, d), F32),
        grid=(t // tm, nj),
        in_specs=[pl.BlockSpec((tm, n_g * n_c), lambda i, j: (i, 0)),
                  pl.BlockSpec((n_g * n_c, tn2), lambda i, j: (0, j)),
                  pl.BlockSpec((n_g * n_c, tn2), lambda i, j: (0, j + nj)),
                  pl.BlockSpec((tm, tn2), lambda i, j: (i, j))],
        out_specs=pl.BlockSpec((tm, tn2), lambda i, j: (i, j)),
        compiler_params=_cparams("parallel", "parallel"),
        name="s5_glu_out_proj",
    )(y, w_out, w_out, x)


def _shift_norm(x_ref, xp_ref, g_ref, first):
    g = g_ref[...]
    h = _rms(x_ref[...], g)
    prev = _rms(xp_ref[7:8, :], g)
    prev = jnp.where(first, 0.0, prev)
    row = lax.broadcasted_iota(jnp.int32, h.shape, 0)
    hp = jnp.where(row == 0, prev, pltpu.roll(h, 1, 0))
    return h, hp


def _rwkv_proj_kernel(x_ref, xp_ref, g_ref, mu_ref, w_ref, o_ref, h_ref, d_ref, l_ref, *, tm, seq):
    i, j, n = pl.program_id(0), pl.program_id(1), pl.program_id(2)

    @pl.when((j == 0) & (n == 0))
    def _():
        h, hp = _shift_norm(x_ref, xp_ref, g_ref, (i * tm) % seq == 0)
        h_ref[...] = h
        d_ref[...] = hp - h

    @pl.when(n == 0)
    def _():
        l_ref[...] = (h_ref[...] + d_ref[...] * mu_ref[...]).astype(BF16)

    o_ref[...] = _dot(l_ref[...], w_ref[...])


def _softplus(z):
    return jnp.maximum(z, 0.0) + jnp.log(1.0 + jnp.exp(-jnp.abs(z)))


def _rwkv_lora_kernel(x_ref, xp_ref, g_ref, mu_ref, w0_ref, w1_ref, w2_ref, a0_ref, a1_ref, a2_ref,
                      g1_ref, g2_ref, lw_ref, a_ref, gate_ref, *, tm, seq):
    h, hp = _shift_norm(x_ref, xp_ref, g_ref, (pl.program_id(0) * tm) % seq == 0)
    dlt = hp - h
    xw = (h + dlt * mu_ref[0:1, :]).astype(BF16)
    xa = (h + dlt * mu_ref[1:2, :]).astype(BF16)
    xg = (h + dlt * mu_ref[2:3, :]).astype(BF16)
    wl = w0_ref[...] + _dot(jnp.tanh(_dot(xw, w1_ref[...])).astype(BF16), w2_ref[...])
    w = -_softplus(-wl) - 0.5
    lw_ref[...] = -jnp.exp(w)
    a_ref[...] = jax.nn.sigmoid(a0_ref[...] + _dot(_dot(xa, a1_ref[...]).astype(BF16), a2_ref[...]))
    gate_ref[...] = _dot(jax.nn.sigmoid(_dot(xg, g1_ref[...])).astype(BF16), g2_ref[...])


def _rwkv_core_kernel(r_ref, k_ref, v_ref, lw_ref, a_ref, gate_ref, kk_ref, ka_ref, rk_ref, lnw_ref, lnb_ref,
                      o_ref, s_ref, lhs_ref, rhs_ref, bk_ref, v2_ref, dec_ref, y_ref):
    ck, hd = RWKV_CHUNK, RWKV_HEAD_DIM
    nb, tc, width = r_ref.shape
    nch, ck2 = tc // ck, 2 * ck
    seqs = [(bi, slice(pi * LANES, (pi + 1) * LANES)) for bi in range(nb) for pi in range(width // LANES)]

    @pl.when(pl.program_id(1) == 0)
    def _():
        s_ref[...] = jnp.zeros_like(s_ref)

    lane = lax.broadcasted_iota(jnp.int32, (1, 1, LANES), 2)
    head_a = lane < hd
    hrow = lax.broadcasted_iota(jnp.int32, (LANES, LANES), 0) // hd
    hcol = lax.broadcasted_iota(jnp.int32, (LANES, LANES), 1) // hd
    head_ones = jnp.where(hrow == hcol, 1.0, 0.0).astype(BF16)
    trow = lax.broadcasted_iota(jnp.int32, (tc, tc), 0)
    tcol = lax.broadcasted_iota(jnp.int32, (tc, tc), 1)
    chunk_tril = jnp.where((trow // ck == tcol // ck) & (tcol <= trow), 1.0, 0.0).astype(BF16)

    def exact_dot(m01, x):
        hi, mid, lo = _split3(x)
        return _dot(m01, hi) + _dot(m01, mid) + _dot(m01, lo)

    def head_sum(x):
        hi, mid, lo = _split3(x)
        return _dot(hi, head_ones) + _dot(mid, head_ones) + _dot(lo, head_ones)

    def stack_heads(x):
        return jnp.concatenate([jnp.where(head_a, x, 0.0), jnp.where(head_a, 0.0, x)], axis=1).astype(BF16)

    for si, (bi, ls) in enumerate(seqs):
        k_all, a_all = k_ref[bi, :, ls], a_ref[bi, :, ls]
        kk = k_all * kk_ref[:, ls]
        kk = kk / jnp.maximum(jnp.sqrt(head_sum(kk * kk)), 1e-12)
        k2 = k_all * (1.0 + (a_all - 1.0) * ka_ref[:, ls])
        lw = lw_ref[bi, :, ls]
        by_chunk = lambda x: x.reshape(nch, ck, LANES)
        cs = by_chunk(exact_dot(chunk_tril, lw))
        tot = cs[:, ck - 1:ck, :]
        gam_inv, gam_rem = jnp.exp(-cs), jnp.exp(tot - cs)
        atm = by_chunk(-kk) * jnp.exp(cs - by_chunk(lw))
        rm = by_chunk(r_ref[bi, :, ls]) * jnp.exp(cs)
        b3, k3 = by_chunk(kk * a_all), by_chunk(k2)
        lhs_ref[si] = jnp.concatenate([stack_heads(atm), stack_heads(rm)], axis=1)
        rhs_ref[si] = jnp.concatenate([stack_heads(b3 * gam_inv), stack_heads(k3 * gam_inv)], axis=1)
        bk_ref[si] = jnp.concatenate([stack_heads(b3 * gam_rem), stack_heads(k3 * gam_rem)], axis=1)
        v2_ref[si] = stack_heads(by_chunk(v_ref[bi, :, ls]))
        dec_ref[si] = jnp.exp(tot)

    row = lax.broadcasted_iota(jnp.int32, (ck2, ck2), 0)
    col = lax.broadcasted_iota(jnp.int32, (ck2, ck2), 1)
    incl = col <= row
    strict = col < row
    eye = jnp.where(row == col, 1.0, 0.0)
    n_seq = len(seqs)
    each = lambda f, *lists: [f(*args) for args in zip(*lists)]

    def chunk_step(ci, carry):
        lhs = [lhs_ref[si, ci] for si in range(n_seq)]
        gram = each(lambda l, si: _dot_nt(l, rhs_ref[si, ci]), lhs, range(n_seq))
        a_ab = each(lambda g: jnp.where(strict, g[:ck2, :ck2], 0.0), gram)
        a_lo = each(lambda g: jnp.concatenate([jnp.where(strict, g[:ck2, ck2:], 0.0),
                                               jnp.where(incl, g[ck2:, ck2:], 0.0)], axis=0).astype(BF16), gram)
        a_rb = each(lambda g: jnp.where(incl, g[ck2:, :ck2], 0.0).astype(BF16), gram)
        inv = each(lambda a: eye + a, a_ab)
        apow = a_ab
        span = 2
        while span < ck:
            apow = each(lambda a: _dot(a.astype(BF16), a.astype(BF16)), apow)
            inv = each(lambda t, a: t + _dot(t.astype(BF16), a.astype(BF16)), inv, apow)
            span *= 2
        v2 = [v2_ref[si, ci] for si in range(n_seq)]
        av = each(_dot, a_lo, v2)
        s = [s_ref[si] for si in range(n_seq)]
        xs = each(lambda l, st: _dot_nt(l, st.astype(BF16)), lhs, s)
        u = each(lambda t, x, w: _dot(t.astype(BF16), (x[:ck2] + w[:ck2]).astype(BF16)).astype(BF16), inv, xs, av)
        y2 = each(lambda x, w, arb, ub: x[ck2:] + w[ck2:] + _dot(arb, ub), xs, av, a_rb, u)
        sl = pl.ds(pl.multiple_of(ci * ck, ck), ck)
        for si, (bi, ls) in enumerate(seqs):
            y_ref[bi, sl, ls] = y2[si][:ck] + y2[si][ck:]
            uv = jnp.concatenate([u[si], v2[si]], axis=0)
            s_ref[si] = s[si] * dec_ref[si, ci] + _dot_tn(uv, bk_ref[si, ci])
        return carry

    lax.fori_loop(0, nch, chunk_step, 0)

    for bi, ls in seqs:
        y = y_ref[bi, :, ls]
        mean = head_sum(y) * (1.0 / hd)
        yc = y - mean
        var = head_sum(yc * yc) * (1.0 / hd)
        yn = yc * lax.rsqrt(var + RWKV_GN_EPS) * lnw_ref[:, ls] + lnb_ref[:, ls]
        k2 = k_ref[bi, :, ls] * (1.0 + (a_ref[bi, :, ls] - 1.0) * ka_ref[:, ls])
        bonus = head_sum(r_ref[bi, :, ls] * k2 * rk_ref[:, ls]) * v_ref[bi, :, ls]
        o_ref[bi, :, ls] = ((yn + bonus) * gate_ref[bi, :, ls]).astype(o_ref.dtype)


def _pad_to(a, axis, size):
    pad = [(0, 0)] * a.ndim
    pad[axis] = (0, size - a.shape[axis])
    return jnp.pad(a, pad)


def rwkv_layer(x, g, mu, w_rkv, w0, w1, w2, a0, a1, a2, g1, g2, k_k, k_a, r_k, ln_w, ln_b, w_o, batch, seq):
    t, d = x.shape
    row = lambda p: p.reshape(1, d).astype(F32)
    g2d = g.reshape(1, d)
    tm = _tile(seq, 256)
    tn = _tile(d, 1024)
    prev_spec = lambda: pl.BlockSpec((8, d), (lambda i, *_: (jnp.maximum(i * (tm // 8) - 1, 0), 0)))
    rkv = pl.pallas_call(
        functools.partial(_rwkv_proj_kernel, tm=tm, seq=seq),
        out_shape=jax.ShapeDtypeStruct((3, t, d), F32),
        grid=(t // tm, 3, d // tn),
        in_specs=[pl.BlockSpec((tm, d), lambda i, j, n: (i, 0)),
                  prev_spec(),
                  pl.BlockSpec((1, d), lambda i, j, n: (0, 0)),
                  pl.BlockSpec((None, 1, d), lambda i, j, n: (j, 0, 0)),
                  pl.BlockSpec((None, d, tn), lambda i, j, n: (j, 0, n))],
        out_specs=pl.BlockSpec((None, tm, tn), lambda i, j, n: (j, i, n)),
        scratch_shapes=[pltpu.VMEM((tm, d), F32), pltpu.VMEM((tm, d), F32), pltpu.VMEM((tm, d), BF16)],
        compiler_params=_cparams("parallel", "arbitrary", "arbitrary"),
        name="rwkv_rkv_proj",
    )(x, x, g2d, mu[:3].reshape(3, 1, d), w_rkv)

    pad_rank = lambda w_a, w_b: (_pad_to(w_a, 1, -(-w_a.shape[1] // LANES) * LANES).astype(BF16),
                                 _pad_to(w_b, 0, -(-w_b.shape[0] // LANES) * LANES).astype(BF16))
    w1p, w2p = pad_rank(w1, w2)
    a1p, a2p = pad_rank(a1, a2)
    g1p, g2p = pad_rank(g1, g2)
    full = lambda a: pl.BlockSpec(a.shape, lambda i: (0,) * a.ndim)
    tok = pl.BlockSpec((tm, d), lambda i: (i, 0))
    lora_in = [x, x, g2d, mu[3:6], row(w0), w1p, w2p, row(a0), a1p, a2p, g1p, g2p]
    lw, a_gate, gate = pl.pallas_call(
        functools.partial(_rwkv_lora_kernel, tm=tm, seq=seq),
        out_shape=(jax.ShapeDtypeStruct((t, d), F32),) * 3,
        grid=(t // tm,),
        in_specs=[tok, prev_spec()] + [full(a) for a in lora_in[2:]],
        out_specs=(tok, tok, tok),
        compiler_params=_cparams("parallel"),
        name="rwkv_lora",
    )(*lora_in)

    tc = _tile(seq, 512)
    wd = _tile(d, 2 * LANES)
    tokc = pl.BlockSpec((batch, tc, wd), lambda p, c: (0, c, p))
    rkvc = lambda which: pl.BlockSpec((None, batch, tc, wd), lambda p, c: (which, 0, c, p))
    par = pl.BlockSpec((1, wd), lambda p, c: (0, p))
    n_seq = batch * (wd // LANES)
    nch = tc // RWKV_CHUNK
    bsd = lambda a: a.reshape(batch, seq, d)
    rkv4 = rkv.reshape(3, batch, seq, d)
    mixed = pl.pallas_call(
        _rwkv_core_kernel,
        out_shape=jax.ShapeDtypeStruct((batch, seq, d), BF16),
        grid=(d // wd, seq // tc),
        in_specs=[rkvc(0), rkvc(1), rkvc(2), tokc, tokc, tokc, par, par, par, par, par],
        out_specs=tokc,
        scratch_shapes=[pltpu.VMEM((n_seq, LANES, LANES), F32)]
        + [pltpu.VMEM((n_seq, nch, 4 * RWKV_CHUNK, LANES), BF16)] * 3
        + [pltpu.VMEM((n_seq, nch, 2 * RWKV_CHUNK, LANES), BF16),
           pltpu.VMEM((n_seq, nch, 1, LANES), F32),
           pltpu.VMEM((batch, tc, wd), F32)],
        compiler_params=_cparams("parallel", "arbitrary"),
        name="rwkv_chunked_state",
    )(rkv4, rkv4, rkv4, bsd(lw), bsd(a_gate), bsd(gate), row(k_k), row(k_a), row(r_k), row(ln_w), row(ln_b))
    return matmul_residual(mixed.reshape(t, d), w_o, x)


def kernel(x, norm_mix, norm_mlp, norm_f, attn_w_qkv, attn_w_o, ssm_w_in, ssm_log_dt, ssm_a_re, ssm_a_im,
           ssm_b_re, ssm_b_im, ssm_c_re, ssm_c_im, ssm_d, ssm_w_out, rwkv_mu, rwkv_w_rkv, rwkv_w0, rwkv_w1,
           rwkv_w2, rwkv_a0, rwkv_a1, rwkv_a2, rwkv_g1, rwkv_g2, rwkv_k_k, rwkv_k_a, rwkv_r_k, rwkv_ln_w,
           rwkv_ln_b, rwkv_w_o, mlp_w1, mlp_w2):
    batch, seq, d = x.shape
    depth = norm_mix.shape[0]
    bf = lambda w: w.astype(BF16)
    h = x.reshape(batch * seq, d)
    ia = ib = ic = 0
    for layer in range(depth):
        kind = layer % 3
        if kind == 0:
            h = attention_layer(h, norm_mix[layer], bf(attn_w_qkv[ia]), bf(attn_w_o[ia]), batch, seq)
            ia += 1
        elif kind == 1:
            h = s5_layer(h, norm_mix[layer], bf(ssm_w_in[ib]), ssm_log_dt[ib], ssm_a_re[ib], ssm_a_im[ib],
                         ssm_b_re[ib], ssm_b_im[ib], ssm_c_re[ib], ssm_c_im[ib], ssm_d[ib],
                         bf(ssm_w_out[ib]), batch, seq)
            ib += 1
        else:
            h = rwkv_layer(h, norm_mix[layer], rwkv_mu[ic], bf(rwkv_w_rkv[ic]), rwkv_w0[ic], rwkv_w1[ic],
                           rwkv_w2[ic], rwkv_a0[ic], rwkv_a1[ic], rwkv_a2[ic], rwkv_g1[ic], rwkv_g2[ic],
                           rwkv_k_k[ic], rwkv_k_a[ic], rwkv_r_k[ic], rwkv_ln_w[ic], rwkv_ln_b[ic],
                           bf(rwkv_w_o[ic]), batch, seq)
            ic += 1
        g_final = norm_f if layer == depth - 1 else None
        h = mlp_residual(h, norm_mlp[layer], bf(mlp_w1[layer]), bf(mlp_w2[layer]), g_final)
    return h.reshape(batch, seq, d)
```

```python
import functools

import jax
import jax.numpy as jnp
from jax import lax
from jax.experimental import pallas as pl
from jax.experimental.pallas import tpu as pltpu

F32 = jnp.float32
BF16 = jnp.bfloat16
HIGHEST = lax.Precision.HIGHEST

NORM_EPS = 1e-5
LANES = 128
VMEM_LIMIT_BYTES = 56 * 2**20
MASK_VALUE = -1e30

ATTN_PATTERNS = ((128, 1), (512, 4), (2048, 16))
ATTN_BLOCK = 128
ATTN_HEAD_DIM = 128
SSM_CH = 16
SSM_CHUNK = 16
SSM_DT_MIN = 0.001
SSM_DT_MAX = 0.1
RWKV_HEAD_DIM = 64
RWKV_CHUNK = 64
RWKV_GN_EPS = RWKV_HEAD_DIM * 1e-5


def _cparams(*sem):
    return pltpu.CompilerParams(dimension_semantics=sem, vmem_limit_bytes=VMEM_LIMIT_BYTES)


def _tile(n, pref):
    t = min(n, pref)
    while n % t:
        t //= 2
    return t


def _rms(x, g):
    ms = jnp.mean(x * x, axis=-1, keepdims=True)
    return x * lax.rsqrt(ms + NORM_EPS) * g


def _dot(a, b):
    return jnp.dot(a, b, preferred_element_type=F32)


def _dot_nt(a, b):
    return lax.dot_general(a, b, (((1,), (1,)), ((), ())), preferred_element_type=F32)


def _dot_tn(a, b):
    return lax.dot_general(a, b, (((0,), (0,)), ((), ())), preferred_element_type=F32)


def _dot_f32(a, b):
    return jnp.dot(a, b, precision=HIGHEST, preferred_element_type=F32)


def _split3(x):
    hi = x.astype(BF16)
    r1 = x - hi.astype(F32)
    mid = r1.astype(BF16)
    lo = (r1 - mid.astype(F32)).astype(BF16)
    return hi, mid, lo


def _norm_matmul_strided_kernel(x_ref, g_ref, w_ref, o_ref, h_ref, *, dilation):
    @pl.when(pl.program_id(1) == 0)
    def _():
        h = _rms(x_ref[...], g_ref[...]).astype(BF16)
        if dilation > 1:
            tm = h.shape[0]
            per = tm // dilation
            new = lax.broadcasted_iota(jnp.int32, (tm, tm), 0)
            old = lax.broadcasted_iota(jnp.int32, (tm, tm), 1)
            perm = jnp.where(old == (new % per) * dilation + new // per, 1.0, 0.0).astype(BF16)
            h = _dot(perm, h).astype(BF16)
        h_ref[...] = h

    o_ref[...] = _dot(h_ref[...], w_ref[...]).astype(o_ref.dtype).reshape(o_ref.shape)


def norm_matmul_strided(x, g, w, col0, ncols, dilation, batch, seq, tm=512, tn=1024):
    t, d = x.shape
    tm, tn = _tile(seq, tm), _tile(ncols, tn)
    assert col0 % tn == 0 and (tm // dilation) % 16 == 0
    nt = seq // tm
    return pl.pallas_call(
        functools.partial(_norm_matmul_strided_kernel, dilation=dilation),
        out_shape=jax.ShapeDtypeStruct((batch, dilation, seq // dilation, ncols), BF16),
        grid=(t // tm, ncols // tn),
        in_specs=[pl.BlockSpec((tm, d), lambda i, j: (i, 0)),
                  pl.BlockSpec((1, d), lambda i, j: (0, 0)),
                  pl.BlockSpec((d, tn), lambda i, j: (0, col0 // tn + j))],
        out_specs=pl.BlockSpec((None, dilation, tm // dilation, tn), lambda i, j: (i // nt, 0, i % nt, j)),
        scratch_shapes=[pltpu.VMEM((tm, d), BF16)],
        compiler_params=_cparams("parallel", "arbitrary"),
        name=f"norm_matmul_stride{dilation}",
    )(x, g.reshape(1, d), w)


def _matmul_res_kernel(a_ref, w_ref, r_ref, o_ref):
    o_ref[...] = r_ref[...] + _dot(a_ref[...], w_ref[...])


def matmul_residual(a, w, res, tm=512, tn=1024):
    t, k = a.shape
    n = w.shape[1]
    tm, tn = _tile(t, tm), _tile(n, tn)
    return pl.pallas_call(
        _matmul_res_kernel,
        out_shape=jax.ShapeDtypeStruct((t, n), F32),
        grid=(t // tm, n // tn),
        in_specs=[pl.BlockSpec((tm, k), lambda i, j: (i, 0)),
                  pl.BlockSpec((k, tn), lambda i, j: (0, j)),
                  pl.BlockSpec((tm, tn), lambda i, j: (i, j))],
        out_specs=pl.BlockSpec((tm, tn), lambda i, j: (i, j)),
        compiler_params=_cparams("parallel", "parallel"),
        name="matmul_residual",
    )(a, w, res)


def _mlp_kernel(x_ref, g_ref, w1_ref, w2_ref, gf_ref, o_ref, h_ref, *, final_norm):
    f = pl.program_id(1)

    @pl.when(f == 0)
    def _():
        x = x_ref[...]
        h_ref[...] = _rms(x, g_ref[...]).astype(BF16)
        o_ref[...] = x

    a = _dot(h_ref[...], w1_ref[...])
    a = jnp.square(jnp.maximum(a, 0.0)).astype(BF16)
    o_ref[...] += _dot(a, w2_ref[...])

    if final_norm:
        @pl.when(f == pl.num_programs(1) - 1)
        def _():
            o_ref[...] = _rms(o_ref[...], gf_ref[...])


def mlp_residual(x, g, w1, w2, g_final=None, tm=512, tf=1024):
    t, d = x.shape
    ff = w1.shape[1]
    tm, tf = _tile(t, tm), _tile(ff, tf)
    final_norm = g_final is not None
    gf = (g_final if final_norm else g).reshape(1, d)
    return pl.pallas_call(
        functools.partial(_mlp_kernel, final_norm=final_norm),
        out_shape=jax.ShapeDtypeStruct((t, d), F32),
        grid=(t // tm, ff // tf),
        in_specs=[pl.BlockSpec((tm, d), lambda i, f: (i, 0)),
                  pl.BlockSpec((1, d), lambda i, f: (0, 0)),
                  pl.BlockSpec((d, tf), lambda i, f: (0, f)),
                  pl.BlockSpec((tf, d), lambda i, f: (f, 0)),
                  pl.BlockSpec((1, d), lambda i, f: (0, 0))],
        out_specs=pl.BlockSpec((tm, d), lambda i, f: (i, 0)),
        scratch_shapes=[pltpu.VMEM((tm, d), BF16)],
        compiler_params=_cparams("parallel", "arbitrary"),
        name="mlp_residual",
    )(x, g.reshape(1, d), w1, w2, gf)


def _attn_kernel(slope_ref, q_ref, kp_ref, kc_ref, vp_ref, vc_ref, o_ref, lse_ref, *, heads, scale):
    blk, e = ATTN_BLOCK, ATTN_HEAD_DIM
    j = pl.program_id(2)
    qi = lax.broadcasted_iota(jnp.int32, (blk, blk), 0)
    kj = lax.broadcasted_iota(jnp.int32, (blk, blk), 1)
    dist_c = (qi - kj).astype(F32)
    dist_p = dist_c + float(blk)
    valid_c = kj <= qi
    valid_p = (kj >= qi) & (j > 0)
    lane = lax.broadcasted_iota(jnp.int32, (blk, LANES), 1)
    lse_tile = jnp.zeros((blk, LANES), F32)
    together = next(n for n in (4, 2, 1) if heads % n == 0)
    for h0 in range(0, heads, together):
        hs = list(range(h0, h0 + together))
        sls = [slice(h * e, (h + 1) * e) for h in hs]
        qs = [q_ref[:, sl] for sl in sls]
        sc = [_dot_nt(q, kc_ref[:, sl]) for q, sl in zip(qs, sls)]
        sp = [_dot_nt(q, kp_ref[:, sl]) for q, sl in zip(qs, sls)]
        sc = [jnp.where(valid_c, s * scale - slope_ref[h] * dist_c, MASK_VALUE) for s, h in zip(sc, hs)]
        sp = [jnp.where(valid_p, s * scale - slope_ref[h] * dist_p, MASK_VALUE) for s, h in zip(sp, hs)]
        m = [jnp.maximum(jnp.max(a, axis=-1, keepdims=True), jnp.max(b, axis=-1, keepdims=True))
             for a, b in zip(sc, sp)]
        pc = [jnp.exp(a - mm) for a, mm in zip(sc, m)]
        pp = [jnp.exp(b - mm) for b, mm in zip(sp, m)]
        den = [jnp.sum(a, axis=-1, keepdims=True) + jnp.sum(b, axis=-1, keepdims=True) for a, b in zip(pc, pp)]
        o = [_dot(a.astype(BF16), vc_ref[:, sl]) + _dot(b.astype(BF16), vp_ref[:, sl])
             for a, b, sl in zip(pc, pp, sls)]
        for h, sl, oo, dd, mm in zip(hs, sls, o, den, m):
            o_ref[:, sl] = (oo / dd).astype(o_ref.dtype)
            lse_tile = jnp.where(lane == h, mm + jnp.log(dd), lse_tile)
    lse_ref[...] = lse_tile


def _attn_group(qkv, slopes, group, dilation, batch, seq, heads):
    e, blk = ATTN_HEAD_DIM, ATTN_BLOCK
    he = heads * e
    sub = seq // dilation
    nb = sub // blk
    cur = lambda kind: pl.BlockSpec((None, None, blk, he), lambda b, r, j: (b, r, j, kind))
    prev = lambda kind: pl.BlockSpec((None, None, blk, he), lambda b, r, j: (b, r, jnp.maximum(j - 1, 0), kind))
    out, lse = pl.pallas_call(
        functools.partial(_attn_kernel, heads=heads, scale=e ** -0.5),
        out_shape=(jax.ShapeDtypeStruct((batch, dilation, sub, he), BF16),
                   jax.ShapeDtypeStruct((batch, dilation, sub, LANES), F32)),
        grid=(batch, dilation, nb),
        in_specs=[pl.BlockSpec(memory_space=pltpu.SMEM), cur(0), prev(1), cur(1), prev(2), cur(2)],
        out_specs=(pl.BlockSpec((None, None, blk, he), lambda b, r, j: (b, r, j, 0)),
                   pl.BlockSpec((None, None, blk, LANES), lambda b, r, j: (b, r, j, 0))),
        compiler_params=_cparams("parallel", "parallel", "arbitrary"),
        name=f"dilated_attn_g{group}",
    )(slopes, qkv, qkv, qkv, qkv, qkv)
    natural = lambda a: a.transpose(0, 2, 1, 3).reshape(batch * seq, a.shape[-1])
    return natural(out), natural(lse)


def _attn_out_kernel(o0_ref, o1_ref, o2_ref, l0_ref, l1_ref, l2_ref, w_ref, r_ref, out_ref, m_ref, *, heads):
    e = ATTN_HEAD_DIM

    @pl.when(pl.program_id(1) == 0)
    def _():
        l0, l1, l2 = l0_ref[...], l1_ref[...], l2_ref[...]
        mx = jnp.maximum(jnp.maximum(l0, l1), l2)
        e0, e1, e2 = jnp.exp(l0 - mx), jnp.exp(l1 - mx), jnp.exp(l2 - mx)
        inv = 1.0 / (e0 + e1 + e2)
        src = lax.broadcasted_iota(jnp.int32, (LANES, heads * e), 0)
        dst = lax.broadcasted_iota(jnp.int32, (LANES, heads * e), 1)
        spread = jnp.where(src == dst // e, 1.0, 0.0).astype(BF16)

        def per_lane(w):
            hi = w.astype(BF16)
            lo = (w - hi.astype(F32)).astype(BF16)
            return _dot(hi, spread) + _dot(lo, spread)

        acc = per_lane(e0 * inv) * o0_ref[...].astype(F32)
        acc += per_lane(e1 * inv) * o1_ref[...].astype(F32)
        acc += per_lane(e2 * inv) * o2_ref[...].astype(F32)
        m_ref[...] = acc.astype(BF16)

    out_ref[...] = r_ref[...] + _dot(m_ref[...], w_ref[...])


def attention_layer(x, g, w_qkv, w_o, batch, seq):
    t, d = x.shape
    n_dil = len(ATTN_PATTERNS)
    he = w_o.shape[0]
    heads = he // ATTN_HEAD_DIM
    n_sl = n_dil * heads
    slopes = (2.0 ** (-8.0 * jnp.arange(1, n_sl + 1, dtype=F32) / n_sl)).reshape(n_dil, heads)
    outs, lses = [], []
    for grp, (window, dilation) in enumerate(ATTN_PATTERNS):
        assert window // dilation == ATTN_BLOCK and (seq // dilation) % ATTN_BLOCK == 0
        qkv = norm_matmul_strided(x, g, w_qkv, grp * 3 * he, 3 * he, dilation, batch, seq)
        o, l = _attn_group(qkv, slopes[grp] * dilation, grp, dilation, batch, seq, heads)
        outs.append(o)
        lses.append(l)
    tm, tn = _tile(t, 512), _tile(d, 1024)
    ospec = pl.BlockSpec((tm, he), lambda i, j: (i, 0))
    lspec = pl.BlockSpec((tm, LANES), lambda i, j: (i, 0))
    return pl.pallas_call(
        functools.partial(_attn_out_kernel, heads=heads),
        out_shape=jax.ShapeDtypeStruct((t, d), F32),
        grid=(t // tm, d // tn),
        in_specs=[ospec, ospec, ospec, lspec, lspec, lspec,
                  pl.BlockSpec((he, tn), lambda i, j: (0, j)),
                  pl.BlockSpec((tm, tn), lambda i, j: (i, j))],
        out_specs=pl.BlockSpec((tm, tn), lambda i, j: (i, j)),
        scratch_shapes=[pltpu.VMEM((tm, he), BF16)],
        compiler_params=_cparams("parallel", "arbitrary"),
        name="attn_merge_out_proj",
    )(*outs, *lses, w_o, x)


def _s5_chunk_operators(log_dt, a_re, a_im, b_re, b_im, c_re, c_im, d_skip, chunk):
    hp = dict(precision=HIGHEST)
    n_g, n_p = a_re.shape
    n_c = b_re.shape[-1]
    dt = jnp.exp(log_dt)[:, None]
    mag = jnp.exp(dt * a_re)
    ab_re = mag * jnp.cos(dt * a_im)
    ab_im = mag * jnp.sin(dt * a_im)
    den = a_re * a_re + a_im * a_im
    zr = ab_re - 1.0
    cr = (zr * a_re + ab_im * a_im) / den
    ci = (ab_im * a_re - zr * a_im) / den
    bb_re = cr[..., None] * b_re - ci[..., None] * b_im
    bb_im = cr[..., None] * b_im + ci[..., None] * b_re
    pr, pi = [jnp.ones_like(ab_re)], [jnp.zeros_like(ab_re)]
    for _ in range(chunk):
        pr, pi = pr + [pr[-1] * ab_re - pi[-1] * ab_im], pi + [pr[-1] * ab_im + pi[-1] * ab_re]
    pr, pi = jnp.stack(pr), jnp.stack(pi)
    ce_re = c_re[None] * pr[:, :, None, :] - c_im[None] * pi[:, :, None, :]
    ce_im = c_re[None] * pi[:, :, None, :] + c_im[None] * pr[:, :, None, :]
    kern = (jnp.einsum('tgcp,gpd->tgcd', ce_re[:chunk], bb_re, **hp)
            - jnp.einsum('tgcp,gpd->tgcd', ce_im[:chunk], bb_im, **hp))
    kern = kern.at[0].add(jax.vmap(jnp.diag)(d_skip))
    gpb = LANES // n_c
    nblk = n_g // gpb
    same = jnp.eye(gpb, dtype=F32)
    lag = jnp.arange(chunk)[None, :] - jnp.arange(chunk)[:, None]
    toep = jnp.where((lag >= 0)[:, :, None, None, None], kern[jnp.maximum(lag, 0)], 0.0)
    toep = toep.reshape(chunk, chunk, nblk, gpb, n_c, n_c).transpose(2, 0, 3, 5, 1, 4)
    m_op = toep[:, :, :, :, :, None, :] * same[None, None, :, None, None, :, None]
    m_op = m_op.reshape(nblk, chunk * LANES, chunk * LANES)
    qr, qi = pr[chunk - 1::-1][:chunk], pi[chunk - 1::-1][:chunk]
    bo_re = qr[..., None] * bb_re[None] - qi[..., None] * bb_im[None]
    bo_im = qr[..., None] * bb_im[None] + qi[..., None] * bb_re[None]
    both = jnp.stack([jnp.concatenate([bo_re, bo_im], axis=2), jnp.concatenate([bo_im, bo_re], axis=2)])
    both = both.reshape(2, chunk, nblk, gpb, 2 * n_p, n_c).transpose(2, 1, 3, 5, 0, 4)
    b_op = both[:, :, :, :, :, None, :] * same[None, None, :, None, None, :, None]
    b_op = b_op.reshape(nblk, chunk * LANES, 2 * gpb * 2 * n_p)
    cq = jnp.concatenate([ce_re[1:], -ce_im[1:]], axis=-1)
    cq = cq.reshape(chunk, nblk, gpb, n_c, 2 * n_p).transpose(1, 2, 4, 0, 3)
    c_op = cq[:, :, :, :, None, :] * same[None, :, None, None, :, None]
    c_op = c_op.reshape(nblk, gpb * 2 * n_p, chunk * LANES)
    al_re, al_im = pr[chunk], pi[chunk]
    per_blk = lambda parts: jnp.concatenate(parts, axis=-1).reshape(nblk, 1, gpb * 2 * n_p)
    coef_same = jnp.concatenate([per_blk([al_re, al_re])] * 2, axis=-1)
    coef_cross = jnp.concatenate([per_blk([-al_im, al_im]), per_blk([al_im, -al_im])], axis=-1)
    return m_op.astype(BF16), b_op.astype(BF16), c_op.astype(BF16), coef_same, coef_cross


def _chunk_rows(u_ref):
    return jnp.concatenate([u_ref[l] for l in range(u_ref.shape[0])], axis=-1)


def _s5_in_kernel(u_ref, b_ref, o_ref):
    o_ref[...] = _dot(_chunk_rows(u_ref), b_ref[...])


def _s5_scan_kernel(xin_ref, cs_ref, cc_ref, o_ref, st_ref, *, half):
    @pl.when(pl.program_id(2) == 0)
    def _():
        st_ref[...] = jnp.zeros_like(st_ref)

    cs, cc = cs_ref[...], cc_ref[...]

    def step(n, st):
        o_ref[pl.ds(n, 1), :] = st[:, :half]
        st_sw = jnp.concatenate([st[:, half:], st[:, :half]], axis=-1)
        return cs * st + cc * st_sw + xin_ref[pl.ds(n, 1), :]

    st_ref[...] = lax.fori_loop(0, xin_ref.shape[0], step, st_ref[...])


def _s5_out_kernel(u_ref, m_ref, xp_ref, c_ref, o_ref):
    y = _dot(_chunk_rows(u_ref), m_ref[...]) + _dot(xp_ref[...].astype(BF16), c_ref[...])
    y = jax.nn.gelu(y).astype(o_ref.dtype)
    for step in range(o_ref.shape[0]):
        o_ref[step] = y[:, step * LANES:(step + 1) * LANES]


def _glu_out_kernel(y_ref, wa_ref, wb_ref, r_ref, o_ref, yn_ref, *, dilation):
    @pl.when(pl.program_id(1) == 0)
    def _():
        tm, kdim = yn_ref.shape
        per = tm // dilation
        nat = lax.broadcasted_iota(jnp.int32, (tm, tm), 0)
        src = lax.broadcasted_iota(jnp.int32, (tm, tm), 1)
        perm = jnp.where(src == (nat % dilation) * per + nat // dilation, 1.0, 0.0).astype(BF16)
        yn_ref[...] = _dot(perm, y_ref[...].reshape(tm, kdim)).astype(BF16)

    y = yn_ref[...]
    o_ref[...] = r_ref[...] + _dot(y, wa_ref[...]) * jax.nn.sigmoid(_dot(y, wb_ref[...]))


def s5_layer(x, g, w_in, log_dt, a_re, a_im, b_re, b_im, c_re, c_im, d_skip, w_out, batch, seq):
    t, d = x.shape
    n_g, n_p = a_re.shape
    n_c = SSM_CH
    ck = SSM_CHUNK
    gc = n_g * n_c
    nblk = gc // LANES
    n_chunks = seq // ck
    sw = 4 * n_p * (LANES // n_c)
    m_op, b_op, c_op, coef_same, coef_cross = _s5_chunk_operators(
        log_dt, a_re, a_im, b_re, b_im, c_re, c_im, d_skip, ck)
    u = norm_matmul_strided(x, g, w_in, 0, gc, ck, batch, seq)
    tr = _tile(n_chunks, 512)
    u_spec = pl.BlockSpec((None, ck, tr, LANES), lambda k, b, n: (b, 0, n, k))
    xin = pl.pallas_call(
        _s5_in_kernel,
        out_shape=jax.ShapeDtypeStruct((batch, n_chunks, nblk * sw), F32),
        grid=(nblk, batch, n_chunks // tr),
        in_specs=[u_spec, pl.BlockSpec((None, ck * LANES, sw), lambda k, b, n: (k, 0, 0))],
        out_specs=pl.BlockSpec((None, tr, sw), lambda k, b, n: (b, n, k)),
        compiler_params=_cparams("parallel", "parallel", "parallel"),
        name="s5_chunk_inputs",
    )(u, b_op)
    coef_spec = pl.BlockSpec((None, 1, sw), lambda b, k, n: (k, 0, 0))
    xprev = pl.pallas_call(
        functools.partial(_s5_scan_kernel, half=sw // 2),
        out_shape=jax.ShapeDtypeStruct((batch, n_chunks, nblk * sw // 2), F32),
        grid=(batch, nblk, n_chunks // tr),
        in_specs=[pl.BlockSpec((None, tr, sw), lambda b, k, n: (b, n, k)), coef_spec, coef_spec],
        out_specs=pl.BlockSpec((None, tr, sw // 2), lambda b, k, n: (b, n, k)),
        scratch_shapes=[pltpu.VMEM((1, sw), F32)],
        compiler_params=_cparams("parallel", "parallel", "arbitrary"),
        name="s5_chunk_scan",
    )(xin, coef_same, coef_cross)
    y = pl.pallas_call(
        _s5_out_kernel,
        out_shape=jax.ShapeDtypeStruct((batch, ck, n_chunks, gc), BF16),
        grid=(nblk, batch, n_chunks // tr),
        in_specs=[u_spec,
                  pl.BlockSpec((None, ck * LANES, ck * LANES), lambda k, b, n: (k, 0, 0)),
                  pl.BlockSpec((None, tr, sw // 2), lambda k, b, n: (b, n, k)),
                  pl.BlockSpec((None, sw // 2, ck * LANES), lambda k, b, n: (k, 0, 0))],
        out_specs=u_spec,
        compiler_params=_cparams("parallel", "parallel", "parallel"),
        name="s5_chunk_outputs",
    )(u, m_op, xprev, c_op)
    tm, tn2 = _tile(seq, 512), _tile(d, 512)
    nt, nj = seq // tm, d // tn2
    return pl.pallas_call(
        functools.partial(_glu_out_kernel, dilation=ck),
        out_shape=jax.ShapeDtypeStruct((t, d), F32),
        grid=(t // tm, nj),
        in_specs=[pl.BlockSpec((None, ck, tm // ck, gc), lambda i, j: (i // nt, 0, i % nt, 0)),
                  pl.BlockSpec((gc, tn2), lambda i, j: (0, j)),
                  pl.BlockSpec((gc, tn2), lambda i, j: (0, j + nj)),
                  pl.BlockSpec((tm, tn2), lambda i, j: (i, j))],
        out_specs=pl.BlockSpec((tm, tn2), lambda i, j: (i, j)),
        scratch_shapes=[pltpu.VMEM((tm, gc), BF16)],
        compiler_params=_cparams("parallel", "arbitrary"),
        name="s5_glu_out_proj",
    )(y, w_out, w_out, x)


def _shift_norm(x_ref, xp_ref, g_ref, first):
    g = g_ref[...]
    h = _rms(x_ref[...], g)
    prev = _rms(xp_ref[7:8, :], g)
    prev = jnp.where(first, 0.0, prev)
    row = lax.broadcasted_iota(jnp.int32, h.shape, 0)
    hp = jnp.where(row == 0, prev, pltpu.roll(h, 1, 0))
    return h, hp


def _rwkv_proj_kernel(x_ref, xp_ref, g_ref, mu_ref, w_ref, o_ref, h_ref, d_ref, l_ref, *, tm, seq):
    i, j, n = pl.program_id(0), pl.program_id(1), pl.program_id(2)

    @pl.when((j == 0) & (n == 0))
    def _():
        h, hp = _shift_norm(x_ref, xp_ref, g_ref, (i * tm) % seq == 0)
        h_ref[...] = h
        d_ref[...] = hp - h

    @pl.when(n == 0)
    def _():
        l_ref[...] = (h_ref[...] + d_ref[...] * mu_ref[...]).astype(BF16)

    o_ref[...] = _dot(l_ref[...], w_ref[...])


def _softplus(z):
    return jnp.maximum(z, 0.0) + jnp.log(1.0 + jnp.exp(-jnp.abs(z)))


def _rwkv_lora_kernel(x_ref, xp_ref, g_ref, mu_ref, w0_ref, w1_ref, w2_ref, a0_ref, a1_ref, a2_ref,
                      g1_ref, g2_ref, lw_ref, a_ref, gate_ref, *, tm, seq):
    h, hp = _shift_norm(x_ref, xp_ref, g_ref, (pl.program_id(0) * tm) % seq == 0)
    dlt = hp - h
    xw = (h + dlt * mu_ref[0:1, :]).astype(BF16)
    xa = (h + dlt * mu_ref[1:2, :]).astype(BF16)
    xg = (h + dlt * mu_ref[2:3, :]).astype(BF16)
    wl = w0_ref[...] + _dot(jnp.tanh(_dot(xw, w1_ref[...])).astype(BF16), w2_ref[...])
    w = -_softplus(-wl) - 0.5
    lw_ref[...] = -jnp.exp(w)
    a_ref[...] = jax.nn.sigmoid(a0_ref[...] + _dot(_dot(xa, a1_ref[...]).astype(BF16), a2_ref[...]))
    gate_ref[...] = _dot(jax.nn.sigmoid(_dot(xg, g1_ref[...])).astype(BF16), g2_ref[...])


def _rwkv_core_kernel(r_ref, k_ref, v_ref, lw_ref, a_ref, gate_ref, kk_ref, ka_ref, rk_ref, lnw_ref, lnb_ref,
                      o_ref, s_ref, lhs_ref, rhs_ref, bk_ref, v2_ref, dec_ref, y_ref):
    ck, hd = RWKV_CHUNK, RWKV_HEAD_DIM
    nb, tc, width = r_ref.shape
    nch, ck2 = tc // ck, 2 * ck
    seqs = [(bi, slice(pi * LANES, (pi + 1) * LANES)) for bi in range(nb) for pi in range(width // LANES)]

    @pl.when(pl.program_id(1) == 0)
    def _():
        s_ref[...] = jnp.zeros_like(s_ref)

    lane = lax.broadcasted_iota(jnp.int32, (1, 1, LANES), 2)
    head_a = lane < hd
    hrow = lax.broadcasted_iota(jnp.int32, (LANES, LANES), 0) // hd
    hcol = lax.broadcasted_iota(jnp.int32, (LANES, LANES), 1) // hd
    head_ones = jnp.where(hrow == hcol, 1.0, 0.0).astype(BF16)
    trow = lax.broadcasted_iota(jnp.int32, (tc, tc), 0)
    tcol = lax.broadcasted_iota(jnp.int32, (tc, tc), 1)
    chunk_tril = jnp.where((trow // ck == tcol // ck) & (tcol <= trow), 1.0, 0.0).astype(BF16)

    def exact_dot(m01, x):
        hi, mid, lo = _split3(x)
        return _dot(m01, hi) + _dot(m01, mid) + _dot(m01, lo)

    def head_sum(x):
        hi, mid, lo = _split3(x)
        return _dot(hi, head_ones) + _dot(mid, head_ones) + _dot(lo, head_ones)

    def stack_heads(x):
        return jnp.concatenate([jnp.where(head_a, x, 0.0), jnp.where(head_a, 0.0, x)], axis=1).astype(BF16)

    for si, (bi, ls) in enumerate(seqs):
        k_all, a_all = k_ref[bi, :, ls], a_ref[bi, :, ls]
        kk = k_all * kk_ref[:, ls]
        kk = kk / jnp.maximum(jnp.sqrt(head_sum(kk * kk)), 1e-12)
        k2 = k_all * (1.0 + (a_all - 1.0) * ka_ref[:, ls])
        lw = lw_ref[bi, :, ls]
        by_chunk = lambda x: x.reshape(nch, ck, LANES)
        cs = by_chunk(exact_dot(chunk_tril, lw))
        tot = cs[:, ck - 1:ck, :]
        gam_inv, gam_rem = jnp.exp(-cs), jnp.exp(tot - cs)
        atm = by_chunk(-kk) * jnp.exp(cs - by_chunk(lw))
        rm = by_chunk(r_ref[bi, :, ls]) * jnp.exp(cs)
        b3, k3 = by_chunk(kk * a_all), by_chunk(k2)
        lhs_ref[si] = jnp.concatenate([stack_heads(atm), stack_heads(rm)], axis=1)
        rhs_ref[si] = jnp.concatenate([stack_heads(b3 * gam_inv), stack_heads(k3 * gam_inv)], axis=1)
        bk_ref[si] = jnp.concatenate([stack_heads(b3 * gam_rem), stack_heads(k3 * gam_rem)], axis=1)
        v2_ref[si] = stack_heads(by_chunk(v_ref[bi, :, ls]))
        dec_ref[si] = jnp.exp(tot)

    row = lax.broadcasted_iota(jnp.int32, (ck2, ck2), 0)
    col = lax.broadcasted_iota(jnp.int32, (ck2, ck2), 1)
    incl = col <= row
    strict = col < row
    eye = jnp.where(row == col, 1.0, 0.0)
    n_seq = len(seqs)
    each = lambda f, *lists: [f(*args) for args in zip(*lists)]

    def chunk_step(ci, carry):
        lhs = [lhs_ref[si, ci] for si in range(n_seq)]
        gram = each(lambda l, si: _dot_nt(l, rhs_ref[si, ci]), lhs, range(n_seq))
        a_ab = each(lambda g: jnp.where(strict, g[:ck2, :ck2], 0.0), gram)
        a_lo = each(lambda g: jnp.concatenate([jnp.where(strict, g[:ck2, ck2:], 0.0),
                                               jnp.where(incl, g[ck2:, ck2:], 0.0)], axis=0).astype(BF16), gram)
        a_rb = each(lambda g: jnp.where(incl, g[ck2:, :ck2], 0.0).astype(BF16), gram)
        inv = each(lambda a: eye + a, a_ab)
        apow = a_ab
        span = 2
        while span < ck:
            apow = each(lambda a: _dot(a.astype(BF16), a.astype(BF16)), apow)
            inv = each(lambda t, a: t + _dot(t.astype(BF16), a.astype(BF16)), inv, apow)
            span *= 2
        v2 = [v2_ref[si, ci] for si in range(n_seq)]
        av = each(_dot, a_lo, v2)
        s = [s_ref[si] for si in range(n_seq)]
        xs = each(lambda l, st: _dot_nt(l, st.astype(BF16)), lhs, s)
        u = each(lambda t, x, w: _dot(t.astype(BF16), (x[:ck2] + w[:ck2]).astype(BF16)).astype(BF16), inv, xs, av)
        y2 = each(lambda x, w, arb, ub: x[ck2:] + w[ck2:] + _dot(arb, ub), xs, av, a_rb, u)
        sl = pl.ds(pl.multiple_of(ci * ck, ck), ck)
        for si, (bi, ls) in enumerate(seqs):
            y_ref[bi, sl, ls] = y2[si][:ck] + y2[si][ck:]
            uv = jnp.concatenate([u[si], v2[si]], axis=0)
            s_ref[si] = s[si] * dec_ref[si, ci] + _dot_tn(uv, bk_ref[si, ci])
        return carry

    lax.fori_loop(0, nch, chunk_step, 0)

    for bi, ls in seqs:
        y = y_ref[bi, :, ls]
        mean = head_sum(y) * (1.0 / hd)
        yc = y - mean
        var = head_sum(yc * yc) * (1.0 / hd)
        yn = yc * lax.rsqrt(var + RWKV_GN_EPS) * lnw_ref[:, ls] + lnb_ref[:, ls]
        k2 = k_ref[bi, :, ls] * (1.0 + (a_ref[bi, :, ls] - 1.0) * ka_ref[:, ls])
        bonus = head_sum(r_ref[bi, :, ls] * k2 * rk_ref[:, ls]) * v_ref[bi, :, ls]
        o_ref[bi, :, ls] = ((yn + bonus) * gate_ref[bi, :, ls]).astype(o_ref.dtype)


def _pad_to(a, axis, size):
    pad = [(0, 0)] * a.ndim
    pad[axis] = (0, size - a.shape[axis])
    return jnp.pad(a, pad)


def rwkv_layer(x, g, mu, w_rkv, w0, w1, w2, a0, a1, a2, g1, g2, k_k, k_a, r_k, ln_w, ln_b, w_o, batch, seq):
    t, d = x.shape
    row = lambda p: p.reshape(1, d).astype(F32)
    g2d = g.reshape(1, d)
    tn = _tile(d, 1024)
    prev_spec = lambda tm: pl.BlockSpec((8, d), (lambda i, *_: (jnp.maximum(i * (tm // 8) - 1, 0), 0)))
    tm = _tile(seq, 512)
    rkv = pl.pallas_call(
        functools.partial(_rwkv_proj_kernel, tm=tm, seq=seq),
        out_shape=jax.ShapeDtypeStruct((3, t, d), F32),
        grid=(t // tm, 3, d // tn),
        in_specs=[pl.BlockSpec((tm, d), lambda i, j, n: (i, 0)),
                  prev_spec(tm),
                  pl.BlockSpec((1, d), lambda i, j, n: (0, 0)),
                  pl.BlockSpec((None, 1, d), lambda i, j, n: (j, 0, 0)),
                  pl.BlockSpec((None, d, tn), lambda i, j, n: (j, 0, n))],
        out_specs=pl.BlockSpec((None, tm, tn), lambda i, j, n: (j, i, n)),
        scratch_shapes=[pltpu.VMEM((tm, d), F32), pltpu.VMEM((tm, d), F32), pltpu.VMEM((tm, d), BF16)],
        compiler_params=_cparams("parallel", "arbitrary", "arbitrary"),
        name="rwkv_rkv_proj",
    )(x, x, g2d, mu[:3].reshape(3, 1, d), w_rkv)

    pad_rank = lambda w_a, w_b: (_pad_to(w_a, 1, -(-w_a.shape[1] // LANES) * LANES).astype(BF16),
                                 _pad_to(w_b, 0, -(-w_b.shape[0] // LANES) * LANES).astype(BF16))
    w1p, w2p = pad_rank(w1, w2)
    a1p, a2p = pad_rank(a1, a2)
    g1p, g2p = pad_rank(g1, g2)
    full = lambda a: pl.BlockSpec(a.shape, lambda i: (0,) * a.ndim)
    tm = _tile(seq, 256)
    tok = pl.BlockSpec((tm, d), lambda i: (i, 0))
    lora_in = [x, x, g2d, mu[3:6], row(w0), w1p, w2p, row(a0), a1p, a2p, g1p, g2p]
    lw, a_gate, gate = pl.pallas_call(
        functools.partial(_rwkv_lora_kernel, tm=tm, seq=seq),
        out_shape=(jax.ShapeDtypeStruct((t, d), F32),) * 3,
        grid=(t // tm,),
        in_specs=[tok, prev_spec(tm)] + [full(a) for a in lora_in[2:]],
        out_specs=(tok, tok, tok),
        compiler_params=_cparams("parallel"),
        name="rwkv_lora",
    )(*lora_in)

    tc = _tile(seq, 512)
    wd = _tile(d, 2 * LANES)
    tokc = pl.BlockSpec((batch, tc, wd), lambda p, c: (0, c, p))
    rkvc = lambda which: pl.BlockSpec((None, batch, tc, wd), lambda p, c: (which, 0, c, p))
    par = pl.BlockSpec((1, wd), lambda p, c: (0, p))
    n_seq = batch * (wd // LANES)
    nch = tc // RWKV_CHUNK
    bsd = lambda a: a.reshape(batch, seq, d)
    rkv4 = rkv.reshape(3, batch, seq, d)
    mixed = pl.pallas_call(
        _rwkv_core_kernel,
        out_shape=jax.ShapeDtypeStruct((batch, seq, d), BF16),
        grid=(d // wd, seq // tc),
        in_specs=[rkvc(0), rkvc(1), rkvc(2), tokc, tokc, tokc, par, par, par, par, par],
        out_specs=tokc,
        scratch_shapes=[pltpu.VMEM((n_seq, LANES, LANES), F32)]
        + [pltpu.VMEM((n_seq, nch, 4 * RWKV_CHUNK, LANES), BF16)] * 3
        + [pltpu.VMEM((n_seq, nch, 2 * RWKV_CHUNK, LANES), BF16),
           pltpu.VMEM((n_seq, nch, 1, LANES), F32),
           pltpu.VMEM((batch, tc, wd), F32)],
        compiler_params=_cparams("parallel", "arbitrary"),
        name="rwkv_chunked_state",
    )(rkv4, rkv4, rkv4, bsd(lw), bsd(a_gate), bsd(gate), row(k_k), row(k_a), row(r_k), row(ln_w), row(ln_b))
    return matmul_residual(mixed.reshape(t, d), w_o, x)


def kernel(x, norm_mix, norm_mlp, norm_f, attn_w_qkv, attn_w_o, ssm_w_in, ssm_log_dt, ssm_a_re, ssm_a_im,
           ssm_b_re, ssm_b_im, ssm_c_re, ssm_c_im, ssm_d, ssm_w_out, rwkv_mu, rwkv_w_rkv, rwkv_w0, rwkv_w1,
           rwkv_w2, rwkv_a0, rwkv_a1, rwkv_a2, rwkv_g1, rwkv_g2, rwkv_k_k, rwkv_k_a, rwkv_r_k, rwkv_ln_w,
           rwkv_ln_b, rwkv_w_o, mlp_w1, mlp_w2):
    batch, seq, d = x.shape
    depth = norm_mix.shape[0]
    bf = lambda w: w.astype(BF16)
    h = x.reshape(batch * seq, d)
    ia = ib = ic = 0
    for layer in range(depth):
        kind = layer % 3
        if kind == 0:
            h = attention_layer(h, norm_mix[layer], bf(attn_w_qkv[ia]), bf(attn_w_o[ia]), batch, seq)
            ia += 1
        elif kind == 1:
            h = s5_layer(h, norm_mix[layer], bf(ssm_w_in[ib]), ssm_log_dt[ib], ssm_a_re[ib], ssm_a_im[ib],
                         ssm_b_re[ib], ssm_b_im[ib], ssm_c_re[ib], ssm_c_im[ib], ssm_d[ib],
                         bf(ssm_w_out[ib]), batch, seq)
            ib += 1
        else:
            h = rwkv_layer(h, norm_mix[layer], rwkv_mu[ic], bf(rwkv_w_rkv[ic]), rwkv_w0[ic], rwkv_w1[ic],
                           rwkv_w2[ic], rwkv_a0[ic], rwkv_a1[ic], rwkv_a2[ic], rwkv_g1[ic], rwkv_g2[ic],
                           rwkv_k_k[ic], rwkv_k_a[ic], rwkv_r_k[ic], rwkv_ln_w[ic], rwkv_ln_b[ic],
                           bf(rwkv_w_o[ic]), batch, seq)
            ic += 1
        g_final = norm_f if layer == depth - 1 else None
        h = mlp_residual(h, norm_mlp[layer], bf(mlp_w1[layer]), bf(mlp_w2[layer]), g_final)
    return h.reshape(batch, seq, d)
```

```python
import functools

import jax
import jax.numpy as jnp
from jax import lax
from jax.experimental import pallas as pl
from jax.experimental.pallas import tpu as pltpu

F32 = jnp.float32
BF16 = jnp.bfloat16
HIGHEST = lax.Precision.HIGHEST

NORM_EPS = 1e-5
LANES = 128
VMEM_LIMIT_BYTES = 56 * 2**20
MASK_VALUE = -1e30

ATTN_PATTERNS = ((128, 1), (512, 4), (2048, 16))
ATTN_BLOCK = 128
ATTN_HEAD_DIM = 128
SSM_CH = 16
SSM_CHUNK = 16
SSM_DT_MIN = 0.001
SSM_DT_MAX = 0.1
RWKV_HEAD_DIM = 64
RWKV_CHUNK = 64
RWKV_GN_EPS = RWKV_HEAD_DIM * 1e-5


def _cparams(*sem):
    return pltpu.CompilerParams(dimension_semantics=sem, vmem_limit_bytes=VMEM_LIMIT_BYTES)


def _tile(n, pref):
    t = min(n, pref)
    while n % t:
        t //= 2
    return t


def _rms(x, g):
    ms = jnp.mean(x * x, axis=-1, keepdims=True)
    return x * lax.rsqrt(ms + NORM_EPS) * g


def _dot(a, b):
    return jnp.dot(a, b, preferred_element_type=F32)


def _dot_nt(a, b):
    return lax.dot_general(a, b, (((1,), (1,)), ((), ())), preferred_element_type=F32)


def _dot_tn(a, b):
    return lax.dot_general(a, b, (((0,), (0,)), ((), ())), preferred_element_type=F32)


def _split2(x):
    hi = x.astype(BF16)
    return hi, (x - hi.astype(F32)).astype(BF16)


def _norm_matmul_strided_kernel(x_ref, g_ref, w_ref, o_ref, h_ref, *, dilation):
    @pl.when(pl.program_id(1) == 0)
    def _():
        h = _rms(x_ref[...], g_ref[...]).astype(BF16)
        if dilation > 1:
            tm = h.shape[0]
            per = tm // dilation
            new = lax.broadcasted_iota(jnp.int32, (tm, tm), 0)
            old = lax.broadcasted_iota(jnp.int32, (tm, tm), 1)
            perm = jnp.where(old == (new % per) * dilation + new // per, 1.0, 0.0).astype(BF16)
            h = _dot(perm, h).astype(BF16)
        h_ref[...] = h

    o_ref[...] = _dot(h_ref[...], w_ref[...]).astype(o_ref.dtype).reshape(o_ref.shape)


def norm_matmul_strided(x, g, w, col0, ncols, dilation, batch, seq, tm=512, tn=1024):
    t, d = x.shape
    tm, tn = _tile(seq, tm), _tile(ncols, tn)
    assert col0 % tn == 0 and (tm // dilation) % 16 == 0
    nt = seq // tm
    return pl.pallas_call(
        functools.partial(_norm_matmul_strided_kernel, dilation=dilation),
        out_shape=jax.ShapeDtypeStruct((batch, dilation, seq // dilation, ncols), BF16),
        grid=(t // tm, ncols // tn),
        in_specs=[pl.BlockSpec((tm, d), lambda i, j: (i, 0)),
                  pl.BlockSpec((1, d), lambda i, j: (0, 0)),
                  pl.BlockSpec((d, tn), lambda i, j: (0, col0 // tn + j))],
        out_specs=pl.BlockSpec((None, dilation, tm // dilation, tn), lambda i, j: (i // nt, 0, i % nt, j)),
        scratch_shapes=[pltpu.VMEM((tm, d), BF16)],
        compiler_params=_cparams("parallel", "arbitrary"),
        name=f"norm_matmul_stride{dilation}",
    )(x, g.reshape(1, d), w)


def _matmul_res_kernel(a_ref, w_ref, r_ref, o_ref):
    o_ref[...] = r_ref[...] + _dot(a_ref[...], w_ref[...])


def matmul_residual(a, w, res, tm=512, tn=1024):
    t, k = a.shape
    n = w.shape[1]
    tm, tn = _tile(t, tm), _tile(n, tn)
    return pl.pallas_call(
        _matmul_res_kernel,
        out_shape=jax.ShapeDtypeStruct((t, n), F32),
        grid=(t // tm, n // tn),
        in_specs=[pl.BlockSpec((tm, k), lambda i, j: (i, 0)),
                  pl.BlockSpec((k, tn), lambda i, j: (0, j)),
                  pl.BlockSpec((tm, tn), lambda i, j: (i, j))],
        out_specs=pl.BlockSpec((tm, tn), lambda i, j: (i, j)),
        compiler_params=_cparams("parallel", "parallel"),
        name="matmul_residual",
    )(a, w, res)


def _mlp_kernel(x_ref, g_ref, w1_ref, w2_ref, gf_ref, o_ref, h_ref, *, final_norm):
    f = pl.program_id(1)

    @pl.when(f == 0)
    def _():
        x = x_ref[...]
        h_ref[...] = _rms(x, g_ref[...]).astype(BF16)
        o_ref[...] = x

    a = _dot(h_ref[...], w1_ref[...])
    a = jnp.square(jnp.maximum(a, 0.0)).astype(BF16)
    o_ref[...] += _dot(a, w2_ref[...])

    if final_norm:
        @pl.when(f == pl.num_programs(1) - 1)
        def _():
            o_ref[...] = _rms(o_ref[...], gf_ref[...])


def mlp_residual(x, g, w1, w2, g_final=None, tm=512, tf=1024):
    t, d = x.shape
    ff = w1.shape[1]
    tm, tf = _tile(t, tm), _tile(ff, tf)
    final_norm = g_final is not None
    gf = (g_final if final_norm else g).reshape(1, d)
    return pl.pallas_call(
        functools.partial(_mlp_kernel, final_norm=final_norm),
        out_shape=jax.ShapeDtypeStruct((t, d), F32),
        grid=(t // tm, ff // tf),
        in_specs=[pl.BlockSpec((tm, d), lambda i, f: (i, 0)),
                  pl.BlockSpec((1, d), lambda i, f: (0, 0)),
                  pl.BlockSpec((d, tf), lambda i, f: (0, f)),
                  pl.BlockSpec((tf, d), lambda i, f: (f, 0)),
                  pl.BlockSpec((1, d), lambda i, f: (0, 0))],
        out_specs=pl.BlockSpec((tm, d), lambda i, f: (i, 0)),
        scratch_shapes=[pltpu.VMEM((tm, d), BF16)],
        compiler_params=_cparams("parallel", "arbitrary"),
        name="mlp_residual",
    )(x, g.reshape(1, d), w1, w2, gf)


def _attn_kernel(slope_ref, q_ref, kp_ref, kc_ref, vp_ref, vc_ref, o_ref, lse_ref, *, heads, scale):
    blk, e = ATTN_BLOCK, ATTN_HEAD_DIM
    j = pl.program_id(2)
    qi = lax.broadcasted_iota(jnp.int32, (blk, blk), 0)
    kj = lax.broadcasted_iota(jnp.int32, (blk, blk), 1)
    dist_c = (qi - kj).astype(F32)
    dist_p = dist_c + float(blk)
    valid_c = kj <= qi
    valid_p = (kj >= qi) & (j > 0)
    lane = lax.broadcasted_iota(jnp.int32, (blk, LANES), 1)
    lse_tile = jnp.zeros((blk, LANES), F32)
    together = next(n for n in (4, 2, 1) if heads % n == 0)
    for h0 in range(0, heads, together):
        hs = list(range(h0, h0 + together))
        sls = [slice(h * e, (h + 1) * e) for h in hs]
        qs = [q_ref[:, sl] for sl in sls]
        sc = [_dot_nt(q, kc_ref[:, sl]) for q, sl in zip(qs, sls)]
        sp = [_dot_nt(q, kp_ref[:, sl]) for q, sl in zip(qs, sls)]
        sc = [jnp.where(valid_c, s * scale - slope_ref[h] * dist_c, MASK_VALUE) for s, h in zip(sc, hs)]
        sp = [jnp.where(valid_p, s * scale - slope_ref[h] * dist_p, MASK_VALUE) for s, h in zip(sp, hs)]
        m = [jnp.maximum(jnp.max(a, axis=-1, keepdims=True), jnp.max(b, axis=-1, keepdims=True))
             for a, b in zip(sc, sp)]
        pc = [jnp.exp(a - mm) for a, mm in zip(sc, m)]
        pp = [jnp.exp(b - mm) for b, mm in zip(sp, m)]
        den = [jnp.sum(a, axis=-1, keepdims=True) + jnp.sum(b, axis=-1, keepdims=True) for a, b in zip(pc, pp)]
        o = [_dot(a.astype(BF16), vc_ref[:, sl]) + _dot(b.astype(BF16), vp_ref[:, sl])
             for a, b, sl in zip(pc, pp, sls)]
        for h, sl, oo, dd, mm in zip(hs, sls, o, den, m):
            o_ref[:, sl] = (oo / dd).astype(o_ref.dtype)
            lse_tile = jnp.where(lane == h, mm + jnp.log(dd), lse_tile)
    lse_ref[...] = lse_tile


def _attn_group(qkv, slopes, group, dilation, batch, seq, heads):
    e, blk = ATTN_HEAD_DIM, ATTN_BLOCK
    he = heads * e
    sub = seq // dilation
    nb = sub // blk
    cur = lambda kind: pl.BlockSpec((None, None, blk, he), lambda b, r, j: (b, r, j, kind))
    prev = lambda kind: pl.BlockSpec((None, None, blk, he), lambda b, r, j: (b, r, jnp.maximum(j - 1, 0), kind))
    out, lse = pl.pallas_call(
        functools.partial(_attn_kernel, heads=heads, scale=e ** -0.5),
        out_shape=(jax.ShapeDtypeStruct((batch, dilation, sub, he), BF16),
                   jax.ShapeDtypeStruct((batch, dilation, sub, LANES), F32)),
        grid=(batch, dilation, nb),
        in_specs=[pl.BlockSpec(memory_space=pltpu.SMEM), cur(0), prev(1), cur(1), prev(2), cur(2)],
        out_specs=(pl.BlockSpec((None, None, blk, he), lambda b, r, j: (b, r, j, 0)),
                   pl.BlockSpec((None, None, blk, LANES), lambda b, r, j: (b, r, j, 0))),
        compiler_params=_cparams("parallel", "parallel", "arbitrary"),
        name=f"dilated_attn_g{group}",
    )(slopes, qkv, qkv, qkv, qkv, qkv)
    natural = lambda a: a.transpose(0, 2, 1, 3).reshape(batch * seq, a.shape[-1])
    return natural(out), natural(lse)


def _attn_out_kernel(o0_ref, o1_ref, o2_ref, l0_ref, l1_ref, l2_ref, w_ref, r_ref, out_ref, m_ref, *, heads):
    e = ATTN_HEAD_DIM

    @pl.when(pl.program_id(1) == 0)
    def _():
        l0, l1, l2 = l0_ref[...], l1_ref[...], l2_ref[...]
        mx = jnp.maximum(jnp.maximum(l0, l1), l2)
        e0, e1, e2 = jnp.exp(l0 - mx), jnp.exp(l1 - mx), jnp.exp(l2 - mx)
        inv = 1.0 / (e0 + e1 + e2)
        src = lax.broadcasted_iota(jnp.int32, (LANES, heads * e), 0)
        dst = lax.broadcasted_iota(jnp.int32, (LANES, heads * e), 1)
        spread = jnp.where(src == dst // e, 1.0, 0.0).astype(BF16)

        def per_lane(w):
            hi = w.astype(BF16)
            lo = (w - hi.astype(F32)).astype(BF16)
            return _dot(hi, spread) + _dot(lo, spread)

        acc = per_lane(e0 * inv) * o0_ref[...].astype(F32)
        acc += per_lane(e1 * inv) * o1_ref[...].astype(F32)
        acc += per_lane(e2 * inv) * o2_ref[...].astype(F32)
        m_ref[...] = acc.astype(BF16)

    out_ref[...] = r_ref[...] + _dot(m_ref[...], w_ref[...])


def attention_layer(x, g, w_qkv, w_o, batch, seq):
    t, d = x.shape
    n_dil = len(ATTN_PATTERNS)
    he = w_o.shape[0]
    heads = he // ATTN_HEAD_DIM
    n_sl = n_dil * heads
    slopes = (2.0 ** (-8.0 * jnp.arange(1, n_sl + 1, dtype=F32) / n_sl)).reshape(n_dil, heads)
    outs, lses = [], []
    for grp, (window, dilation) in enumerate(ATTN_PATTERNS):
        assert window // dilation == ATTN_BLOCK and (seq // dilation) % ATTN_BLOCK == 0
        qkv = norm_matmul_strided(x, g, w_qkv, grp * 3 * he, 3 * he, dilation, batch, seq)
        o, l = _attn_group(qkv, slopes[grp] * dilation, grp, dilation, batch, seq, heads)
        outs.append(o)
        lses.append(l)
    tm, tn = _tile(t, 512), _tile(d, 1024)
    ospec = pl.BlockSpec((tm, he), lambda i, j: (i, 0))
    lspec = pl.BlockSpec((tm, LANES), lambda i, j: (i, 0))
    return pl.pallas_call(
        functools.partial(_attn_out_kernel, heads=heads),
        out_shape=jax.ShapeDtypeStruct((t, d), F32),
        grid=(t // tm, d // tn),
        in_specs=[ospec, ospec, ospec, lspec, lspec, lspec,
                  pl.BlockSpec((he, tn), lambda i, j: (0, j)),
                  pl.BlockSpec((tm, tn), lambda i, j: (i, j))],
        out_specs=pl.BlockSpec((tm, tn), lambda i, j: (i, j)),
        scratch_shapes=[pltpu.VMEM((tm, he), BF16)],
        compiler_params=_cparams("parallel", "arbitrary"),
        name="attn_merge_out_proj",
    )(*outs, *lses, w_o, x)


def _s5_chunk_operators(log_dt, a_re, a_im, b_re, b_im, c_re, c_im, d_skip, chunk):
    hp = dict(precision=HIGHEST)
    n_g, n_p = a_re.shape
    n_c = b_re.shape[-1]
    dt = jnp.exp(log_dt)[:, None]
    mag = jnp.exp(dt * a_re)
    ab_re = mag * jnp.cos(dt * a_im)
    ab_im = mag * jnp.sin(dt * a_im)
    den = a_re * a_re + a_im * a_im
    zr = ab_re - 1.0
    cr = (zr * a_re + ab_im * a_im) / den
    ci = (ab_im * a_re - zr * a_im) / den
    bb_re = cr[..., None] * b_re - ci[..., None] * b_im
    bb_im = cr[..., None] * b_im + ci[..., None] * b_re
    pr, pi = [jnp.ones_like(ab_re)], [jnp.zeros_like(ab_re)]
    for _ in range(chunk):
        pr, pi = pr + [pr[-1] * ab_re - pi[-1] * ab_im], pi + [pr[-1] * ab_im + pi[-1] * ab_re]
    pr, pi = jnp.stack(pr), jnp.stack(pi)
    ce_re = c_re[None] * pr[:, :, None, :] - c_im[None] * pi[:, :, None, :]
    ce_im = c_re[None] * pi[:, :, None, :] + c_im[None] * pr[:, :, None, :]
    kern = (jnp.einsum('tgcp,gpd->tgcd', ce_re[:chunk], bb_re, **hp)
            - jnp.einsum('tgcp,gpd->tgcd', ce_im[:chunk], bb_im, **hp))
    kern = kern.at[0].add(jax.vmap(jnp.diag)(d_skip))
    gpb = LANES // n_c
    nblk = n_g // gpb
    lag = jnp.arange(chunk)[None, :] - jnp.arange(chunk)[:, None]
    m_op = jnp.where((lag >= 0)[:, :, None, None, None], kern[jnp.maximum(lag, 0)], 0.0)
    m_op = m_op.transpose(2, 0, 4, 1, 3).reshape(n_g, chunk * n_c, chunk * n_c)
    qr, qi = pr[chunk - 1::-1][:chunk], pi[chunk - 1::-1][:chunk]
    bo_re = qr[..., None] * bb_re[None] - qi[..., None] * bb_im[None]
    bo_im = qr[..., None] * bb_im[None] + qi[..., None] * bb_re[None]
    flat_b = lambda a: a.transpose(1, 0, 3, 2).reshape(n_g, chunk * n_c, n_p)
    bo_re, bo_im = flat_b(bo_re), flat_b(bo_im)
    b_op = jnp.concatenate([bo_re, bo_im, bo_im, bo_re], axis=-1)
    flat_c = lambda a: a.transpose(1, 3, 0, 2).reshape(n_g, n_p, chunk * n_c)
    c_op = jnp.concatenate([flat_c(ce_re[1:]), -flat_c(ce_im[1:])], axis=1)
    al_re, al_im = pr[chunk], pi[chunk]
    per_blk = lambda parts: jnp.concatenate(parts, axis=-1).reshape(nblk, 1, gpb * 2 * n_p)
    coef_same = jnp.concatenate([per_blk([al_re, al_re])] * 2, axis=-1)
    coef_cross = jnp.concatenate([per_blk([-al_im, al_im]), per_blk([al_im, -al_im])], axis=-1)
    return m_op.astype(BF16), b_op.astype(BF16), c_op.astype(BF16), coef_same, coef_cross


def _chunk_rows(u_ref):
    return jnp.concatenate([u_ref[l] for l in range(u_ref.shape[0])], axis=-1)


def _first_visit_of_block():
    return (pl.program_id(1) == 0) & (pl.program_id(2) == 0)


def _s5_in_kernel(u_ref, b_ref, o_ref, dense_ref, *, n_c):
    @pl.when(_first_visit_of_block())
    def _():
        gpb, rows, cols = b_ref.shape
        half = cols // 2
        dense_ref[...] = jnp.zeros_like(dense_ref)
        for gi in range(gpb):
            for l in range(rows // n_c):
                r0 = l * LANES + gi * n_c
                piece = b_ref[gi, l * n_c:(l + 1) * n_c, :]
                dense_ref[r0:r0 + n_c, gi * half:(gi + 1) * half] = piece[:, :half]
                dense_ref[r0:r0 + n_c, (gpb + gi) * half:(gpb + gi + 1) * half] = piece[:, half:]

    o_ref[...] = _dot(_chunk_rows(u_ref), dense_ref[...])


def _s5_scan_kernel(xin_ref, cs_ref, cc_ref, o_ref, st_ref, *, half):
    @pl.when(pl.program_id(2) == 0)
    def _():
        st_ref[...] = jnp.zeros_like(st_ref)

    cs, cc = cs_ref[...], cc_ref[...]

    def step(n, st):
        o_ref[pl.ds(n, 1), :] = st[:, :half]
        st_sw = jnp.concatenate([st[:, half:], st[:, :half]], axis=-1)
        return cs * st + cc * st_sw + xin_ref[pl.ds(n, 1), :]

    st_ref[...] = lax.fori_loop(0, xin_ref.shape[0], step, st_ref[...])


def _s5_out_kernel(u_ref, m_ref, xp_ref, c_ref, o_ref, mdense_ref, cdense_ref, *, n_c):
    @pl.when(_first_visit_of_block())
    def _():
        gpb, rows, cols = m_ref.shape
        n_q = c_ref.shape[1]
        src = lax.broadcasted_iota(jnp.int32, (cols, mdense_ref.shape[1]), 0)
        dst = lax.broadcasted_iota(jnp.int32, (cols, mdense_ref.shape[1]), 1)
        for gi in range(gpb):
            spread = jnp.where(dst == (src // n_c) * LANES + gi * n_c + src % n_c, 1.0, 0.0).astype(BF16)
            wide = _dot(m_ref[gi], spread).astype(BF16)
            for l in range(rows // n_c):
                r0 = l * LANES + gi * n_c
                mdense_ref[r0:r0 + n_c, :] = wide[l * n_c:(l + 1) * n_c, :]
            cdense_ref[gi * n_q:(gi + 1) * n_q, :] = _dot(c_ref[gi], spread).astype(BF16)

    y = _dot(_chunk_rows(u_ref), mdense_ref[...]) + _dot(xp_ref[...].astype(BF16), cdense_ref[...])
    y = jax.nn.gelu(y).astype(o_ref.dtype)
    for step in range(o_ref.shape[0]):
        o_ref[step] = y[:, step * LANES:(step + 1) * LANES]


def _glu_out_kernel(y_ref, wa_ref, wb_ref, r_ref, o_ref, yn_ref, *, dilation):
    @pl.when(pl.program_id(1) == 0)
    def _():
        tm, kdim = yn_ref.shape
        per = tm // dilation
        nat = lax.broadcasted_iota(jnp.int32, (tm, tm), 0)
        src = lax.broadcasted_iota(jnp.int32, (tm, tm), 1)
        perm = jnp.where(src == (nat % dilation) * per + nat // dilation, 1.0, 0.0).astype(BF16)
        yn_ref[...] = _dot(perm, y_ref[...].reshape(tm, kdim)).astype(BF16)

    y = yn_ref[...]
    o_ref[...] = r_ref[...] + _dot(y, wa_ref[...]) * jax.nn.sigmoid(_dot(y, wb_ref[...]))


def s5_layer(x, g, w_in, log_dt, a_re, a_im, b_re, b_im, c_re, c_im, d_skip, w_out, batch, seq):
    t, d = x.shape
    n_g, n_p = a_re.shape
    n_c = SSM_CH
    ck = SSM_CHUNK
    gc = n_g * n_c
    nblk = gc // LANES
    n_chunks = seq // ck
    sw = 4 * n_p * (LANES // n_c)
    m_op, b_op, c_op, coef_same, coef_cross = _s5_chunk_operators(
        log_dt, a_re, a_im, b_re, b_im, c_re, c_im, d_skip, ck)
    u = norm_matmul_strided(x, g, w_in, 0, gc, ck, batch, seq)
    tr = _tile(n_chunks, 512)
    u_spec = pl.BlockSpec((None, ck, tr, LANES), lambda k, b, n: (b, 0, n, k))
    gpb = LANES // n_c
    group_ops = lambda a: pl.BlockSpec((gpb,) + a.shape[1:], lambda k, b, n: (k, 0, 0))
    xin = pl.pallas_call(
        functools.partial(_s5_in_kernel, n_c=n_c),
        out_shape=jax.ShapeDtypeStruct((batch, n_chunks, nblk * sw), F32),
        grid=(nblk, batch, n_chunks // tr),
        in_specs=[u_spec, group_ops(b_op)],
        out_specs=pl.BlockSpec((None, tr, sw), lambda k, b, n: (b, n, k)),
        scratch_shapes=[pltpu.VMEM((ck * LANES, sw), BF16)],
        compiler_params=_cparams("arbitrary", "arbitrary", "arbitrary"),
        name="s5_chunk_inputs",
    )(u, b_op)
    coef_spec = pl.BlockSpec((None, 1, sw), lambda b, k, n: (k, 0, 0))
    xprev = pl.pallas_call(
        functools.partial(_s5_scan_kernel, half=sw // 2),
        out_shape=jax.ShapeDtypeStruct((batch, n_chunks, nblk * sw // 2), F32),
        grid=(batch, nblk, n_chunks // tr),
        in_specs=[pl.BlockSpec((None, tr, sw), lambda b, k, n: (b, n, k)), coef_spec, coef_spec],
        out_specs=pl.BlockSpec((None, tr, sw // 2), lambda b, k, n: (b, n, k)),
        scratch_shapes=[pltpu.VMEM((1, sw), F32)],
        compiler_params=_cparams("parallel", "parallel", "arbitrary"),
        name="s5_chunk_scan",
    )(xin, coef_same, coef_cross)
    y = pl.pallas_call(
        functools.partial(_s5_out_kernel, n_c=n_c),
        out_shape=jax.ShapeDtypeStruct((batch, ck, n_chunks, gc), BF16),
        grid=(nblk, batch, n_chunks // tr),
        in_specs=[u_spec, group_ops(m_op),
                  pl.BlockSpec((None, tr, sw // 2), lambda k, b, n: (b, n, k)), group_ops(c_op)],
        out_specs=u_spec,
        scratch_shapes=[pltpu.VMEM((ck * LANES, ck * LANES), BF16), pltpu.VMEM((sw // 2, ck * LANES), BF16)],
        compiler_params=_cparams("arbitrary", "arbitrary", "arbitrary"),
        name="s5_chunk_outputs",
    )(u, m_op, xprev, c_op)
    tm, tn2 = _tile(seq, 512), _tile(d, 512)
    nt, nj = seq // tm, d // tn2
    return pl.pallas_call(
        functools.partial(_glu_out_kernel, dilation=ck),
        out_shape=jax.ShapeDtypeStruct((t, d), F32),
        grid=(t // tm, nj),
        in_specs=[pl.BlockSpec((None, ck, tm // ck, gc), lambda i, j: (i // nt, 0, i % nt, 0)),
                  pl.BlockSpec((gc, tn2), lambda i, j: (0, j)),
                  pl.BlockSpec((gc, tn2), lambda i, j: (0, j + nj)),
                  pl.BlockSpec((tm, tn2), lambda i, j: (i, j))],
        out_specs=pl.BlockSpec((tm, tn2), lambda i, j: (i, j)),
        scratch_shapes=[pltpu.VMEM((tm, gc), BF16)],
        compiler_params=_cparams("parallel", "arbitrary"),
        name="s5_glu_out_proj",
    )(y, w_out, w_out, x)


def _shift_norm(x_ref, xp_ref, g_ref, first):
    g = g_ref[...]
    h = _rms(x_ref[...], g)
    prev = _rms(xp_ref[7:8, :], g)
    prev = jnp.where(first, 0.0, prev)
    row = lax.broadcasted_iota(jnp.int32, h.shape, 0)
    hp = jnp.where(row == 0, prev, pltpu.roll(h, 1, 0))
    return h, hp


def _rwkv_proj_kernel(x_ref, xp_ref, g_ref, mu_ref, w_ref, o_ref, h_ref, d_ref, l_ref, *, tm, seq):
    i, j, n = pl.program_id(0), pl.program_id(1), pl.program_id(2)

    @pl.when((j == 0) & (n == 0))
    def _():
        h, hp = _shift_norm(x_ref, xp_ref, g_ref, (i * tm) % seq == 0)
        h_ref[...] = h
        d_ref[...] = hp - h

    @pl.when(n == 0)
    def _():
        l_ref[...] = (h_ref[...] + d_ref[...] * mu_ref[...]).astype(BF16)

    o_ref[...] = _dot(l_ref[...], w_ref[...])


def _softplus(z):
    return jnp.maximum(z, 0.0) + jnp.log(1.0 + jnp.exp(-jnp.abs(z)))


def _rwkv_lora_kernel(x_ref, xp_ref, g_ref, mu_ref, w0_ref, w1_ref, w2_ref, a0_ref, a1_ref, a2_ref,
                      g1_ref, g2_ref, lw_ref, a_ref, gate_ref, *, tm, seq):
    h, hp = _shift_norm(x_ref, xp_ref, g_ref, (pl.program_id(0) * tm) % seq == 0)
    dlt = hp - h
    xw = (h + dlt * mu_ref[0:1, :]).astype(BF16)
    xa = (h + dlt * mu_ref[1:2, :]).astype(BF16)
    xg = (h + dlt * mu_ref[2:3, :]).astype(BF16)
    wl = w0_ref[...] + _dot(jnp.tanh(_dot(xw, w1_ref[...])).astype(BF16), w2_ref[...])
    w = -_softplus(-wl) - 0.5
    lw_ref[...] = -jnp.exp(w)
    a_ref[...] = jax.nn.sigmoid(a0_ref[...] + _dot(_dot(xa, a1_ref[...]).astype(BF16), a2_ref[...]))
    gate_ref[...] = _dot(jax.nn.sigmoid(_dot(xg, g1_ref[...])).astype(BF16), g2_ref[...])


def _rwkv_core_kernel(r_ref, k_ref, v_ref, lw_ref, a_ref, gate_ref, kk_ref, ka_ref, rk_ref, lnw_ref, lnb_ref,
                      o_ref, s_ref, lhs_ref, rhs_ref, bk_ref, v2_ref, dec_ref, y_ref):
    ck, hd = RWKV_CHUNK, RWKV_HEAD_DIM
    nb, tc, width = r_ref.shape
    nch, ck2 = tc // ck, 2 * ck
    seqs = [(bi, slice(pi * LANES, (pi + 1) * LANES)) for bi in range(nb) for pi in range(width // LANES)]

    @pl.when(pl.program_id(1) == 0)
    def _():
        s_ref[...] = jnp.zeros_like(s_ref)

    lane = lax.broadcasted_iota(jnp.int32, (1, 1, LANES), 2)
    head_a = lane < hd
    hrow = lax.broadcasted_iota(jnp.int32, (LANES, LANES), 0) // hd
    hcol = lax.broadcasted_iota(jnp.int32, (LANES, LANES), 1) // hd
    head_ones = jnp.where(hrow == hcol, 1.0, 0.0).astype(BF16)
    trow = lax.broadcasted_iota(jnp.int32, (tc, tc), 0)
    tcol = lax.broadcasted_iota(jnp.int32, (tc, tc), 1)
    chunk_tril = jnp.where((trow // ck == tcol // ck) & (tcol <= trow), 1.0, 0.0).astype(BF16)

    def chunk_cumsum(x):
        hi, lo = _split2(x)
        return _dot(chunk_tril, hi) + _dot(chunk_tril, lo)

    def head_sum(x):
        hi, lo = _split2(x)
        return _dot(hi, head_ones) + _dot(lo, head_ones)

    def stack_heads(x):
        return jnp.concatenate([jnp.where(head_a, x, 0.0), jnp.where(head_a, 0.0, x)], axis=1).astype(BF16)

    for si, (bi, ls) in enumerate(seqs):
        k_all, a_all = k_ref[bi, :, ls], a_ref[bi, :, ls]
        kk = k_all * kk_ref[:, ls]
        kk = kk / jnp.maximum(jnp.sqrt(head_sum(kk * kk)), 1e-12)
        k2 = k_all * (1.0 + (a_all - 1.0) * ka_ref[:, ls])
        lw = lw_ref[bi, :, ls]
        by_chunk = lambda x: x.reshape(nch, ck, LANES)
        cs = by_chunk(chunk_cumsum(lw))
        tot = cs[:, ck - 1:ck, :]
        gam_inv, gam_rem = jnp.exp(-cs), jnp.exp(tot - cs)
        atm = by_chunk(-kk) * jnp.exp(cs - by_chunk(lw))
        rm = by_chunk(r_ref[bi, :, ls]) * jnp.exp(cs)
        b3, k3 = by_chunk(kk * a_all), by_chunk(k2)
        lhs_ref[si] = jnp.concatenate([stack_heads(atm), stack_heads(rm)], axis=1)
        rhs_ref[si] = jnp.concatenate([stack_heads(b3 * gam_inv), stack_heads(k3 * gam_inv)], axis=1)
        bk_ref[si] = jnp.concatenate([stack_heads(b3 * gam_rem), stack_heads(k3 * gam_rem)], axis=1)
        v2_ref[si] = stack_heads(by_chunk(v_ref[bi, :, ls]))
        dec_ref[si] = jnp.exp(tot)

    row = lax.broadcasted_iota(jnp.int32, (ck2, ck2), 0)
    col = lax.broadcasted_iota(jnp.int32, (ck2, ck2), 1)
    incl = col <= row
    strict = col < row
    eye = jnp.where(row == col, 1.0, 0.0)
    n_seq = len(seqs)
    each = lambda f, *lists: [f(*args) for args in zip(*lists)]

    def chunk_step(ci, carry):
        lhs = [lhs_ref[si, ci] for si in range(n_seq)]
        gram = each(lambda l, si: _dot_nt(l, rhs_ref[si, ci]), lhs, range(n_seq))
        a_ab = each(lambda g: jnp.where(strict, g[:ck2, :ck2], 0.0), gram)
        a_lo = each(lambda g: jnp.concatenate([jnp.where(strict, g[:ck2, ck2:], 0.0),
                                               jnp.where(incl, g[ck2:, ck2:], 0.0)], axis=0).astype(BF16), gram)
        a_rb = each(lambda g: jnp.where(incl, g[ck2:, :ck2], 0.0).astype(BF16), gram)
        inv = each(lambda a: eye + a, a_ab)
        pw = each(lambda a: _dot(a.astype(BF16), a.astype(BF16)), a_ab)
        m = 2
        while 2 * m < ck:
            pwb = each(lambda p: p.astype(BF16), pw)
            both = each(lambda p, t: _dot(jnp.concatenate([p, t.astype(BF16)], axis=0), p), pwb, inv)
            pw = each(lambda z: z[:ck2], both)
            inv = each(lambda t, z: t + z[ck2:], inv, both)
            m *= 2
        inv = each(lambda t, p: t + _dot(t.astype(BF16), p.astype(BF16)), inv, pw)
        v2 = [v2_ref[si, ci] for si in range(n_seq)]
        av = each(_dot, a_lo, v2)
        s = [s_ref[si] for si in range(n_seq)]
        xs = each(lambda l, st: _dot_nt(l, st.astype(BF16)), lhs, s)
        u = each(lambda t, x, w: _dot(t.astype(BF16), (x[:ck2] + w[:ck2]).astype(BF16)).astype(BF16), inv, xs, av)
        y2 = each(lambda x, w, arb, ub: x[ck2:] + w[ck2:] + _dot(arb, ub), xs, av, a_rb, u)
        sl = pl.ds(pl.multiple_of(ci * ck, ck), ck)
        for si, (bi, ls) in enumerate(seqs):
            y_ref[bi, sl, ls] = y2[si][:ck] + y2[si][ck:]
            uv = jnp.concatenate([u[si], v2[si]], axis=0)
            s_ref[si] = s[si] * dec_ref[si, ci] + _dot_tn(uv, bk_ref[si, ci])
        return carry

    lax.fori_loop(0, nch, chunk_step, 0)

    for bi, ls in seqs:
        y = y_ref[bi, :, ls]
        mean = head_sum(y) * (1.0 / hd)
        yc = y - mean
        var = head_sum(yc * yc) * (1.0 / hd)
        yn = yc * lax.rsqrt(var + RWKV_GN_EPS) * lnw_ref[:, ls] + lnb_ref[:, ls]
        k2 = k_ref[bi, :, ls] * (1.0 + (a_ref[bi, :, ls] - 1.0) * ka_ref[:, ls])
        bonus = head_sum(r_ref[bi, :, ls] * k2 * rk_ref[:, ls]) * v_ref[bi, :, ls]
        o_ref[bi, :, ls] = ((yn + bonus) * gate_ref[bi, :, ls]).astype(o_ref.dtype)


def _pad_to(a, axis, size):
    pad = [(0, 0)] * a.ndim
    pad[axis] = (0, size - a.shape[axis])
    return jnp.pad(a, pad)


def rwkv_layer(x, g, mu, w_rkv, w0, w1, w2, a0, a1, a2, g1, g2, k_k, k_a, r_k, ln_w, ln_b, w_o, batch, seq):
    t, d = x.shape
    row = lambda p: p.reshape(1, d).astype(F32)
    g2d = g.reshape(1, d)
    tn = _tile(d, 1024)
    prev_spec = lambda tm: pl.BlockSpec((8, d), (lambda i, *_: (jnp.maximum(i * (tm // 8) - 1, 0), 0)))
    tm = _tile(seq, 512)
    rkv = pl.pallas_call(
        functools.partial(_rwkv_proj_kernel, tm=tm, seq=seq),
        out_shape=jax.ShapeDtypeStruct((3, t, d), F32),
        grid=(t // tm, 3, d // tn),
        in_specs=[pl.BlockSpec((tm, d), lambda i, j, n: (i, 0)),
                  prev_spec(tm),
                  pl.BlockSpec((1, d), lambda i, j, n: (0, 0)),
                  pl.BlockSpec((None, 1, d), lambda i, j, n: (j, 0, 0)),
                  pl.BlockSpec((None, d, tn), lambda i, j, n: (j, 0, n))],
        out_specs=pl.BlockSpec((None, tm, tn), lambda i, j, n: (j, i, n)),
        scratch_shapes=[pltpu.VMEM((tm, d), F32), pltpu.VMEM((tm, d), F32), pltpu.VMEM((tm, d), BF16)],
        compiler_params=_cparams("parallel", "arbitrary", "arbitrary"),
        name="rwkv_rkv_proj",
    )(x, x, g2d, mu[:3].reshape(3, 1, d), w_rkv)

    pad_rank = lambda w_a, w_b: (_pad_to(w_a, 1, -(-w_a.shape[1] // LANES) * LANES).astype(BF16),
                                 _pad_to(w_b, 0, -(-w_b.shape[0] // LANES) * LANES).astype(BF16))
    w1p, w2p = pad_rank(w1, w2)
    a1p, a2p = pad_rank(a1, a2)
    g1p, g2p = pad_rank(g1, g2)
    full = lambda a: pl.BlockSpec(a.shape, lambda i: (0,) * a.ndim)
    tm = _tile(seq, 256)
    tok = pl.BlockSpec((tm, d), lambda i: (i, 0))
    lora_in = [x, x, g2d, mu[3:6], row(w0), w1p, w2p, row(a0), a1p, a2p, g1p, g2p]
    lw, a_gate, gate = pl.pallas_call(
        functools.partial(_rwkv_lora_kernel, tm=tm, seq=seq),
        out_shape=(jax.ShapeDtypeStruct((t, d), F32),) * 3,
        grid=(t // tm,),
        in_specs=[tok, prev_spec(tm)] + [full(a) for a in lora_in[2:]],
        out_specs=(tok, tok, tok),
        compiler_params=_cparams("parallel"),
        name="rwkv_lora",
    )(*lora_in)

    tc = _tile(seq, 512)
    wd = _tile(d, 4 * LANES)
    tokc = pl.BlockSpec((batch, tc, wd), lambda p, c: (0, c, p))
    rkvc = lambda which: pl.BlockSpec((None, batch, tc, wd), lambda p, c: (which, 0, c, p))
    par = pl.BlockSpec((1, wd), lambda p, c: (0, p))
    n_seq = batch * (wd // LANES)
    nch = tc // RWKV_CHUNK
    bsd = lambda a: a.reshape(batch, seq, d)
    rkv4 = rkv.reshape(3, batch, seq, d)
    mixed = pl.pallas_call(
        _rwkv_core_kernel,
        out_shape=jax.ShapeDtypeStruct((batch, seq, d), BF16),
        grid=(d // wd, seq // tc),
        in_specs=[rkvc(0), rkvc(1), rkvc(2), tokc, tokc, tokc, par, par, par, par, par],
        out_specs=tokc,
        scratch_shapes=[pltpu.VMEM((n_seq, LANES, LANES), F32)]
        + [pltpu.VMEM((n_seq, nch, 4 * RWKV_CHUNK, LANES), BF16)] * 3
        + [pltpu.VMEM((n_seq, nch, 2 * RWKV_CHUNK, LANES), BF16),
           pltpu.VMEM((n_seq, nch, 1, LANES), F32),
           pltpu.VMEM((batch, tc, wd), F32)],
        compiler_params=_cparams("parallel", "arbitrary"),
        name="rwkv_chunked_state",
    )(rkv4, rkv4, rkv4, bsd(lw), bsd(a_gate), bsd(gate), row(k_k), row(k_a), row(r_k), row(ln_w), row(ln_b))
    return matmul_residual(mixed.reshape(t, d), w_o, x)


def kernel(x, norm_mix, norm_mlp, norm_f, attn_w_qkv, attn_w_o, ssm_w_in, ssm_log_dt, ssm_a_re, ssm_a_im,
           ssm_b_re, ssm_b_im, ssm_c_re, ssm_c_im, ssm_d, ssm_w_out, rwkv_mu, rwkv_w_rkv, rwkv_w0, rwkv_w1,
           rwkv_w2, rwkv_a0, rwkv_a1, rwkv_a2, rwkv_g1, rwkv_g2, rwkv_k_k, rwkv_k_a, rwkv_r_k, rwkv_ln_w,
           rwkv_ln_b, rwkv_w_o, mlp_w1, mlp_w2):
    batch, seq, d = x.shape
    depth = norm_mix.shape[0]
    bf = lambda w: w.astype(BF16)
    h = x.reshape(batch * seq, d)
    ia = ib = ic = 0
    for layer in range(depth):
        kind = layer % 3
        if kind == 0:
            h = attention_layer(h, norm_mix[layer], bf(attn_w_qkv[ia]), bf(attn_w_o[ia]), batch, seq)
            ia += 1
        elif kind == 1:
            h = s5_layer(h, norm_mix[layer], bf(ssm_w_in[ib]), ssm_log_dt[ib], ssm_a_re[ib], ssm_a_im[ib],
                         ssm_b_re[ib], ssm_b_im[ib], ssm_c_re[ib], ssm_c_im[ib], ssm_d[ib],
                         bf(ssm_w_out[ib]), batch, seq)
            ib += 1
        else:
            h = rwkv_layer(h, norm_mix[layer], rwkv_mu[ic], bf(rwkv_w_rkv[ic]), rwkv_w0[ic], rwkv_w1[ic],
                           rwkv_w2[ic], rwkv_a0[ic], rwkv_a1[ic], rwkv_a2[ic], rwkv_g1[ic], rwkv_g2[ic],
                           rwkv_k_k[ic], rwkv_k_a[ic], rwkv_r_k[ic], rwkv_ln_w[ic], rwkv_ln_b[ic],
                           bf(rwkv_w_o[ic]), batch, seq)
            ic += 1
        g_final = norm_f if layer == depth - 1 else None
        h = mlp_residual(h, norm_mlp[layer], bf(mlp_w1[layer]), bf(mlp_w2[layer]), g_final)
    return h.reshape(batch, seq, d)
```

```python
import functools

import jax
import jax.numpy as jnp
from jax import lax
from jax.experimental import pallas as pl
from jax.experimental.pallas import tpu as pltpu

F32 = jnp.float32
BF16 = jnp.bfloat16
HIGHEST = lax.Precision.HIGHEST

NORM_EPS = 1e-5
LANES = 128
VMEM_LIMIT_BYTES = 56 * 2**20
MASK_VALUE = -1e30

ATTN_PATTERNS = ((128, 1), (512, 4), (2048, 16))
ATTN_BLOCK = 128
ATTN_HEAD_DIM = 128
SSM_CH = 16
SSM_CHUNK = 16
SSM_DT_MIN = 0.001
SSM_DT_MAX = 0.1
RWKV_HEAD_DIM = 64
RWKV_CHUNK = 64
RWKV_GN_EPS = RWKV_HEAD_DIM * 1e-5


def _cparams(*sem):
    return pltpu.CompilerParams(dimension_semantics=sem, vmem_limit_bytes=VMEM_LIMIT_BYTES)


def _tile(n, pref):
    t = min(n, pref)
    while n % t:
        t //= 2
    return t


def _rms(x, g):
    ms = jnp.mean(x * x, axis=-1, keepdims=True)
    return x * lax.rsqrt(ms + NORM_EPS) * g


def _dot(a, b):
    return jnp.dot(a, b, preferred_element_type=F32)


def _dot_nt(a, b):
    return lax.dot_general(a, b, (((1,), (1,)), ((), ())), preferred_element_type=F32)


def _dot_tn(a, b):
    return lax.dot_general(a, b, (((0,), (0,)), ((), ())), preferred_element_type=F32)


def _split2(x):
    hi = x.astype(BF16)
    return hi, (x - hi.astype(F32)).astype(BF16)


def _norm_matmul_strided_kernel(x_ref, g_ref, w_ref, o_ref, h_ref, *, dilation, sub):
    tm = h_ref.shape[0]
    per = sub // dilation

    @pl.when(pl.program_id(1) == 0)
    def _():
        if dilation > 1:
            new = lax.broadcasted_iota(jnp.int32, (sub, sub), 0)
            old = lax.broadcasted_iota(jnp.int32, (sub, sub), 1)
            perm = jnp.where(old == (new % per) * dilation + new // per, 1.0, 0.0).astype(BF16)
        for s in range(tm // sub):
            rows = slice(s * sub, (s + 1) * sub)
            h = _rms(x_ref[rows, :], g_ref[...]).astype(BF16)
            h_ref[rows, :] = _dot(perm, h).astype(BF16) if dilation > 1 else h

    y = _dot(h_ref[...], w_ref[...]).astype(o_ref.dtype)
    for s in range(tm // sub):
        o_ref[:, s * per:(s + 1) * per, :] = y[s * sub:(s + 1) * sub].reshape(dilation, per, y.shape[-1])


def norm_matmul_strided(x, g, w, col0, ncols, dilation, batch, seq, tm=1024, tn=1024, sub=512):
    t, d = x.shape
    tm, tn = _tile(seq, tm), _tile(ncols, tn)
    sub = min(sub, tm)
    assert col0 % tn == 0 and (sub // dilation) % 16 == 0
    nt = seq // tm
    return pl.pallas_call(
        functools.partial(_norm_matmul_strided_kernel, dilation=dilation, sub=sub),
        out_shape=jax.ShapeDtypeStruct((batch, dilation, seq // dilation, ncols), BF16),
        grid=(t // tm, ncols // tn),
        in_specs=[pl.BlockSpec((tm, d), lambda i, j: (i, 0)),
                  pl.BlockSpec((1, d), lambda i, j: (0, 0)),
                  pl.BlockSpec((d, tn), lambda i, j: (0, col0 // tn + j))],
        out_specs=pl.BlockSpec((None, dilation, tm // dilation, tn), lambda i, j: (i // nt, 0, i % nt, j)),
        scratch_shapes=[pltpu.VMEM((tm, d), BF16)],
        compiler_params=_cparams("parallel", "arbitrary"),
        name=f"norm_matmul_stride{dilation}",
    )(x, g.reshape(1, d), w)


def _matmul_res_kernel(a_ref, w_ref, r_ref, o_ref):
    o_ref[...] = r_ref[...] + _dot(a_ref[...], w_ref[...])


def matmul_residual(a, w, res, tm=512, tn=1024):
    t, k = a.shape
    n = w.shape[1]
    tm, tn = _tile(t, tm), _tile(n, tn)
    return pl.pallas_call(
        _matmul_res_kernel,
        out_shape=jax.ShapeDtypeStruct((t, n), F32),
        grid=(t // tm, n // tn),
        in_specs=[pl.BlockSpec((tm, k), lambda i, j: (i, 0)),
                  pl.BlockSpec((k, tn), lambda i, j: (0, j)),
                  pl.BlockSpec((tm, tn), lambda i, j: (i, j))],
        out_specs=pl.BlockSpec((tm, tn), lambda i, j: (i, j)),
        compiler_params=_cparams("parallel", "parallel"),
        name="matmul_residual",
    )(a, w, res)


def _mlp_kernel(x_ref, g_ref, w1_ref, w2_ref, gf_ref, o_ref, h_ref, *, final_norm):
    f = pl.program_id(1)

    @pl.when(f == 0)
    def _():
        x = x_ref[...]
        h_ref[...] = _rms(x, g_ref[...]).astype(BF16)
        o_ref[...] = x

    a = _dot(h_ref[...], w1_ref[...])
    a = jnp.square(jnp.maximum(a, 0.0)).astype(BF16)
    o_ref[...] += _dot(a, w2_ref[...])

    if final_norm:
        @pl.when(f == pl.num_programs(1) - 1)
        def _():
            o_ref[...] = _rms(o_ref[...], gf_ref[...])


def mlp_residual(x, g, w1, w2, g_final=None, tm=512, tf=1024):
    t, d = x.shape
    ff = w1.shape[1]
    tm, tf = _tile(t, tm), _tile(ff, tf)
    final_norm = g_final is not None
    gf = (g_final if final_norm else g).reshape(1, d)
    return pl.pallas_call(
        functools.partial(_mlp_kernel, final_norm=final_norm),
        out_shape=jax.ShapeDtypeStruct((t, d), F32),
        grid=(t // tm, ff // tf),
        in_specs=[pl.BlockSpec((tm, d), lambda i, f: (i, 0)),
                  pl.BlockSpec((1, d), lambda i, f: (0, 0)),
                  pl.BlockSpec((d, tf), lambda i, f: (0, f)),
                  pl.BlockSpec((tf, d), lambda i, f: (f, 0)),
                  pl.BlockSpec((1, d), lambda i, f: (0, 0))],
        out_specs=pl.BlockSpec((tm, d), lambda i, f: (i, 0)),
        scratch_shapes=[pltpu.VMEM((tm, d), BF16)],
        compiler_params=_cparams("parallel", "arbitrary"),
        name="mlp_residual",
    )(x, g.reshape(1, d), w1, w2, gf)


def _attn_kernel(slope_ref, q_ref, kp_ref, kc_ref, vp_ref, vc_ref, o_ref, lse_ref, *, heads, scale):
    blk, e = ATTN_BLOCK, ATTN_HEAD_DIM
    j = pl.program_id(2)
    qi = lax.broadcasted_iota(jnp.int32, (blk, blk), 0)
    kj = lax.broadcasted_iota(jnp.int32, (blk, blk), 1)
    dist_c = (qi - kj).astype(F32)
    dist_p = dist_c + float(blk)
    valid_c = kj <= qi
    valid_p = (kj >= qi) & (j > 0)
    lane = lax.broadcasted_iota(jnp.int32, (blk, LANES), 1)
    lse_tile = jnp.zeros((blk, LANES), F32)
    ones = jnp.ones((blk, e), BF16)
    together = next(n for n in (4, 2, 1) if heads % n == 0)
    for h0 in range(0, heads, together):
        hs = list(range(h0, h0 + together))
        sls = [slice(h * e, (h + 1) * e) for h in hs]
        qs = [q_ref[:, sl] for sl in sls]
        sc = [_dot_nt(q, kc_ref[:, sl]) for q, sl in zip(qs, sls)]
        sp = [_dot_nt(q, kp_ref[:, sl]) for q, sl in zip(qs, sls)]
        sc = [jnp.where(valid_c, s * scale - slope_ref[h] * dist_c, MASK_VALUE) for s, h in zip(sc, hs)]
        sp = [jnp.where(valid_p, s * scale - slope_ref[h] * dist_p, MASK_VALUE) for s, h in zip(sp, hs)]
        m = [jnp.max(jnp.maximum(a, b), axis=-1, keepdims=True) for a, b in zip(sc, sp)]
        pc = [jnp.exp(a - mm).astype(BF16) for a, mm in zip(sc, m)]
        pp = [jnp.exp(b - mm).astype(BF16) for b, mm in zip(sp, m)]
        den = [_dot(a, ones) + _dot(b, ones) for a, b in zip(pc, pp)]
        o = [_dot(a, vc_ref[:, sl]) + _dot(b, vp_ref[:, sl]) for a, b, sl in zip(pc, pp, sls)]
        for h, sl, oo, dd, mm in zip(hs, sls, o, den, m):
            o_ref[:, sl] = (oo / dd).astype(o_ref.dtype)
            lse_tile = jnp.where(lane == h, mm + jnp.log(dd), lse_tile)
    lse_ref[...] = lse_tile


def _attn_group(qkv, slopes, group, dilation, batch, seq, heads):
    e, blk = ATTN_HEAD_DIM, ATTN_BLOCK
    he = heads * e
    sub = seq // dilation
    nb = sub // blk
    cur = lambda kind: pl.BlockSpec((None, None, blk, he), lambda b, r, j: (b, r, j, kind))
    prev = lambda kind: pl.BlockSpec((None, None, blk, he), lambda b, r, j: (b, r, jnp.maximum(j - 1, 0), kind))
    out, lse = pl.pallas_call(
        functools.partial(_attn_kernel, heads=heads, scale=e ** -0.5),
        out_shape=(jax.ShapeDtypeStruct((batch, dilation, sub, he), BF16),
                   jax.ShapeDtypeStruct((batch, dilation, sub, LANES), F32)),
        grid=(batch, dilation, nb),
        in_specs=[pl.BlockSpec(memory_space=pltpu.SMEM), cur(0), prev(1), cur(1), prev(2), cur(2)],
        out_specs=(pl.BlockSpec((None, None, blk, he), lambda b, r, j: (b, r, j, 0)),
                   pl.BlockSpec((None, None, blk, LANES), lambda b, r, j: (b, r, j, 0))),
        compiler_params=_cparams("parallel", "parallel", "arbitrary"),
        name=f"dilated_attn_g{group}",
    )(slopes, qkv, qkv, qkv, qkv, qkv)
    natural = lambda a: a.transpose(0, 2, 1, 3).reshape(batch * seq, a.shape[-1])
    return natural(out), natural(lse)


def _attn_out_kernel(o0_ref, o1_ref, o2_ref, l0_ref, l1_ref, l2_ref, w_ref, r_ref, out_ref, m_ref, *, heads):
    e = ATTN_HEAD_DIM

    @pl.when(pl.program_id(1) == 0)
    def _():
        l0, l1, l2 = l0_ref[...], l1_ref[...], l2_ref[...]
        mx = jnp.maximum(jnp.maximum(l0, l1), l2)
        e0, e1, e2 = jnp.exp(l0 - mx), jnp.exp(l1 - mx), jnp.exp(l2 - mx)
        inv = 1.0 / (e0 + e1 + e2)
        src = lax.broadcasted_iota(jnp.int32, (LANES, heads * e), 0)
        dst = lax.broadcasted_iota(jnp.int32, (LANES, heads * e), 1)
        spread = jnp.where(src == dst // e, 1.0, 0.0).astype(BF16)

        def per_lane(w):
            hi = w.astype(BF16)
            lo = (w - hi.astype(F32)).astype(BF16)
            return _dot(hi, spread) + _dot(lo, spread)

        acc = per_lane(e0 * inv) * o0_ref[...].astype(F32)
        acc += per_lane(e1 * inv) * o1_ref[...].astype(F32)
        acc += per_lane(e2 * inv) * o2_ref[...].astype(F32)
        m_ref[...] = acc.astype(BF16)

    out_ref[...] = r_ref[...] + _dot(m_ref[...], w_ref[...])


def attention_layer(x, g, w_qkv, w_o, batch, seq):
    t, d = x.shape
    n_dil = len(ATTN_PATTERNS)
    he = w_o.shape[0]
    heads = he // ATTN_HEAD_DIM
    n_sl = n_dil * heads
    slopes = (2.0 ** (-8.0 * jnp.arange(1, n_sl + 1, dtype=F32) / n_sl)).reshape(n_dil, heads)
    outs, lses = [], []
    for grp, (window, dilation) in enumerate(ATTN_PATTERNS):
        assert window // dilation == ATTN_BLOCK and (seq // dilation) % ATTN_BLOCK == 0
        qkv = norm_matmul_strided(x, g, w_qkv, grp * 3 * he, 3 * he, dilation, batch, seq)
        o, l = _attn_group(qkv, slopes[grp] * dilation, grp, dilation, batch, seq, heads)
        outs.append(o)
        lses.append(l)
    tm, tn = _tile(t, 512), _tile(d, 1024)
    ospec = pl.BlockSpec((tm, he), lambda i, j: (i, 0))
    lspec = pl.BlockSpec((tm, LANES), lambda i, j: (i, 0))
    return pl.pallas_call(
        functools.partial(_attn_out_kernel, heads=heads),
        out_shape=jax.ShapeDtypeStruct((t, d), F32),
        grid=(t // tm, d // tn),
        in_specs=[ospec, ospec, ospec, lspec, lspec, lspec,
                  pl.BlockSpec((he, tn), lambda i, j: (0, j)),
                  pl.BlockSpec((tm, tn), lambda i, j: (i, j))],
        out_specs=pl.BlockSpec((tm, tn), lambda i, j: (i, j)),
        scratch_shapes=[pltpu.VMEM((tm, he), BF16)],
        compiler_params=_cparams("parallel", "arbitrary"),
        name="attn_merge_out_proj",
    )(*outs, *lses, w_o, x)


def _s5_chunk_operators(log_dt, a_re, a_im, b_re, b_im, c_re, c_im, d_skip, chunk):
    hp = dict(precision=HIGHEST)
    n_g, n_p = a_re.shape
    n_c = b_re.shape[-1]
    dt = jnp.exp(log_dt)[:, None]
    mag = jnp.exp(dt * a_re)
    ab_re = mag * jnp.cos(dt * a_im)
    ab_im = mag * jnp.sin(dt * a_im)
    den = a_re * a_re + a_im * a_im
    zr = ab_re - 1.0
    cr = (zr * a_re + ab_im * a_im) / den
    ci = (ab_im * a_re - zr * a_im) / den
    bb_re = cr[..., None] * b_re - ci[..., None] * b_im
    bb_im = cr[..., None] * b_im + ci[..., None] * b_re
    pr, pi = [jnp.ones_like(ab_re)], [jnp.zeros_like(ab_re)]
    for _ in range(chunk):
        pr, pi = pr + [pr[-1] * ab_re - pi[-1] * ab_im], pi + [pr[-1] * ab_im + pi[-1] * ab_re]
    pr, pi = jnp.stack(pr), jnp.stack(pi)
    ce_re = c_re[None] * pr[:, :, None, :] - c_im[None] * pi[:, :, None, :]
    ce_im = c_re[None] * pi[:, :, None, :] + c_im[None] * pr[:, :, None, :]
    kern = (jnp.einsum('tgcp,gpd->tgcd', ce_re[:chunk], bb_re, **hp)
            - jnp.einsum('tgcp,gpd->tgcd', ce_im[:chunk], bb_im, **hp))
    kern = kern.at[0].add(jax.vmap(jnp.diag)(d_skip))
    gpb = LANES // n_c
    nblk = n_g // gpb
    lag = jnp.arange(chunk)[None, :] - jnp.arange(chunk)[:, None]
    m_op = jnp.where((lag >= 0)[:, :, None, None, None], kern[jnp.maximum(lag, 0)], 0.0)
    m_op = m_op.transpose(2, 0, 4, 1, 3).reshape(n_g, chunk * n_c, chunk * n_c)
    qr, qi = pr[chunk - 1::-1][:chunk], pi[chunk - 1::-1][:chunk]
    bo_re = qr[..., None] * bb_re[None] - qi[..., None] * bb_im[None]
    bo_im = qr[..., None] * bb_im[None] + qi[..., None] * bb_re[None]
    flat_b = lambda a: a.transpose(1, 0, 3, 2).reshape(n_g, chunk * n_c, n_p)
    bo_re, bo_im = flat_b(bo_re), flat_b(bo_im)
    b_op = jnp.concatenate([bo_re, bo_im, bo_im, bo_re], axis=-1)
    flat_c = lambda a: a.transpose(1, 3, 0, 2).reshape(n_g, n_p, chunk * n_c)
    c_op = jnp.concatenate([flat_c(ce_re[1:]), -flat_c(ce_im[1:])], axis=1)
    al_re, al_im = pr[chunk], pi[chunk]
    per_blk = lambda parts: jnp.concatenate(parts, axis=-1).reshape(nblk, 1, gpb * 2 * n_p)
    coef_same = jnp.concatenate([per_blk([al_re, al_re])] * 2, axis=-1)
    coef_cross = jnp.concatenate([per_blk([-al_im, al_im]), per_blk([al_im, -al_im])], axis=-1)
    return m_op.astype(BF16), b_op.astype(BF16), c_op.astype(BF16), coef_same, coef_cross


def _chunk_rows(u_ref):
    return jnp.concatenate([u_ref[l] for l in range(u_ref.shape[0])], axis=-1)


def _first_visit_of_block():
    return (pl.program_id(1) == 0) & (pl.program_id(2) == 0)


def _s5_in_kernel(u_ref, b_ref, o_ref, dense_ref, *, n_c):
    @pl.when(_first_visit_of_block())
    def _():
        gpb, rows, cols = b_ref.shape
        half = cols // 2
        dense_ref[...] = jnp.zeros_like(dense_ref)
        for gi in range(gpb):
            for l in range(rows // n_c):
                r0 = l * LANES + gi * n_c
                piece = b_ref[gi, l * n_c:(l + 1) * n_c, :]
                dense_ref[r0:r0 + n_c, gi * half:(gi + 1) * half] = piece[:, :half]
                dense_ref[r0:r0 + n_c, (gpb + gi) * half:(gpb + gi + 1) * half] = piece[:, half:]

    o_ref[...] = _dot(_chunk_rows(u_ref), dense_ref[...])


def _s5_scan_kernel(xin_ref, cs_ref, cc_ref, o_ref, st_ref, *, half):
    @pl.when(pl.program_id(2) == 0)
    def _():
        st_ref[...] = jnp.zeros_like(st_ref)

    cs, cc = cs_ref[...], cc_ref[...]

    def step(n, st):
        o_ref[pl.ds(n, 1), :] = st[:, :half]
        st_sw = jnp.concatenate([st[:, half:], st[:, :half]], axis=-1)
        return cs * st + cc * st_sw + xin_ref[pl.ds(n, 1), :]

    st_ref[...] = lax.fori_loop(0, xin_ref.shape[0], step, st_ref[...])


def _s5_out_kernel(u_ref, m_ref, xp_ref, c_ref, o_ref, mdense_ref, cdense_ref, *, n_c):
    @pl.when(_first_visit_of_block())
    def _():
        gpb, rows, cols = m_ref.shape
        n_q = c_ref.shape[1]
        src = lax.broadcasted_iota(jnp.int32, (cols, mdense_ref.shape[1]), 0)
        dst = lax.broadcasted_iota(jnp.int32, (cols, mdense_ref.shape[1]), 1)
        for gi in range(gpb):
            spread = jnp.where(dst == (src // n_c) * LANES + gi * n_c + src % n_c, 1.0, 0.0).astype(BF16)
            wide = _dot(m_ref[gi], spread).astype(BF16)
            for l in range(rows // n_c):
                r0 = l * LANES + gi * n_c
                mdense_ref[r0:r0 + n_c, :] = wide[l * n_c:(l + 1) * n_c, :]
            cdense_ref[gi * n_q:(gi + 1) * n_q, :] = _dot(c_ref[gi], spread).astype(BF16)

    y = _dot(_chunk_rows(u_ref), mdense_ref[...]) + _dot(xp_ref[...].astype(BF16), cdense_ref[...])
    y = jax.nn.gelu(y).astype(o_ref.dtype)
    for step in range(o_ref.shape[0]):
        o_ref[step] = y[:, step * LANES:(step + 1) * LANES]


def _glu_out_kernel(y_ref, wa_ref, wb_ref, r_ref, o_ref, yn_ref, *, dilation):
    @pl.when(pl.program_id(1) == 0)
    def _():
        tm, kdim = yn_ref.shape
        per = tm // dilation
        nat = lax.broadcasted_iota(jnp.int32, (tm, tm), 0)
        src = lax.broadcasted_iota(jnp.int32, (tm, tm), 1)
        perm = jnp.where(src == (nat % dilation) * per + nat // dilation, 1.0, 0.0).astype(BF16)
        yn_ref[...] = _dot(perm, y_ref[...].reshape(tm, kdim)).astype(BF16)

    y = yn_ref[...]
    o_ref[...] = r_ref[...] + _dot(y, wa_ref[...]) * jax.nn.sigmoid(_dot(y, wb_ref[...]))


def s5_layer(x, g, w_in, log_dt, a_re, a_im, b_re, b_im, c_re, c_im, d_skip, w_out, batch, seq):
    t, d = x.shape
    n_g, n_p = a_re.shape
    n_c = SSM_CH
    ck = SSM_CHUNK
    gc = n_g * n_c
    nblk = gc // LANES
    n_chunks = seq // ck
    sw = 4 * n_p * (LANES // n_c)
    m_op, b_op, c_op, coef_same, coef_cross = _s5_chunk_operators(
        log_dt, a_re, a_im, b_re, b_im, c_re, c_im, d_skip, ck)
    u = norm_matmul_strided(x, g, w_in, 0, gc, ck, batch, seq)
    tr = _tile(n_chunks, 512)
    u_spec = pl.BlockSpec((None, ck, tr, LANES), lambda k, b, n: (b, 0, n, k))
    gpb = LANES // n_c
    group_ops = lambda a: pl.BlockSpec((gpb,) + a.shape[1:], lambda k, b, n: (k, 0, 0))
    xin = pl.pallas_call(
        functools.partial(_s5_in_kernel, n_c=n_c),
        out_shape=jax.ShapeDtypeStruct((batch, n_chunks, nblk * sw), F32),
        grid=(nblk, batch, n_chunks // tr),
        in_specs=[u_spec, group_ops(b_op)],
        out_specs=pl.BlockSpec((None, tr, sw), lambda k, b, n: (b, n, k)),
        scratch_shapes=[pltpu.VMEM((ck * LANES, sw), BF16)],
        compiler_params=_cparams("arbitrary", "arbitrary", "arbitrary"),
        name="s5_chunk_inputs",
    )(u, b_op)
    coef_spec = pl.BlockSpec((None, 1, sw), lambda b, k, n: (k, 0, 0))
    xprev = pl.pallas_call(
        functools.partial(_s5_scan_kernel, half=sw // 2),
        out_shape=jax.ShapeDtypeStruct((batch, n_chunks, nblk * sw // 2), F32),
        grid=(batch, nblk, n_chunks // tr),
        in_specs=[pl.BlockSpec((None, tr, sw), lambda b, k, n: (b, n, k)), coef_spec, coef_spec],
        out_specs=pl.BlockSpec((None, tr, sw // 2), lambda b, k, n: (b, n, k)),
        scratch_shapes=[pltpu.VMEM((1, sw), F32)],
        compiler_params=_cparams("parallel", "parallel", "arbitrary"),
        name="s5_chunk_scan",
    )(xin, coef_same, coef_cross)
    y = pl.pallas_call(
        functools.partial(_s5_out_kernel, n_c=n_c),
        out_shape=jax.ShapeDtypeStruct((batch, ck, n_chunks, gc), BF16),
        grid=(nblk, batch, n_chunks // tr),
        in_specs=[u_spec, group_ops(m_op),
                  pl.BlockSpec((None, tr, sw // 2), lambda k, b, n: (b, n, k)), group_ops(c_op)],
        out_specs=u_spec,
        scratch_shapes=[pltpu.VMEM((ck * LANES, ck * LANES), BF16), pltpu.VMEM((sw // 2, ck * LANES), BF16)],
        compiler_params=_cparams("arbitrary", "arbitrary", "arbitrary"),
        name="s5_chunk_outputs",
    )(u, m_op, xprev, c_op)
    tm, tn2 = _tile(seq, 512), _tile(d, 512)
    nt, nj = seq // tm, d // tn2
    return pl.pallas_call(
        functools.partial(_glu_out_kernel, dilation=ck),
        out_shape=jax.ShapeDtypeStruct((t, d), F32),
        grid=(t // tm, nj),
        in_specs=[pl.BlockSpec((None, ck, tm // ck, gc), lambda i, j: (i // nt, 0, i % nt, 0)),
                  pl.BlockSpec((gc, tn2), lambda i, j: (0, j)),
                  pl.BlockSpec((gc, tn2), lambda i, j: (0, j + nj)),
                  pl.BlockSpec((tm, tn2), lambda i, j: (i, j))],
        out_specs=pl.BlockSpec((tm, tn2), lambda i, j: (i, j)),
        scratch_shapes=[pltpu.VMEM((tm, gc), BF16)],
        compiler_params=_cparams("parallel", "arbitrary"),
        name="s5_glu_out_proj",
    )(y, w_out, w_out, x)


def _shift_norm(x_ref, xp_ref, g_ref, first):
    g = g_ref[...]
    h = _rms(x_ref[...], g)
    prev = _rms(xp_ref[7:8, :], g)
    prev = jnp.where(first, 0.0, prev)
    row = lax.broadcasted_iota(jnp.int32, h.shape, 0)
    hp = jnp.where(row == 0, prev, pltpu.roll(h, 1, 0))
    return h, hp


def _rwkv_proj_kernel(x_ref, xp_ref, g_ref, mu_ref, w_ref, o_ref, h_ref, d_ref, l_ref, *, tm, seq):
    i, j, n = pl.program_id(0), pl.program_id(1), pl.program_id(2)

    @pl.when((j == 0) & (n == 0))
    def _():
        h, hp = _shift_norm(x_ref, xp_ref, g_ref, (i * tm) % seq == 0)
        h_ref[...] = h
        d_ref[...] = hp - h

    @pl.when(n == 0)
    def _():
        l_ref[...] = (h_ref[...] + d_ref[...] * mu_ref[...]).astype(BF16)

    o_ref[...] = _dot(l_ref[...], w_ref[...])


def _softplus(z):
    return jnp.maximum(z, 0.0) + jnp.log(1.0 + jnp.exp(-jnp.abs(z)))


def _rwkv_lora_kernel(x_ref, xp_ref, g_ref, mu_ref, w0_ref, w1_ref, w2_ref, a0_ref, a1_ref, a2_ref,
                      g1_ref, g2_ref, lw_ref, a_ref, gate_ref, *, tm, seq):
    h, hp = _shift_norm(x_ref, xp_ref, g_ref, (pl.program_id(0) * tm) % seq == 0)
    dlt = hp - h
    xw = (h + dlt * mu_ref[0:1, :]).astype(BF16)
    xa = (h + dlt * mu_ref[1:2, :]).astype(BF16)
    xg = (h + dlt * mu_ref[2:3, :]).astype(BF16)
    wl = w0_ref[...] + _dot(jnp.tanh(_dot(xw, w1_ref[...])).astype(BF16), w2_ref[...])
    w = -_softplus(-wl) - 0.5
    lw_ref[...] = -jnp.exp(w)
    a_ref[...] = jax.nn.sigmoid(a0_ref[...] + _dot(_dot(xa, a1_ref[...]).astype(BF16), a2_ref[...]))
    gate_ref[...] = _dot(jax.nn.sigmoid(_dot(xg, g1_ref[...])).astype(BF16), g2_ref[...])


def _rwkv_core_kernel(r_ref, k_ref, v_ref, lw_ref, a_ref, gate_ref, kk_ref, ka_ref, rk_ref, lnw_ref, lnb_ref,
                      o_ref, s_ref, lhs_ref, rhs_ref, bk_ref, v2_ref, dec_ref, y_ref):
    ck, hd = RWKV_CHUNK, RWKV_HEAD_DIM
    nb, tc, width = r_ref.shape
    nch, ck2 = tc // ck, 2 * ck
    seqs = [(bi, slice(pi * LANES, (pi + 1) * LANES)) for bi in range(nb) for pi in range(width // LANES)]

    @pl.when(pl.program_id(1) == 0)
    def _():
        s_ref[...] = jnp.zeros_like(s_ref)

    lane = lax.broadcasted_iota(jnp.int32, (1, 1, LANES), 2)
    head_a = lane < hd
    hrow = lax.broadcasted_iota(jnp.int32, (LANES, LANES), 0) // hd
    hcol = lax.broadcasted_iota(jnp.int32, (LANES, LANES), 1) // hd
    head_ones = jnp.where(hrow == hcol, 1.0, 0.0).astype(BF16)
    trow = lax.broadcasted_iota(jnp.int32, (tc, tc), 0)
    tcol = lax.broadcasted_iota(jnp.int32, (tc, tc), 1)
    same_chunk = trow // ck == tcol // ck
    chunk_tril = jnp.where(same_chunk & (tcol <= trow), 1.0, 0.0).astype(BF16)
    chunk_ones = jnp.where(same_chunk, 1.0, 0.0).astype(BF16)

    def chunk_sums(x):
        hi, lo = _split2(x)
        return _dot(chunk_tril, hi) + _dot(chunk_tril, lo), _dot(chunk_ones, hi) + _dot(chunk_ones, lo)

    def head_sum(x):
        hi, lo = _split2(x)
        return _dot(hi, head_ones) + _dot(lo, head_ones)

    def stack_heads(x):
        xb = x.astype(BF16)
        zero = jnp.zeros_like(xb)
        return jnp.concatenate([jnp.where(head_a, xb, zero), jnp.where(head_a, zero, xb)], axis=1)

    for si, (bi, ls) in enumerate(seqs):
        k_all, a_all = k_ref[bi, :, ls], a_ref[bi, :, ls]
        kk = k_all * kk_ref[:, ls]
        kk = kk * lax.rsqrt(jnp.maximum(head_sum(kk * kk), 1e-24))
        k2 = k_all * (1.0 + (a_all - 1.0) * ka_ref[:, ls])
        lw = lw_ref[bi, :, ls]
        by_chunk = lambda x: x.reshape(nch, ck, LANES)
        cs, tot = chunk_sums(lw)
        cs, tot = by_chunk(cs), by_chunk(tot)
        gam_inv, gam_rem = jnp.exp(-cs), jnp.exp(tot - cs)
        atm = by_chunk(-kk) * jnp.exp(cs - by_chunk(lw))
        rm = by_chunk(r_ref[bi, :, ls]) * jnp.exp(cs)
        b3, k3 = by_chunk(kk * a_all), by_chunk(k2)
        lhs_ref[si] = jnp.concatenate([stack_heads(atm), stack_heads(rm)], axis=1)
        rhs_ref[si] = jnp.concatenate([stack_heads(b3 * gam_inv), stack_heads(k3 * gam_inv)], axis=1)
        bk_ref[si] = jnp.concatenate([stack_heads(b3 * gam_rem), stack_heads(k3 * gam_rem)], axis=1)
        v2_ref[si] = stack_heads(by_chunk(v_ref[bi, :, ls]))
        dec_ref[si] = jnp.exp(tot[:, 0:1, :])

    row = lax.broadcasted_iota(jnp.int32, (ck2, ck2), 0)
    col = lax.broadcasted_iota(jnp.int32, (ck2, ck2), 1)
    incl = col <= row
    strict = col < row
    eye = jnp.where(row == col, 1.0, 0.0)
    n_seq = len(seqs)
    each = lambda f, *lists: [f(*args) for args in zip(*lists)]

    def chunk_step(ci, carry):
        lhs = [lhs_ref[si, ci] for si in range(n_seq)]
        gram = each(lambda l, si: _dot_nt(l, rhs_ref[si, ci]), lhs, range(n_seq))
        a_ab = each(lambda g: jnp.where(strict, g[:ck2, :ck2], 0.0), gram)
        a_lo = each(lambda g: jnp.concatenate([jnp.where(strict, g[:ck2, ck2:], 0.0),
                                               jnp.where(incl, g[ck2:, ck2:], 0.0)], axis=0).astype(BF16), gram)
        a_rb = each(lambda g: jnp.where(incl, g[ck2:, :ck2], 0.0).astype(BF16), gram)
        inv = each(lambda a: eye + a, a_ab)
        pw = each(lambda a: _dot(a.astype(BF16), a.astype(BF16)), a_ab)
        m = 2
        while 2 * m < ck:
            pwb = each(lambda p: p.astype(BF16), pw)
            both = each(lambda p, t: _dot(jnp.concatenate([p, t.astype(BF16)], axis=0), p), pwb, inv)
            pw = each(lambda z: z[:ck2], both)
            inv = each(lambda t, z: t + z[ck2:], inv, both)
            m *= 2
        inv = each(lambda t, p: t + _dot(t.astype(BF16), p.astype(BF16)), inv, pw)
        v2 = [v2_ref[si, ci] for si in range(n_seq)]
        av = each(_dot, a_lo, v2)
        s = [s_ref[si] for si in range(n_seq)]
        xs = each(lambda l, st: _dot_nt(l, st.astype(BF16)), lhs, s)
        u = each(lambda t, x, w: _dot(t.astype(BF16), (x[:ck2] + w[:ck2]).astype(BF16)).astype(BF16), inv, xs, av)
        y2 = each(lambda x, w, arb, ub: x[ck2:] + w[ck2:] + _dot(arb, ub), xs, av, a_rb, u)
        sl = pl.ds(pl.multiple_of(ci * ck, ck), ck)
        for si, (bi, ls) in enumerate(seqs):
            y_ref[bi, sl, ls] = y2[si][:ck] + y2[si][ck:]
            uv = jnp.concatenate([u[si], v2[si]], axis=0)
            s_ref[si] = s[si] * dec_ref[si, ci] + _dot_tn(uv, bk_ref[si, ci])
        return carry

    lax.fori_loop(0, nch, chunk_step, 0)

    for bi, ls in seqs:
        y = y_ref[bi, :, ls]
        mean = head_sum(y) * (1.0 / hd)
        yc = y - mean
        var = head_sum(yc * yc) * (1.0 / hd)
        yn = yc * lax.rsqrt(var + RWKV_GN_EPS) * lnw_ref[:, ls] + lnb_ref[:, ls]
        k2 = k_ref[bi, :, ls] * (1.0 + (a_ref[bi, :, ls] - 1.0) * ka_ref[:, ls])
        bonus = head_sum(r_ref[bi, :, ls] * k2 * rk_ref[:, ls]) * v_ref[bi, :, ls]
        o_ref[bi, :, ls] = ((yn + bonus) * gate_ref[bi, :, ls]).astype(o_ref.dtype)


def _pad_to(a, axis, size):
    pad = [(0, 0)] * a.ndim
    pad[axis] = (0, size - a.shape[axis])
    return jnp.pad(a, pad)


def rwkv_layer(x, g, mu, w_rkv, w0, w1, w2, a0, a1, a2, g1, g2, k_k, k_a, r_k, ln_w, ln_b, w_o, batch, seq):
    t, d = x.shape
    row = lambda p: p.reshape(1, d).astype(F32)
    g2d = g.reshape(1, d)
    tn = _tile(d, 1024)
    prev_spec = lambda tm: pl.BlockSpec((8, d), (lambda i, *_: (jnp.maximum(i * (tm // 8) - 1, 0), 0)))
    tm = _tile(seq, 512)
    rkv = pl.pallas_call(
        functools.partial(_rwkv_proj_kernel, tm=tm, seq=seq),
        out_shape=jax.ShapeDtypeStruct((3, t, d), F32),
        grid=(t // tm, 3, d // tn),
        in_specs=[pl.BlockSpec((tm, d), lambda i, j, n: (i, 0)),
                  prev_spec(tm),
                  pl.BlockSpec((1, d), lambda i, j, n: (0, 0)),
                  pl.BlockSpec((None, 1, d), lambda i, j, n: (j, 0, 0)),
                  pl.BlockSpec((None, d, tn), lambda i, j, n: (j, 0, n))],
        out_specs=pl.BlockSpec((None, tm, tn), lambda i, j, n: (j, i, n)),
        scratch_shapes=[pltpu.VMEM((tm, d), F32), pltpu.VMEM((tm, d), F32), pltpu.VMEM((tm, d), BF16)],
        compiler_params=_cparams("parallel", "arbitrary", "arbitrary"),
        name="rwkv_rkv_proj",
    )(x, x, g2d, mu[:3].reshape(3, 1, d), w_rkv)

    pad_rank = lambda w_a, w_b: (_pad_to(w_a, 1, -(-w_a.shape[1] // LANES) * LANES).astype(BF16),
                                 _pad_to(w_b, 0, -(-w_b.shape[0] // LANES) * LANES).astype(BF16))
    w1p, w2p = pad_rank(w1, w2)
    a1p, a2p = pad_rank(a1, a2)
    g1p, g2p = pad_rank(g1, g2)
    full = lambda a: pl.BlockSpec(a.shape, lambda i: (0,) * a.ndim)
    tm = _tile(seq, 256)
    tok = pl.BlockSpec((tm, d), lambda i: (i, 0))
    lora_in = [x, x, g2d, mu[3:6], row(w0), w1p, w2p, row(a0), a1p, a2p, g1p, g2p]
    lw, a_gate, gate = pl.pallas_call(
        functools.partial(_rwkv_lora_kernel, tm=tm, seq=seq),
        out_shape=(jax.ShapeDtypeStruct((t, d), F32),) * 3,
        grid=(t // tm,),
        in_specs=[tok, prev_spec(tm)] + [full(a) for a in lora_in[2:]],
        out_specs=(tok, tok, tok),
        compiler_params=_cparams("parallel"),
        name="rwkv_lora",
    )(*lora_in)

    tc = _tile(seq, 512)
    wd = _tile(d, 4 * LANES)
    tokc = pl.BlockSpec((batch, tc, wd), lambda p, c: (0, c, p))
    rkvc = lambda which: pl.BlockSpec((None, batch, tc, wd), lambda p, c: (which, 0, c, p))
    par = pl.BlockSpec((1, wd), lambda p, c: (0, p))
    n_seq = batch * (wd // LANES)
    nch = tc // RWKV_CHUNK
    bsd = lambda a: a.reshape(batch, seq, d)
    rkv4 = rkv.reshape(3, batch, seq, d)
    mixed = pl.pallas_call(
        _rwkv_core_kernel,
        out_shape=jax.ShapeDtypeStruct((batch, seq, d), BF16),
        grid=(d // wd, seq // tc),
        in_specs=[rkvc(0), rkvc(1), rkvc(2), tokc, tokc, tokc, par, par, par, par, par],
        out_specs=tokc,
        scratch_shapes=[pltpu.VMEM((n_seq, LANES, LANES), F32)]
        + [pltpu.VMEM((n_seq, nch, 4 * RWKV_CHUNK, LANES), BF16)] * 3
        + [pltpu.VMEM((n_seq, nch, 2 * RWKV_CHUNK, LANES), BF16),
           pltpu.VMEM((n_seq, nch, 1, LANES), F32),
           pltpu.VMEM((batch, tc, wd), F32)],
        compiler_params=_cparams("parallel", "arbitrary"),
        name="rwkv_chunked_state",
    )(rkv4, rkv4, rkv4, bsd(lw), bsd(a_gate), bsd(gate), row(k_k), row(k_a), row(r_k), row(ln_w), row(ln_b))
    return matmul_residual(mixed.reshape(t, d), w_o, x)


def kernel(x, norm_mix, norm_mlp, norm_f, attn_w_qkv, attn_w_o, ssm_w_in, ssm_log_dt, ssm_a_re, ssm_a_im,
           ssm_b_re, ssm_b_im, ssm_c_re, ssm_c_im, ssm_d, ssm_w_out, rwkv_mu, rwkv_w_rkv, rwkv_w0, rwkv_w1,
           rwkv_w2, rwkv_a0, rwkv_a1, rwkv_a2, rwkv_g1, rwkv_g2, rwkv_k_k, rwkv_k_a, rwkv_r_k, rwkv_ln_w,
           rwkv_ln_b, rwkv_w_o, mlp_w1, mlp_w2):
    batch, seq, d = x.shape
    depth = norm_mix.shape[0]
    bf = lambda w: w.astype(BF16)
    h = x.reshape(batch * seq, d)
    ia = ib = ic = 0
    for layer in range(depth):
        kind = layer % 3
        if kind == 0:
            h = attention_layer(h, norm_mix[layer], bf(attn_w_qkv[ia]), bf(attn_w_o[ia]), batch, seq)
            ia += 1
        elif kind == 1:
            h = s5_layer(h, norm_mix[layer], bf(ssm_w_in[ib]), ssm_log_dt[ib], ssm_a_re[ib], ssm_a_im[ib],
                         ssm_b_re[ib], ssm_b_im[ib], ssm_c_re[ib], ssm_c_im[ib], ssm_d[ib],
                         bf(ssm_w_out[ib]), batch, seq)
            ib += 1
        else:
            h = rwkv_layer(h, norm_mix[layer], rwkv_mu[ic], bf(rwkv_w_rkv[ic]), rwkv_w0[ic], rwkv_w1[ic],
                           rwkv_w2[ic], rwkv_a0[ic], rwkv_a1[ic], rwkv_a2[ic], rwkv_g1[ic], rwkv_g2[ic],
                           rwkv_k_k[ic], rwkv_k_a[ic], rwkv_r_k[ic], rwkv_ln_w[ic], rwkv_ln_b[ic],
                           bf(rwkv_w_o[ic]), batch, seq)
            ic += 1
        g_final = norm_f if layer == depth - 1 else None
        h = mlp_residual(h, norm_mlp[layer], bf(mlp_w1[layer]), bf(mlp_w2[layer]), g_final)
    return h.reshape(batch, seq, d)
```

```python
import functools

import jax
import jax.numpy as jnp
from jax import lax
from jax.experimental import pallas as pl
from jax.experimental.pallas import tpu as pltpu

F32 = jnp.float32
BF16 = jnp.bfloat16
HIGHEST = lax.Precision.HIGHEST

NORM_EPS = 1e-5
LANES = 128
VMEM_LIMIT_BYTES = 56 * 2**20
MASK_VALUE = -1e30

ATTN_PATTERNS = ((128, 1), (512, 4), (2048, 16))
ATTN_BLOCK = 128
ATTN_HEAD_DIM = 128
SSM_CH = 16
SSM_CHUNK = 16
SSM_DT_MIN = 0.001
SSM_DT_MAX = 0.1
RWKV_HEAD_DIM = 64
RWKV_CHUNK = 64
RWKV_GN_EPS = RWKV_HEAD_DIM * 1e-5


def _cparams(*sem):
    return pltpu.CompilerParams(dimension_semantics=sem, vmem_limit_bytes=VMEM_LIMIT_BYTES)


def _tile(n, pref):
    t = min(n, pref)
    while n % t:
        t //= 2
    return t


def _rms(x, g):
    ms = jnp.mean(x * x, axis=-1, keepdims=True)
    return x * lax.rsqrt(ms + NORM_EPS) * g


def _dot(a, b):
    return jnp.dot(a, b, preferred_element_type=F32)


def _dot_nt(a, b):
    return lax.dot_general(a, b, (((1,), (1,)), ((), ())), preferred_element_type=F32)


def _dot_tn(a, b):
    return lax.dot_general(a, b, (((0,), (0,)), ((), ())), preferred_element_type=F32)


def _split2(x):
    hi = x.astype(BF16)
    return hi, (x - hi.astype(F32)).astype(BF16)


def _norm_matmul_strided_kernel(x_ref, g_ref, w_ref, o_ref, h_ref, *, dilation, sub):
    tm = h_ref.shape[0]
    per = sub // dilation

    @pl.when(pl.program_id(1) == 0)
    def _():
        if dilation > 1:
            new = lax.broadcasted_iota(jnp.int32, (sub, sub), 0)
            old = lax.broadcasted_iota(jnp.int32, (sub, sub), 1)
            perm = jnp.where(old == (new % per) * dilation + new // per, 1.0, 0.0).astype(BF16)
        for s in range(tm // sub):
            rows = slice(s * sub, (s + 1) * sub)
            h = _rms(x_ref[rows, :], g_ref[...]).astype(BF16)
            h_ref[rows, :] = _dot(perm, h).astype(BF16) if dilation > 1 else h

    y = _dot(h_ref[...], w_ref[...]).astype(o_ref.dtype)
    for s in range(tm // sub):
        o_ref[:, s * per:(s + 1) * per, :] = y[s * sub:(s + 1) * sub].reshape(dilation, per, y.shape[-1])


def norm_matmul_strided(x, g, w, col0, ncols, dilation, batch, seq, tm=1024, tn=1024, sub=512):
    t, d = x.shape
    tm, tn = _tile(seq, tm), _tile(ncols, tn)
    sub = min(sub, tm)
    assert col0 % tn == 0 and (sub // dilation) % 16 == 0
    nt = seq // tm
    return pl.pallas_call(
        functools.partial(_norm_matmul_strided_kernel, dilation=dilation, sub=sub),
        out_shape=jax.ShapeDtypeStruct((batch, dilation, seq // dilation, ncols), BF16),
        grid=(t // tm, ncols // tn),
        in_specs=[pl.BlockSpec((tm, d), lambda i, j: (i, 0)),
                  pl.BlockSpec((1, d), lambda i, j: (0, 0)),
                  pl.BlockSpec((d, tn), lambda i, j: (0, col0 // tn + j))],
        out_specs=pl.BlockSpec((None, dilation, tm // dilation, tn), lambda i, j: (i // nt, 0, i % nt, j)),
        scratch_shapes=[pltpu.VMEM((tm, d), BF16)],
        compiler_params=_cparams("parallel", "arbitrary"),
        name=f"norm_matmul_stride{dilation}",
    )(x, g.reshape(1, d), w)


def _matmul_res_kernel(a_ref, w_ref, r_ref, o_ref):
    o_ref[...] = r_ref[...] + _dot(a_ref[...], w_ref[...])


def matmul_residual(a, w, res, tm=512, tn=1024):
    t, k = a.shape
    n = w.shape[1]
    tm, tn = _tile(t, tm), _tile(n, tn)
    return pl.pallas_call(
        _matmul_res_kernel,
        out_shape=jax.ShapeDtypeStruct((t, n), F32),
        grid=(t // tm, n // tn),
        in_specs=[pl.BlockSpec((tm, k), lambda i, j: (i, 0)),
                  pl.BlockSpec((k, tn), lambda i, j: (0, j)),
                  pl.BlockSpec((tm, tn), lambda i, j: (i, j))],
        out_specs=pl.BlockSpec((tm, tn), lambda i, j: (i, j)),
        compiler_params=_cparams("parallel", "parallel"),
        name="matmul_residual",
    )(a, w, res)


def _mlp_kernel(x_ref, g_ref, w1_ref, w2_ref, gf_ref, o_ref, h_ref, *, final_norm):
    f = pl.program_id(1)

    @pl.when(f == 0)
    def _():
        x = x_ref[...]
        h_ref[...] = _rms(x, g_ref[...]).astype(BF16)
        o_ref[...] = x

    a = _dot(h_ref[...], w1_ref[...])
    a = jnp.square(jnp.maximum(a, 0.0)).astype(BF16)
    o_ref[...] += _dot(a, w2_ref[...])

    if final_norm:
        @pl.when(f == pl.num_programs(1) - 1)
        def _():
            o_ref[...] = _rms(o_ref[...], gf_ref[...])


def mlp_residual(x, g, w1, w2, g_final=None, tm=512, tf=1024):
    t, d = x.shape
    ff = w1.shape[1]
    tm, tf = _tile(t, tm), _tile(ff, tf)
    final_norm = g_final is not None
    gf = (g_final if final_norm else g).reshape(1, d)
    return pl.pallas_call(
        functools.partial(_mlp_kernel, final_norm=final_norm),
        out_shape=jax.ShapeDtypeStruct((t, d), F32),
        grid=(t // tm, ff // tf),
        in_specs=[pl.BlockSpec((tm, d), lambda i, f: (i, 0)),
                  pl.BlockSpec((1, d), lambda i, f: (0, 0)),
                  pl.BlockSpec((d, tf), lambda i, f: (0, f)),
                  pl.BlockSpec((tf, d), lambda i, f: (f, 0)),
                  pl.BlockSpec((1, d), lambda i, f: (0, 0))],
        out_specs=pl.BlockSpec((tm, d), lambda i, f: (i, 0)),
        scratch_shapes=[pltpu.VMEM((tm, d), BF16)],
        compiler_params=_cparams("parallel", "arbitrary"),
        name="mlp_residual",
    )(x, g.reshape(1, d), w1, w2, gf)


def _attn_kernel(slope_ref, q_ref, kp_ref, kc_ref, vp_ref, vc_ref, o_ref, lse_ref, *, heads, scale):
    blk, e = ATTN_BLOCK, ATTN_HEAD_DIM
    j = pl.program_id(2)
    qi = lax.broadcasted_iota(jnp.int32, (blk, blk), 0)
    kj = lax.broadcasted_iota(jnp.int32, (blk, blk), 1)
    log2e, ln2 = 1.4426950408889634, 0.6931471805599453
    dist_c = (qi - kj).astype(F32) * log2e
    dist_p = dist_c + float(blk) * log2e
    mask_c = jnp.where(kj <= qi, 0.0, MASK_VALUE)
    mask_p = jnp.where((kj >= qi) & (j > 0), 0.0, MASK_VALUE)
    lane = lax.broadcasted_iota(jnp.int32, (blk, LANES), 1)
    lse_tile = jnp.zeros((blk, LANES), F32)
    together = next(n for n in (4, 2, 1) if heads % n == 0)
    for h0 in range(0, heads, together):
        hs = list(range(h0, h0 + together))
        sls = [slice(h * e, (h + 1) * e) for h in hs]
        qs = [q_ref[:, sl] for sl in sls]
        sc = [_dot_nt(q, kc_ref[:, sl]) for q, sl in zip(qs, sls)]
        sp = [_dot_nt(q, kp_ref[:, sl]) for q, sl in zip(qs, sls)]
        sc = [s * (scale * log2e) + (mask_c - slope_ref[h] * dist_c) for s, h in zip(sc, hs)]
        sp = [s * (scale * log2e) + (mask_p - slope_ref[h] * dist_p) for s, h in zip(sp, hs)]
        m = [jnp.max(jnp.maximum(a, b), axis=-1, keepdims=True) for a, b in zip(sc, sp)]
        pc = [jnp.exp2(a - mm) for a, mm in zip(sc, m)]
        pp = [jnp.exp2(b - mm) for b, mm in zip(sp, m)]
        den = [jnp.sum(a + b, axis=-1, keepdims=True) for a, b in zip(pc, pp)]
        o = [_dot(a.astype(BF16), vc_ref[:, sl]) + _dot(b.astype(BF16), vp_ref[:, sl])
             for a, b, sl in zip(pc, pp, sls)]
        for h, sl, oo, dd, mm in zip(hs, sls, o, den, m):
            o_ref[:, sl] = (oo / dd).astype(o_ref.dtype)
            lse_tile = jnp.where(lane == h, mm * ln2 + jnp.log(dd), lse_tile)
    lse_ref[...] = lse_tile


def _attn_group(qkv, slopes, group, dilation, batch, seq, heads):
    e, blk = ATTN_HEAD_DIM, ATTN_BLOCK
    he = heads * e
    sub = seq // dilation
    nb = sub // blk
    cur = lambda kind: pl.BlockSpec((None, None, blk, he), lambda b, r, j: (b, r, j, kind))
    prev = lambda kind: pl.BlockSpec((None, None, blk, he), lambda b, r, j: (b, r, jnp.maximum(j - 1, 0), kind))
    out, lse = pl.pallas_call(
        functools.partial(_attn_kernel, heads=heads, scale=e ** -0.5),
        out_shape=(jax.ShapeDtypeStruct((batch, dilation, sub, he), BF16),
                   jax.ShapeDtypeStruct((batch, dilation, sub, LANES), F32)),
        grid=(batch, dilation, nb),
        in_specs=[pl.BlockSpec(memory_space=pltpu.SMEM), cur(0), prev(1), cur(1), prev(2), cur(2)],
        out_specs=(pl.BlockSpec((None, None, blk, he), lambda b, r, j: (b, r, j, 0)),
                   pl.BlockSpec((None, None, blk, LANES), lambda b, r, j: (b, r, j, 0))),
        compiler_params=_cparams("parallel", "parallel", "arbitrary"),
        name=f"dilated_attn_g{group}",
    )(slopes, qkv, qkv, qkv, qkv, qkv)
    natural = lambda a: a.transpose(0, 2, 1, 3).reshape(batch * seq, a.shape[-1])
    return natural(out), natural(lse)


def _attn_out_kernel(o0_ref, o1_ref, o2_ref, l0_ref, l1_ref, l2_ref, w_ref, r_ref, out_ref, m_ref, *, heads):
    e = ATTN_HEAD_DIM

    @pl.when(pl.program_id(1) == 0)
    def _():
        l0, l1, l2 = l0_ref[...], l1_ref[...], l2_ref[...]
        mx = jnp.maximum(jnp.maximum(l0, l1), l2)
        e0, e1, e2 = jnp.exp(l0 - mx), jnp.exp(l1 - mx), jnp.exp(l2 - mx)
        inv = 1.0 / (e0 + e1 + e2)
        src = lax.broadcasted_iota(jnp.int32, (LANES, heads * e), 0)
        dst = lax.broadcasted_iota(jnp.int32, (LANES, heads * e), 1)
        spread = jnp.where(src == dst // e, 1.0, 0.0).astype(BF16)

        per_lane = lambda w: _dot(w.astype(BF16), spread)

        acc = per_lane(e0 * inv) * o0_ref[...].astype(F32)
        acc += per_lane(e1 * inv) * o1_ref[...].astype(F32)
        acc += per_lane(e2 * inv) * o2_ref[...].astype(F32)
        m_ref[...] = acc.astype(BF16)

    out_ref[...] = r_ref[...] + _dot(m_ref[...], w_ref[...])


def attention_layer(x, g, w_qkv, w_o, batch, seq):
    t, d = x.shape
    n_dil = len(ATTN_PATTERNS)
    he = w_o.shape[0]
    heads = he // ATTN_HEAD_DIM
    n_sl = n_dil * heads
    slopes = (2.0 ** (-8.0 * jnp.arange(1, n_sl + 1, dtype=F32) / n_sl)).reshape(n_dil, heads)
    outs, lses = [], []
    for grp, (window, dilation) in enumerate(ATTN_PATTERNS):
        assert window // dilation == ATTN_BLOCK and (seq // dilation) % ATTN_BLOCK == 0
        qkv = norm_matmul_strided(x, g, w_qkv, grp * 3 * he, 3 * he, dilation, batch, seq)
        o, l = _attn_group(qkv, slopes[grp] * dilation, grp, dilation, batch, seq, heads)
        outs.append(o)
        lses.append(l)
    tm, tn = _tile(t, 512), _tile(d, 1024)
    ospec = pl.BlockSpec((tm, he), lambda i, j: (i, 0))
    lspec = pl.BlockSpec((tm, LANES), lambda i, j: (i, 0))
    return pl.pallas_call(
        functools.partial(_attn_out_kernel, heads=heads),
        out_shape=jax.ShapeDtypeStruct((t, d), F32),
        grid=(t // tm, d // tn),
        in_specs=[ospec, ospec, ospec, lspec, lspec, lspec,
                  pl.BlockSpec((he, tn), lambda i, j: (0, j)),
                  pl.BlockSpec((tm, tn), lambda i, j: (i, j))],
        out_specs=pl.BlockSpec((tm, tn), lambda i, j: (i, j)),
        scratch_shapes=[pltpu.VMEM((tm, he), BF16)],
        compiler_params=_cparams("parallel", "arbitrary"),
        name="attn_merge_out_proj",
    )(*outs, *lses, w_o, x)


def _s5_chunk_operators(log_dt, a_re, a_im, b_re, b_im, c_re, c_im, d_skip, chunk):
    hp = dict(precision=HIGHEST)
    n_g, n_p = a_re.shape
    n_c = b_re.shape[-1]
    dt = jnp.exp(log_dt)[:, None]
    mag = jnp.exp(dt * a_re)
    ab_re = mag * jnp.cos(dt * a_im)
    ab_im = mag * jnp.sin(dt * a_im)
    den = a_re * a_re + a_im * a_im
    zr = ab_re - 1.0
    cr = (zr * a_re + ab_im * a_im) / den
    ci = (ab_im * a_re - zr * a_im) / den
    bb_re = cr[..., None] * b_re - ci[..., None] * b_im
    bb_im = cr[..., None] * b_im + ci[..., None] * b_re
    pr, pi = [jnp.ones_like(ab_re)], [jnp.zeros_like(ab_re)]
    for _ in range(chunk):
        pr, pi = pr + [pr[-1] * ab_re - pi[-1] * ab_im], pi + [pr[-1] * ab_im + pi[-1] * ab_re]
    pr, pi = jnp.stack(pr), jnp.stack(pi)
    ce_re = c_re[None] * pr[:, :, None, :] - c_im[None] * pi[:, :, None, :]
    ce_im = c_re[None] * pi[:, :, None, :] + c_im[None] * pr[:, :, None, :]
    kern = (jnp.einsum('tgcp,gpd->tgcd', ce_re[:chunk], bb_re, **hp)
            - jnp.einsum('tgcp,gpd->tgcd', ce_im[:chunk], bb_im, **hp))
    kern = kern.at[0].add(jax.vmap(jnp.diag)(d_skip))
    gpb = LANES // n_c
    nblk = n_g // gpb
    lag = jnp.arange(chunk)[None, :] - jnp.arange(chunk)[:, None]
    m_op = jnp.where((lag >= 0)[:, :, None, None, None], kern[jnp.maximum(lag, 0)], 0.0)
    m_op = m_op.transpose(2, 0, 4, 1, 3).reshape(n_g, chunk * n_c, chunk * n_c)
    qr, qi = pr[chunk - 1::-1][:chunk], pi[chunk - 1::-1][:chunk]
    bo_re = qr[..., None] * bb_re[None] - qi[..., None] * bb_im[None]
    bo_im = qr[..., None] * bb_im[None] + qi[..., None] * bb_re[None]
    flat_b = lambda a: a.transpose(1, 0, 3, 2).reshape(n_g, chunk * n_c, n_p)
    bo_re, bo_im = flat_b(bo_re), flat_b(bo_im)
    b_op = jnp.concatenate([bo_re, bo_im, bo_im, bo_re], axis=-1)
    flat_c = lambda a: a.transpose(1, 3, 0, 2).reshape(n_g, n_p, chunk * n_c)
    c_op = jnp.concatenate([flat_c(ce_re[1:]), -flat_c(ce_im[1:])], axis=1)
    al_re, al_im = pr[chunk], pi[chunk]
    per_blk = lambda parts: jnp.concatenate(parts, axis=-1).reshape(nblk, 1, gpb * 2 * n_p)
    coef_same = jnp.concatenate([per_blk([al_re, al_re])] * 2, axis=-1)
    coef_cross = jnp.concatenate([per_blk([-al_im, al_im]), per_blk([al_im, -al_im])], axis=-1)
    return m_op.astype(BF16), b_op.astype(BF16), c_op.astype(BF16), coef_same, coef_cross


def _chunk_rows(u_ref):
    return jnp.concatenate([u_ref[l] for l in range(u_ref.shape[0])], axis=-1)


def _first_visit_of_block():
    return (pl.program_id(1) == 0) & (pl.program_id(2) == 0)


def _s5_in_kernel(u_ref, b_ref, o_ref, dense_ref, *, n_c):
    @pl.when(_first_visit_of_block())
    def _():
        gpb, rows, cols = b_ref.shape
        half = cols // 2
        dense_ref[...] = jnp.zeros_like(dense_ref)
        for gi in range(gpb):
            for l in range(rows // n_c):
                r0 = l * LANES + gi * n_c
                piece = b_ref[gi, l * n_c:(l + 1) * n_c, :]
                dense_ref[r0:r0 + n_c, gi * half:(gi + 1) * half] = piece[:, :half]
                dense_ref[r0:r0 + n_c, (gpb + gi) * half:(gpb + gi + 1) * half] = piece[:, half:]

    o_ref[...] = _dot(_chunk_rows(u_ref), dense_ref[...])


def _s5_scan_kernel(xin_ref, cs_ref, cc_ref, o_ref, st_ref, *, half):
    @pl.when(pl.program_id(2) == 0)
    def _():
        st_ref[...] = jnp.zeros_like(st_ref)

    cs, cc = cs_ref[...], cc_ref[...]

    def step(n, st):
        o_ref[pl.ds(n, 1), :] = st[:, :half]
        st_sw = jnp.concatenate([st[:, half:], st[:, :half]], axis=-1)
        return cs * st + cc * st_sw + xin_ref[pl.ds(n, 1), :]

    st_ref[...] = lax.fori_loop(0, xin_ref.shape[0], step, st_ref[...])


def _s5_out_kernel(u_ref, m_ref, xp_ref, c_ref, o_ref, mdense_ref, cdense_ref, *, n_c):
    @pl.when(_first_visit_of_block())
    def _():
        gpb, rows, cols = m_ref.shape
        n_q = c_ref.shape[1]
        src = lax.broadcasted_iota(jnp.int32, (cols, mdense_ref.shape[1]), 0)
        dst = lax.broadcasted_iota(jnp.int32, (cols, mdense_ref.shape[1]), 1)
        for gi in range(gpb):
            spread = jnp.where(dst == (src // n_c) * LANES + gi * n_c + src % n_c, 1.0, 0.0).astype(BF16)
            wide = _dot(m_ref[gi], spread).astype(BF16)
            for l in range(rows // n_c):
                r0 = l * LANES + gi * n_c
                mdense_ref[r0:r0 + n_c, :] = wide[l * n_c:(l + 1) * n_c, :]
            cdense_ref[gi * n_q:(gi + 1) * n_q, :] = _dot(c_ref[gi], spread).astype(BF16)

    y = _dot(_chunk_rows(u_ref), mdense_ref[...]) + _dot(xp_ref[...].astype(BF16), cdense_ref[...])
    y = jax.nn.gelu(y).astype(o_ref.dtype)
    for step in range(o_ref.shape[0]):
        o_ref[step] = y[:, step * LANES:(step + 1) * LANES]


def _glu_out_kernel(y_ref, wa_ref, wb_ref, r_ref, o_ref, yn_ref, *, dilation):
    @pl.when(pl.program_id(1) == 0)
    def _():
        tm, kdim = yn_ref.shape
        per = tm // dilation
        nat = lax.broadcasted_iota(jnp.int32, (tm, tm), 0)
        src = lax.broadcasted_iota(jnp.int32, (tm, tm), 1)
        perm = jnp.where(src == (nat % dilation) * per + nat // dilation, 1.0, 0.0).astype(BF16)
        yn_ref[...] = _dot(perm, y_ref[...].reshape(tm, kdim)).astype(BF16)

    y = yn_ref[...]
    o_ref[...] = r_ref[...] + _dot(y, wa_ref[...]) * jax.nn.sigmoid(_dot(y, wb_ref[...]))


def s5_layer(x, g, w_in, log_dt, a_re, a_im, b_re, b_im, c_re, c_im, d_skip, w_out, batch, seq):
    t, d = x.shape
    n_g, n_p = a_re.shape
    n_c = SSM_CH
    ck = SSM_CHUNK
    gc = n_g * n_c
    nblk = gc // LANES
    n_chunks = seq // ck
    sw = 4 * n_p * (LANES // n_c)
    m_op, b_op, c_op, coef_same, coef_cross = _s5_chunk_operators(
        log_dt, a_re, a_im, b_re, b_im, c_re, c_im, d_skip, ck)
    u = norm_matmul_strided(x, g, w_in, 0, gc, ck, batch, seq)
    tr = _tile(n_chunks, 512)
    u_spec = pl.BlockSpec((None, ck, tr, LANES), lambda k, b, n: (b, 0, n, k))
    gpb = LANES // n_c
    group_ops = lambda a: pl.BlockSpec((gpb,) + a.shape[1:], lambda k, b, n: (k, 0, 0))
    xin = pl.pallas_call(
        functools.partial(_s5_in_kernel, n_c=n_c),
        out_shape=jax.ShapeDtypeStruct((batch, n_chunks, nblk * sw), F32),
        grid=(nblk, batch, n_chunks // tr),
        in_specs=[u_spec, group_ops(b_op)],
        out_specs=pl.BlockSpec((None, tr, sw), lambda k, b, n: (b, n, k)),
        scratch_shapes=[pltpu.VMEM((ck * LANES, sw), BF16)],
        compiler_params=_cparams("arbitrary", "arbitrary", "arbitrary"),
        name="s5_chunk_inputs",
    )(u, b_op)
    coef_spec = pl.BlockSpec((None, 1, sw), lambda b, k, n: (k, 0, 0))
    xprev = pl.pallas_call(
        functools.partial(_s5_scan_kernel, half=sw // 2),
        out_shape=jax.ShapeDtypeStruct((batch, n_chunks, nblk * sw // 2), F32),
        grid=(batch, nblk, n_chunks // tr),
        in_specs=[pl.BlockSpec((None, tr, sw), lambda b, k, n: (b, n, k)), coef_spec, coef_spec],
        out_specs=pl.BlockSpec((None, tr, sw // 2), lambda b, k, n: (b, n, k)),
        scratch_shapes=[pltpu.VMEM((1, sw), F32)],
        compiler_params=_cparams("parallel", "parallel", "arbitrary"),
        name="s5_chunk_scan",
    )(xin, coef_same, coef_cross)
    y = pl.pallas_call(
        functools.partial(_s5_out_kernel, n_c=n_c),
        out_shape=jax.ShapeDtypeStruct((batch, ck, n_chunks, gc), BF16),
        grid=(nblk, batch, n_chunks // tr),
        in_specs=[u_spec, group_ops(m_op),
                  pl.BlockSpec((None, tr, sw // 2), lambda k, b, n: (b, n, k)), group_ops(c_op)],
        out_specs=u_spec,
        scratch_shapes=[pltpu.VMEM((ck * LANES, ck * LANES), BF16), pltpu.VMEM((sw // 2, ck * LANES), BF16)],
        compiler_params=_cparams("arbitrary", "arbitrary", "arbitrary"),
        name="s5_chunk_outputs",
    )(u, m_op, xprev, c_op)
    tm, tn2 = _tile(seq, 512), _tile(d, 512)
    nt, nj = seq // tm, d // tn2
    return pl.pallas_call(
        functools.partial(_glu_out_kernel, dilation=ck),
        out_shape=jax.ShapeDtypeStruct((t, d), F32),
        grid=(t // tm, nj),
        in_specs=[pl.BlockSpec((None, ck, tm // ck, gc), lambda i, j: (i // nt, 0, i % nt, 0)),
                  pl.BlockSpec((gc, tn2), lambda i, j: (0, j)),
                  pl.BlockSpec((gc, tn2), lambda i, j: (0, j + nj)),
                  pl.BlockSpec((tm, tn2), lambda i, j: (i, j))],
        out_specs=pl.BlockSpec((tm, tn2), lambda i, j: (i, j)),
        scratch_shapes=[pltpu.VMEM((tm, gc), BF16)],
        compiler_params=_cparams("parallel", "arbitrary"),
        name="s5_glu_out_proj",
    )(y, w_out, w_out, x)


def _shift_norm(x_ref, xp_ref, g_ref, first):
    g = g_ref[...]
    h = _rms(x_ref[...], g)
    prev = _rms(xp_ref[7:8, :], g)
    prev = jnp.where(first, 0.0, prev)
    row = lax.broadcasted_iota(jnp.int32, h.shape, 0)
    hp = jnp.where(row == 0, prev, pltpu.roll(h, 1, 0))
    return h, hp


def _rwkv_proj_kernel(x_ref, xp_ref, g_ref, mu_ref, w_ref, o_ref, l_ref, *, tm, seq, col_tiles):
    i, c = pl.program_id(0), pl.program_id(1)

    @pl.when(c == 0)
    def _():
        h, hp = _shift_norm(x_ref, xp_ref, g_ref, (i * tm) % seq == 0)
        dlt = hp - h
        for j in range(l_ref.shape[0]):
            l_ref[j] = (h + dlt * mu_ref[j]).astype(BF16)

    o_ref[...] = _dot(l_ref[c // col_tiles], w_ref[...])


def _softplus(z):
    return jnp.maximum(z, 0.0) + jnp.log(1.0 + jnp.exp(-jnp.abs(z)))


def _rwkv_lora_kernel(x_ref, xp_ref, g_ref, mu_ref, w0_ref, w1_ref, w2_ref, a0_ref, a1_ref, a2_ref,
                      g1_ref, g2_ref, lw_ref, a_ref, gate_ref, *, tm, seq):
    h, hp = _shift_norm(x_ref, xp_ref, g_ref, (pl.program_id(0) * tm) % seq == 0)
    dlt = hp - h
    xw = (h + dlt * mu_ref[0:1, :]).astype(BF16)
    xa = (h + dlt * mu_ref[1:2, :]).astype(BF16)
    xg = (h + dlt * mu_ref[2:3, :]).astype(BF16)
    wl = w0_ref[...] + _dot(jnp.tanh(_dot(xw, w1_ref[...])).astype(BF16), w2_ref[...])
    w = -_softplus(-wl) - 0.5
    lw_ref[...] = -jnp.exp(w)
    a_ref[...] = jax.nn.sigmoid(a0_ref[...] + _dot(_dot(xa, a1_ref[...]).astype(BF16), a2_ref[...]))
    gate_ref[...] = _dot(jax.nn.sigmoid(_dot(xg, g1_ref[...])).astype(BF16), g2_ref[...])


def _rwkv_core_kernel(r_ref, k_ref, v_ref, lw_ref, a_ref, gate_ref, kk_ref, ka_ref, rk_ref, lnw_ref, lnb_ref,
                      o_ref, s_ref, lhs_ref, rhs_ref, bk_ref, v2_ref, dec_ref, y_ref):
    ck, hd = RWKV_CHUNK, RWKV_HEAD_DIM
    nb, tc, width = r_ref.shape
    nch, ck2 = tc // ck, 2 * ck
    seqs = [(bi, slice(pi * LANES, (pi + 1) * LANES)) for bi in range(nb) for pi in range(width // LANES)]

    @pl.when(pl.program_id(1) == 0)
    def _():
        s_ref[...] = jnp.zeros_like(s_ref)

    lane = lax.broadcasted_iota(jnp.int32, (1, 1, LANES), 2)
    head_a = lane < hd
    hrow = lax.broadcasted_iota(jnp.int32, (LANES, LANES), 0) // hd
    hcol = lax.broadcasted_iota(jnp.int32, (LANES, LANES), 1) // hd
    head_ones = jnp.where(hrow == hcol, 1.0, 0.0).astype(BF16)
    trow = lax.broadcasted_iota(jnp.int32, (nch, ck2, ck), 1)
    tcol = lax.broadcasted_iota(jnp.int32, (nch, ck2, ck), 2)
    sum_ops = jnp.where((tcol <= trow) | (trow >= ck), 1.0, 0.0).astype(BF16)

    def chunk_sums(x):
        hi, lo = _split2(x)
        bdot = lambda t: lax.dot_general(sum_ops, t, (((2,), (1,)), ((0,), (0,))), preferred_element_type=F32)
        both = bdot(hi) + bdot(lo)
        return both[:, :ck], both[:, ck:]

    def head_sum(x):
        hi, lo = _split2(x)
        return _dot(hi, head_ones) + _dot(lo, head_ones)

    def stack_heads(x):
        xb = x.astype(BF16)
        zero = jnp.zeros_like(xb)
        return jnp.concatenate([jnp.where(head_a, xb, zero), jnp.where(head_a, zero, xb)], axis=1)

    for si, (bi, ls) in enumerate(seqs):
        k_all, a_all = k_ref[bi, :, ls], a_ref[bi, :, ls]
        kk = k_all * kk_ref[:, ls]
        kk = kk * lax.rsqrt(jnp.maximum(head_sum(kk * kk), 1e-24))
        k2 = k_all * (1.0 + (a_all - 1.0) * ka_ref[:, ls])
        lw = lw_ref[bi, :, ls]
        by_chunk = lambda x: x.reshape(nch, ck, LANES)
        cs, tot = chunk_sums(by_chunk(lw))
        gam_inv, gam_rem = jnp.exp(-cs), jnp.exp(tot - cs)
        atm = by_chunk(-kk) * jnp.exp(cs - by_chunk(lw))
        rm = by_chunk(r_ref[bi, :, ls]) * jnp.exp(cs)
        b3, k3 = by_chunk(kk * a_all), by_chunk(k2)
        lhs_ref[si] = jnp.concatenate([stack_heads(atm), stack_heads(rm)], axis=1)
        rhs_ref[si] = jnp.concatenate([stack_heads(b3 * gam_inv), stack_heads(k3 * gam_inv)], axis=1)
        bk_ref[si] = jnp.concatenate([stack_heads(b3 * gam_rem), stack_heads(k3 * gam_rem)], axis=1)
        v2_ref[si] = stack_heads(by_chunk(v_ref[bi, :, ls]))
        dec_ref[si] = jnp.exp(tot[:, 0:1, :])

    row = lax.broadcasted_iota(jnp.int32, (ck2, ck2), 0)
    col = lax.broadcasted_iota(jnp.int32, (ck2, ck2), 1)
    incl = col <= row
    strict = col < row
    eye = jnp.where(row == col, 1.0, 0.0)
    n_seq = len(seqs)
    each = lambda f, *lists: [f(*args) for args in zip(*lists)]

    def chunk_step(ci, carry):
        lhs = [lhs_ref[si, ci] for si in range(n_seq)]
        gram = each(lambda l, si: _dot_nt(l, rhs_ref[si, ci]), lhs, range(n_seq))
        a_ab = each(lambda g: jnp.where(strict, g[:ck2, :ck2], 0.0), gram)
        a_lo = each(lambda g: jnp.concatenate([jnp.where(strict, g[:ck2, ck2:], 0.0),
                                               jnp.where(incl, g[ck2:, ck2:], 0.0)], axis=0).astype(BF16), gram)
        a_rb = each(lambda g: jnp.where(incl, g[ck2:, :ck2], 0.0).astype(BF16), gram)
        inv = each(lambda a: eye + a, a_ab)
        pw = each(lambda a: _dot(a.astype(BF16), a.astype(BF16)), a_ab)
        m = 2
        while 2 * m < ck:
            pwb = each(lambda p: p.astype(BF16), pw)
            both = each(lambda p, t: _dot(jnp.concatenate([p, t.astype(BF16)], axis=0), p), pwb, inv)
            pw = each(lambda z: z[:ck2], both)
            inv = each(lambda t, z: t + z[ck2:], inv, both)
            m *= 2
        inv = each(lambda t, p: t + _dot(t.astype(BF16), p.astype(BF16)), inv, pw)
        v2 = [v2_ref[si, ci] for si in range(n_seq)]
        av = each(_dot, a_lo, v2)
        s = [s_ref[si] for si in range(n_seq)]
        xs = each(lambda l, st: _dot_nt(l, st.astype(BF16)), lhs, s)
        u = each(lambda t, x, w: _dot(t.astype(BF16), (x[:ck2] + w[:ck2]).astype(BF16)).astype(BF16), inv, xs, av)
        y2 = each(lambda x, w, arb, ub: x[ck2:] + w[ck2:] + _dot(arb, ub), xs, av, a_rb, u)
        sl = pl.ds(pl.multiple_of(ci * ck, ck), ck)
        for si, (bi, ls) in enumerate(seqs):
            y_ref[bi, sl, ls] = y2[si][:ck] + y2[si][ck:]
            uv = jnp.concatenate([u[si], v2[si]], axis=0)
            s_ref[si] = s[si] * dec_ref[si, ci] + _dot_tn(uv, bk_ref[si, ci])
        return carry

    lax.fori_loop(0, nch, chunk_step, 0)

    for bi, ls in seqs:
        y = y_ref[bi, :, ls]
        mean = head_sum(y) * (1.0 / hd)
        yc = y - mean
        var = _dot((yc * yc).astype(BF16), head_ones) * (1.0 / hd)
        yn = yc * lax.rsqrt(var + RWKV_GN_EPS) * lnw_ref[:, ls] + lnb_ref[:, ls]
        k2 = k_ref[bi, :, ls] * (1.0 + (a_ref[bi, :, ls] - 1.0) * ka_ref[:, ls])
        bonus = head_sum(r_ref[bi, :, ls] * k2 * rk_ref[:, ls]) * v_ref[bi, :, ls]
        o_ref[bi, :, ls] = ((yn + bonus) * gate_ref[bi, :, ls]).astype(o_ref.dtype)


def _pad_to(a, axis, size):
    pad = [(0, 0)] * a.ndim
    pad[axis] = (0, size - a.shape[axis])
    return jnp.pad(a, pad)


def rwkv_layer(x, g, mu, w_rkv, w0, w1, w2, a0, a1, a2, g1, g2, k_k, k_a, r_k, ln_w, ln_b, w_o, batch, seq):
    t, d = x.shape
    row = lambda p: p.reshape(1, d).astype(F32)
    g2d = g.reshape(1, d)
    tn = _tile(d, 1024)
    prev_spec = lambda tm: pl.BlockSpec((8, d), (lambda i, *_: (jnp.maximum(i * (tm // 8) - 1, 0), 0)))
    tm = _tile(seq, 512)
    nn = d // tn
    rkv = pl.pallas_call(
        functools.partial(_rwkv_proj_kernel, tm=tm, seq=seq, col_tiles=nn),
        out_shape=jax.ShapeDtypeStruct((3, t, d), F32),
        grid=(t // tm, 3 * nn),
        in_specs=[pl.BlockSpec((tm, d), lambda i, c: (i, 0)),
                  prev_spec(tm),
                  pl.BlockSpec((1, d), lambda i, c: (0, 0)),
                  pl.BlockSpec((3, 1, d), lambda i, c: (0, 0, 0)),
                  pl.BlockSpec((None, d, tn), lambda i, c: (c // nn, 0, c % nn))],
        out_specs=pl.BlockSpec((None, tm, tn), lambda i, c: (c // nn, i, c % nn)),
        scratch_shapes=[pltpu.VMEM((3, tm, d), BF16)],
        compiler_params=_cparams("parallel", "arbitrary"),
        name="rwkv_rkv_proj",
    )(x, x, g2d, mu[:3].reshape(3, 1, d), w_rkv)

    pad_rank = lambda w_a, w_b: (_pad_to(w_a, 1, -(-w_a.shape[1] // LANES) * LANES).astype(BF16),
                                 _pad_to(w_b, 0, -(-w_b.shape[0] // LANES) * LANES).astype(BF16))
    w1p, w2p = pad_rank(w1, w2)
    a1p, a2p = pad_rank(a1, a2)
    g1p, g2p = pad_rank(g1, g2)
    full = lambda a: pl.BlockSpec(a.shape, lambda i: (0,) * a.ndim)
    tm = _tile(seq, 256)
    tok = pl.BlockSpec((tm, d), lambda i: (i, 0))
    lora_in = [x, x, g2d, mu[3:6], row(w0), w1p, w2p, row(a0), a1p, a2p, g1p, g2p]
    lw, a_gate, gate = pl.pallas_call(
        functools.partial(_rwkv_lora_kernel, tm=tm, seq=seq),
        out_shape=(jax.ShapeDtypeStruct((t, d), F32),) * 3,
        grid=(t // tm,),
        in_specs=[tok, prev_spec(tm)] + [full(a) for a in lora_in[2:]],
        out_specs=(tok, tok, tok),
        compiler_params=_cparams("parallel"),
        name="rwkv_lora",
    )(*lora_in)

    tc = _tile(seq, 512)
    wd = _tile(d, 4 * LANES)
    tokc = pl.BlockSpec((batch, tc, wd), lambda p, c: (0, c, p))
    rkvc = lambda which: pl.BlockSpec((None, batch, tc, wd), lambda p, c: (which, 0, c, p))
    par = pl.BlockSpec((1, wd), lambda p, c: (0, p))
    n_seq = batch * (wd // LANES)
    nch = tc // RWKV_CHUNK
    bsd = lambda a: a.reshape(batch, seq, d)
    rkv4 = rkv.reshape(3, batch, seq, d)
    mixed = pl.pallas_call(
        _rwkv_core_kernel,
        out_shape=jax.ShapeDtypeStruct((batch, seq, d), BF16),
        grid=(d // wd, seq // tc),
        in_specs=[rkvc(0), rkvc(1), rkvc(2), tokc, tokc, tokc, par, par, par, par, par],
        out_specs=tokc,
        scratch_shapes=[pltpu.VMEM((n_seq, LANES, LANES), F32)]
        + [pltpu.VMEM((n_seq, nch, 4 * RWKV_CHUNK, LANES), BF16)] * 3
        + [pltpu.VMEM((n_seq, nch, 2 * RWKV_CHUNK, LANES), BF16),
           pltpu.VMEM((n_seq, nch, 1, LANES), F32),
           pltpu.VMEM((batch, tc, wd), F32)],
        compiler_params=_cparams("parallel", "arbitrary"),
        name="rwkv_chunked_state",
    )(rkv4, rkv4, rkv4, bsd(lw), bsd(a_gate), bsd(gate), row(k_k), row(k_a), row(r_k), row(ln_w), row(ln_b))
    return matmul_residual(mixed.reshape(t, d), w_o, x)


def kernel(x, norm_mix, norm_mlp, norm_f, attn_w_qkv, attn_w_o, ssm_w_in, ssm_log_dt, ssm_a_re, ssm_a_im,
           ssm_b_re, ssm_b_im, ssm_c_re, ssm_c_im, ssm_d, ssm_w_out, rwkv_mu, rwkv_w_rkv, rwkv_w0, rwkv_w1,
           rwkv_w2, rwkv_a0, rwkv_a1, rwkv_a2, rwkv_g1, rwkv_g2, rwkv_k_k, rwkv_k_a, rwkv_r_k, rwkv_ln_w,
           rwkv_ln_b, rwkv_w_o, mlp_w1, mlp_w2):
    batch, seq, d = x.shape
    depth = norm_mix.shape[0]
    bf = lambda w: w.astype(BF16)
    h = x.reshape(batch * seq, d)
    ia = ib = ic = 0
    for layer in range(depth):
        kind = layer % 3
        if kind == 0:
            h = attention_layer(h, norm_mix[layer], bf(attn_w_qkv[ia]), bf(attn_w_o[ia]), batch, seq)
            ia += 1
        elif kind == 1:
            h = s5_layer(h, norm_mix[layer], bf(ssm_w_in[ib]), ssm_log_dt[ib], ssm_a_re[ib], ssm_a_im[ib],
                         ssm_b_re[ib], ssm_b_im[ib], ssm_c_re[ib], ssm_c_im[ib], ssm_d[ib],
                         bf(ssm_w_out[ib]), batch, seq)
            ib += 1
        else:
            h = rwkv_layer(h, norm_mix[layer], rwkv_mu[ic], bf(rwkv_w_rkv[ic]), rwkv_w0[ic], rwkv_w1[ic],
                           rwkv_w2[ic], rwkv_a0[ic], rwkv_a1[ic], rwkv_a2[ic], rwkv_g1[ic], rwkv_g2[ic],
                           rwkv_k_k[ic], rwkv_k_a[ic], rwkv_r_k[ic], rwkv_ln_w[ic], rwkv_ln_b[ic],
                           bf(rwkv_w_o[ic]), batch, seq)
            ic += 1
        g_final = norm_f if layer == depth - 1 else None
        h = mlp_residual(h, norm_mlp[layer], bf(mlp_w1[layer]), bf(mlp_w2[layer]), g_final)
    return h.reshape(batch, seq, d)
```

```python
import functools

import jax
import jax.numpy as jnp
from jax import lax
from jax.experimental import pallas as pl
from jax.experimental.pallas import tpu as pltpu

F32 = jnp.float32
BF16 = jnp.bfloat16
HIGHEST = lax.Precision.HIGHEST

NORM_EPS = 1e-5
LANES = 128
VMEM_LIMIT_BYTES = 56 * 2**20
MASK_VALUE = -1e30

ATTN_PATTERNS = ((128, 1), (512, 4), (2048, 16))
ATTN_BLOCK = 128
ATTN_HEAD_DIM = 128
SSM_CH = 16
SSM_CHUNK = 16
SSM_DT_MIN = 0.001
SSM_DT_MAX = 0.1
RWKV_HEAD_DIM = 64
RWKV_CHUNK = 64
RWKV_GN_EPS = RWKV_HEAD_DIM * 1e-5


def _cparams(*sem):
    return pltpu.CompilerParams(dimension_semantics=sem, vmem_limit_bytes=VMEM_LIMIT_BYTES)


def _tile(n, pref):
    t = min(n, pref)
    while n % t:
        t //= 2
    return t


def _rms(x, g):
    ms = jnp.mean(x * x, axis=-1, keepdims=True)
    return x * lax.rsqrt(ms + NORM_EPS) * g


def _dot(a, b):
    return jnp.dot(a, b, preferred_element_type=F32)


def _dot_nt(a, b):
    return lax.dot_general(a, b, (((1,), (1,)), ((), ())), preferred_element_type=F32)


def _dot_tn(a, b):
    return lax.dot_general(a, b, (((0,), (0,)), ((), ())), preferred_element_type=F32)


def _split2(x):
    hi = x.astype(BF16)
    return hi, (x - hi.astype(F32)).astype(BF16)


def _norm_matmul_strided_kernel(x_ref, g_ref, w_ref, o_ref, h_ref, *, dilation, sub):
    tm = h_ref.shape[0]
    per = sub // dilation

    @pl.when(pl.program_id(1) == 0)
    def _():
        if dilation > 1:
            new = lax.broadcasted_iota(jnp.int32, (sub, sub), 0)
            old = lax.broadcasted_iota(jnp.int32, (sub, sub), 1)
            perm = jnp.where(old == (new % per) * dilation + new // per, 1.0, 0.0).astype(BF16)
        for s in range(tm // sub):
            rows = slice(s * sub, (s + 1) * sub)
            h = _rms(x_ref[rows, :], g_ref[...]).astype(BF16)
            h_ref[rows, :] = _dot(perm, h).astype(BF16) if dilation > 1 else h

    y = _dot(h_ref[...], w_ref[...]).astype(o_ref.dtype)
    for s in range(tm // sub):
        o_ref[:, s * per:(s + 1) * per, :] = y[s * sub:(s + 1) * sub].reshape(dilation, per, y.shape[-1])


def norm_matmul_strided(x, g, w, col0, ncols, dilation, batch, seq, tm=1024, tn=1024, sub=512):
    t, d = x.shape
    tm, tn = _tile(seq, tm), _tile(ncols, tn)
    sub = min(sub, tm)
    assert col0 % tn == 0 and (sub // dilation) % 16 == 0
    nt = seq // tm
    return pl.pallas_call(
        functools.partial(_norm_matmul_strided_kernel, dilation=dilation, sub=sub),
        out_shape=jax.ShapeDtypeStruct((batch, dilation, seq // dilation, ncols), BF16),
        grid=(t // tm, ncols // tn),
        in_specs=[pl.BlockSpec((tm, d), lambda i, j: (i, 0)),
                  pl.BlockSpec((1, d), lambda i, j: (0, 0)),
                  pl.BlockSpec((d, tn), lambda i, j: (0, col0 // tn + j))],
        out_specs=pl.BlockSpec((None, dilation, tm // dilation, tn), lambda i, j: (i // nt, 0, i % nt, j)),
        scratch_shapes=[pltpu.VMEM((tm, d), BF16)],
        compiler_params=_cparams("parallel", "arbitrary"),
        name=f"norm_matmul_stride{dilation}",
    )(x, g.reshape(1, d), w)


def _matmul_res_kernel(a_ref, w_ref, r_ref, o_ref):
    o_ref[...] = r_ref[...] + _dot(a_ref[...], w_ref[...])


def matmul_residual(a, w, res, tm=512, tn=1024):
    t, k = a.shape
    n = w.shape[1]
    tm, tn = _tile(t, tm), _tile(n, tn)
    return pl.pallas_call(
        _matmul_res_kernel,
        out_shape=jax.ShapeDtypeStruct((t, n), F32),
        grid=(t // tm, n // tn),
        in_specs=[pl.BlockSpec((tm, k), lambda i, j: (i, 0)),
                  pl.BlockSpec((k, tn), lambda i, j: (0, j)),
                  pl.BlockSpec((tm, tn), lambda i, j: (i, j))],
        out_specs=pl.BlockSpec((tm, tn), lambda i, j: (i, j)),
        compiler_params=_cparams("parallel", "parallel"),
        name="matmul_residual",
    )(a, w, res)


def _mlp_kernel(x_ref, g_ref, w1_ref, w2_ref, gf_ref, o_ref, h_ref, *, final_norm):
    f = pl.program_id(1)

    @pl.when(f == 0)
    def _():
        x = x_ref[...]
        h_ref[...] = _rms(x, g_ref[...]).astype(BF16)
        o_ref[...] = x

    a = _dot(h_ref[...], w1_ref[...])
    a = jnp.square(jnp.maximum(a, 0.0)).astype(BF16)
    o_ref[...] += _dot(a, w2_ref[...])

    if final_norm:
        @pl.when(f == pl.num_programs(1) - 1)
        def _():
            o_ref[...] = _rms(o_ref[...], gf_ref[...])


def mlp_residual(x, g, w1, w2, g_final=None, tm=512, tf=1024):
    t, d = x.shape
    ff = w1.shape[1]
    tm, tf = _tile(t, tm), _tile(ff, tf)
    final_norm = g_final is not None
    gf = (g_final if final_norm else g).reshape(1, d)
    return pl.pallas_call(
        functools.partial(_mlp_kernel, final_norm=final_norm),
        out_shape=jax.ShapeDtypeStruct((t, d), F32),
        grid=(t // tm, ff // tf),
        in_specs=[pl.BlockSpec((tm, d), lambda i, f: (i, 0)),
                  pl.BlockSpec((1, d), lambda i, f: (0, 0)),
                  pl.BlockSpec((d, tf), lambda i, f: (0, f)),
                  pl.BlockSpec((tf, d), lambda i, f: (f, 0)),
                  pl.BlockSpec((1, d), lambda i, f: (0, 0))],
        out_specs=pl.BlockSpec((tm, d), lambda i, f: (i, 0)),
        scratch_shapes=[pltpu.VMEM((tm, d), BF16)],
        compiler_params=_cparams("parallel", "arbitrary"),
        name="mlp_residual",
    )(x, g.reshape(1, d), w1, w2, gf)


def _attn_kernel(slope_ref, qkv_ref, o_ref, lse_ref, kvp_ref, *, heads, scale):
    blk, e = ATTN_BLOCK, ATTN_HEAD_DIM
    he = heads * e
    j = pl.program_id(2)

    @pl.when(j == 0)
    def _():
        kvp_ref[...] = jnp.zeros_like(kvp_ref)

    q_ref, kc_ref, vc_ref = qkv_ref.at[:, 0:he], qkv_ref.at[:, he:2 * he], qkv_ref.at[:, 2 * he:3 * he]
    kp_ref, vp_ref = kvp_ref.at[:, 0:he], kvp_ref.at[:, he:2 * he]
    qi = lax.broadcasted_iota(jnp.int32, (blk, blk), 0)
    kj = lax.broadcasted_iota(jnp.int32, (blk, blk), 1)
    log2e, ln2 = 1.4426950408889634, 0.6931471805599453
    dist_c = (qi - kj).astype(F32) * log2e
    dist_p = dist_c + float(blk) * log2e
    mask_c = jnp.where(kj <= qi, 0.0, MASK_VALUE)
    mask_p = jnp.where((kj >= qi) & (j > 0), 0.0, MASK_VALUE)
    lane = lax.broadcasted_iota(jnp.int32, (blk, LANES), 1)
    lse_tile = jnp.zeros((blk, LANES), F32)
    together = next(n for n in (8, 4, 2, 1) if heads % n == 0)
    outs = []
    for h0 in range(0, heads, together):
        hs = list(range(h0, h0 + together))
        sls = [slice(h * e, (h + 1) * e) for h in hs]
        qs = [q_ref[:, sl] for sl in sls]
        sc = [_dot_nt(q, kc_ref[:, sl]) for q, sl in zip(qs, sls)]
        sp = [_dot_nt(q, kp_ref[:, sl]) for q, sl in zip(qs, sls)]
        sc = [s * (scale * log2e) + (mask_c - slope_ref[h] * dist_c) for s, h in zip(sc, hs)]
        sp = [s * (scale * log2e) + (mask_p - slope_ref[h] * dist_p) for s, h in zip(sp, hs)]
        m = [jnp.max(jnp.maximum(a, b), axis=-1, keepdims=True) for a, b in zip(sc, sp)]
        pc = [jnp.exp2(a - mm) for a, mm in zip(sc, m)]
        pp = [jnp.exp2(b - mm) for b, mm in zip(sp, m)]
        den = [jnp.sum(a + b, axis=-1, keepdims=True) for a, b in zip(pc, pp)]
        o = [_dot(a.astype(BF16), vc_ref[:, sl]) + _dot(b.astype(BF16), vp_ref[:, sl])
             for a, b, sl in zip(pc, pp, sls)]
        for h, oo, dd, mm in zip(hs, o, den, m):
            outs.append((oo / dd).astype(o_ref.dtype))
            lse_tile = jnp.where(lane == h, mm * ln2 + jnp.log(dd), lse_tile)
    for h, oo in enumerate(outs):
        o_ref[:, h * e:(h + 1) * e] = oo
    lse_ref[...] = lse_tile
    kvp_ref[...] = qkv_ref[:, he:3 * he]


def _attn_group(qkv, slopes, group, dilation, batch, seq, heads):
    e, blk = ATTN_HEAD_DIM, ATTN_BLOCK
    he = heads * e
    sub = seq // dilation
    nb = sub // blk
    out, lse = pl.pallas_call(
        functools.partial(_attn_kernel, heads=heads, scale=e ** -0.5),
        out_shape=(jax.ShapeDtypeStruct((batch, dilation, sub, he), BF16),
                   jax.ShapeDtypeStruct((batch, dilation, sub, LANES), F32)),
        grid=(batch, dilation, nb),
        in_specs=[pl.BlockSpec(memory_space=pltpu.SMEM),
                  pl.BlockSpec((None, None, blk, 3 * he), lambda b, r, j: (b, r, j, 0))],
        out_specs=(pl.BlockSpec((None, None, blk, he), lambda b, r, j: (b, r, j, 0)),
                   pl.BlockSpec((None, None, blk, LANES), lambda b, r, j: (b, r, j, 0))),
        scratch_shapes=[pltpu.VMEM((blk, 2 * he), BF16)],
        compiler_params=_cparams("arbitrary", "arbitrary", "arbitrary"),
        name=f"dilated_attn_g{group}",
    )(slopes, qkv)
    natural = lambda a: a.transpose(0, 2, 1, 3).reshape(batch * seq, a.shape[-1])
    return natural(out), natural(lse)


def _attn_out_kernel(o0_ref, o1_ref, o2_ref, l0_ref, l1_ref, l2_ref, w_ref, r_ref, out_ref, m_ref, *, heads):
    e = ATTN_HEAD_DIM

    @pl.when(pl.program_id(1) == 0)
    def _():
        l0, l1, l2 = l0_ref[...], l1_ref[...], l2_ref[...]
        mx = jnp.maximum(jnp.maximum(l0, l1), l2)
        e0, e1, e2 = jnp.exp(l0 - mx), jnp.exp(l1 - mx), jnp.exp(l2 - mx)
        inv = 1.0 / (e0 + e1 + e2)
        src = lax.broadcasted_iota(jnp.int32, (LANES, heads * e), 0)
        dst = lax.broadcasted_iota(jnp.int32, (LANES, heads * e), 1)
        spread = jnp.where(src == dst // e, 1.0, 0.0).astype(BF16)

        per_lane = lambda w: _dot(w.astype(BF16), spread)

        acc = per_lane(e0 * inv) * o0_ref[...].astype(F32)
        acc += per_lane(e1 * inv) * o1_ref[...].astype(F32)
        acc += per_lane(e2 * inv) * o2_ref[...].astype(F32)
        m_ref[...] = acc.astype(BF16)

    out_ref[...] = r_ref[...] + _dot(m_ref[...], w_ref[...])


def attention_layer(x, g, w_qkv, w_o, batch, seq):
    t, d = x.shape
    n_dil = len(ATTN_PATTERNS)
    he = w_o.shape[0]
    heads = he // ATTN_HEAD_DIM
    n_sl = n_dil * heads
    slopes = (2.0 ** (-8.0 * jnp.arange(1, n_sl + 1, dtype=F32) / n_sl)).reshape(n_dil, heads)
    outs, lses = [], []
    for grp, (window, dilation) in enumerate(ATTN_PATTERNS):
        assert window // dilation == ATTN_BLOCK and (seq // dilation) % ATTN_BLOCK == 0
        qkv = norm_matmul_strided(x, g, w_qkv, grp * 3 * he, 3 * he, dilation, batch, seq)
        o, l = _attn_group(qkv, slopes[grp] * dilation, grp, dilation, batch, seq, heads)
        outs.append(o)
        lses.append(l)
    tm, tn = _tile(t, 512), _tile(d, 1024)
    ospec = pl.BlockSpec((tm, he), lambda i, j: (i, 0))
    lspec = pl.BlockSpec((tm, LANES), lambda i, j: (i, 0))
    return pl.pallas_call(
        functools.partial(_attn_out_kernel, heads=heads),
        out_shape=jax.ShapeDtypeStruct((t, d), F32),
        grid=(t // tm, d // tn),
        in_specs=[ospec, ospec, ospec, lspec, lspec, lspec,
                  pl.BlockSpec((he, tn), lambda i, j: (0, j)),
                  pl.BlockSpec((tm, tn), lambda i, j: (i, j))],
        out_specs=pl.BlockSpec((tm, tn), lambda i, j: (i, j)),
        scratch_shapes=[pltpu.VMEM((tm, he), BF16)],
        compiler_params=_cparams("parallel", "arbitrary"),
        name="attn_merge_out_proj",
    )(*outs, *lses, w_o, x)


def _s5_chunk_operators(log_dt, a_re, a_im, b_re, b_im, c_re, c_im, d_skip, chunk):
    hp = dict(precision=HIGHEST)
    n_g, n_p = a_re.shape
    n_c = b_re.shape[-1]
    dt = jnp.exp(log_dt)[:, None]
    mag = jnp.exp(dt * a_re)
    ab_re = mag * jnp.cos(dt * a_im)
    ab_im = mag * jnp.sin(dt * a_im)
    den = a_re * a_re + a_im * a_im
    zr = ab_re - 1.0
    cr = (zr * a_re + ab_im * a_im) / den
    ci = (ab_im * a_re - zr * a_im) / den
    bb_re = cr[..., None] * b_re - ci[..., None] * b_im
    bb_im = cr[..., None] * b_im + ci[..., None] * b_re
    pr, pi = [jnp.ones_like(ab_re)], [jnp.zeros_like(ab_re)]
    for _ in range(chunk):
        pr, pi = pr + [pr[-1] * ab_re - pi[-1] * ab_im], pi + [pr[-1] * ab_im + pi[-1] * ab_re]
    pr, pi = jnp.stack(pr), jnp.stack(pi)
    ce_re = c_re[None] * pr[:, :, None, :] - c_im[None] * pi[:, :, None, :]
    ce_im = c_re[None] * pi[:, :, None, :] + c_im[None] * pr[:, :, None, :]
    kern = (jnp.einsum('tgcp,gpd->tgcd', ce_re[:chunk], bb_re, **hp)
            - jnp.einsum('tgcp,gpd->tgcd', ce_im[:chunk], bb_im, **hp))
    kern = kern.at[0].add(jax.vmap(jnp.diag)(d_skip))
    gpb = LANES // n_c
    nblk = n_g // gpb
    lag = jnp.arange(chunk)[None, :] - jnp.arange(chunk)[:, None]
    m_op = jnp.where((lag >= 0)[:, :, None, None, None], kern[jnp.maximum(lag, 0)], 0.0)
    m_op = m_op.transpose(2, 0, 4, 1, 3).reshape(n_g, chunk * n_c, chunk * n_c)
    qr, qi = pr[chunk - 1::-1][:chunk], pi[chunk - 1::-1][:chunk]
    bo_re = qr[..., None] * bb_re[None] - qi[..., None] * bb_im[None]
    bo_im = qr[..., None] * bb_im[None] + qi[..., None] * bb_re[None]
    flat_b = lambda a: a.transpose(1, 0, 3, 2).reshape(n_g, chunk * n_c, n_p)
    bo_re, bo_im = flat_b(bo_re), flat_b(bo_im)
    b_op = jnp.concatenate([bo_re, bo_im, bo_im, bo_re], axis=-1)
    flat_c = lambda a: a.transpose(1, 3, 0, 2).reshape(n_g, n_p, chunk * n_c)
    c_op = jnp.concatenate([flat_c(ce_re[1:]), -flat_c(ce_im[1:])], axis=1)
    al_re, al_im = pr[chunk], pi[chunk]
    per_blk = lambda parts: jnp.concatenate(parts, axis=-1).reshape(nblk, 1, gpb * 2 * n_p)
    coef_same = jnp.concatenate([per_blk([al_re, al_re])] * 2, axis=-1)
    coef_cross = jnp.concatenate([per_blk([-al_im, al_im]), per_blk([al_im, -al_im])], axis=-1)
    return m_op.astype(BF16), b_op.astype(BF16), c_op.astype(BF16), coef_same, coef_cross


def _chunk_rows(u_ref):
    return jnp.concatenate([u_ref[l] for l in range(u_ref.shape[0])], axis=-1)


def _first_visit_of_block():
    return (pl.program_id(1) == 0) & (pl.program_id(2) == 0)


def _s5_in_kernel(u_ref, b_ref, o_ref, dense_ref, *, n_c):
    @pl.when(_first_visit_of_block())
    def _():
        gpb, rows, cols = b_ref.shape
        half = cols // 2
        dense_ref[...] = jnp.zeros_like(dense_ref)
        for gi in range(gpb):
            for l in range(rows // n_c):
                r0 = l * LANES + gi * n_c
                piece = b_ref[gi, l * n_c:(l + 1) * n_c, :]
                dense_ref[r0:r0 + n_c, gi * half:(gi + 1) * half] = piece[:, :half]
                dense_ref[r0:r0 + n_c, (gpb + gi) * half:(gpb + gi + 1) * half] = piece[:, half:]

    o_ref[...] = _dot(_chunk_rows(u_ref), dense_ref[...])


def _s5_scan_kernel(xin_ref, cs_ref, cc_ref, o_ref, st_ref, *, half):
    @pl.when(pl.program_id(2) == 0)
    def _():
        st_ref[...] = jnp.zeros_like(st_ref)

    cs, cc = cs_ref[...], cc_ref[...]

    def step(n, st):
        o_ref[pl.ds(n, 1), :] = st[:, :half]
        st_sw = jnp.concatenate([st[:, half:], st[:, :half]], axis=-1)
        return cs * st + cc * st_sw + xin_ref[pl.ds(n, 1), :]

    st_ref[...] = lax.fori_loop(0, xin_ref.shape[0], step, st_ref[...])


def _s5_out_kernel(u_ref, m_ref, xp_ref, c_ref, o_ref, mdense_ref, cdense_ref, *, n_c):
    @pl.when(_first_visit_of_block())
    def _():
        gpb, rows, cols = m_ref.shape
        n_q = c_ref.shape[1]
        src = lax.broadcasted_iota(jnp.int32, (cols, mdense_ref.shape[1]), 0)
        dst = lax.broadcasted_iota(jnp.int32, (cols, mdense_ref.shape[1]), 1)
        for gi in range(gpb):
            spread = jnp.where(dst == (src // n_c) * LANES + gi * n_c + src % n_c, 1.0, 0.0).astype(BF16)
            wide = _dot(m_ref[gi], spread).astype(BF16)
            for l in range(rows // n_c):
                r0 = l * LANES + gi * n_c
                mdense_ref[r0:r0 + n_c, :] = wide[l * n_c:(l + 1) * n_c, :]
            cdense_ref[gi * n_q:(gi + 1) * n_q, :] = _dot(c_ref[gi], spread).astype(BF16)

    y = _dot(_chunk_rows(u_ref), mdense_ref[...]) + _dot(xp_ref[...].astype(BF16), cdense_ref[...])
    y = jax.nn.gelu(y).astype(o_ref.dtype)
    for step in range(o_ref.shape[0]):
        o_ref[step] = y[:, step * LANES:(step + 1) * LANES]


def _glu_out_kernel(y_ref, wa_ref, wb_ref, r_ref, o_ref, yn_ref, *, dilation):
    @pl.when(pl.program_id(1) == 0)
    def _():
        tm, kdim = yn_ref.shape
        per = tm // dilation
        nat = lax.broadcasted_iota(jnp.int32, (tm, tm), 0)
        src = lax.broadcasted_iota(jnp.int32, (tm, tm), 1)
        perm = jnp.where(src == (nat % dilation) * per + nat // dilation, 1.0, 0.0).astype(BF16)
        yn_ref[...] = _dot(perm, y_ref[...].reshape(tm, kdim)).astype(BF16)

    y = yn_ref[...]
    o_ref[...] = r_ref[...] + _dot(y, wa_ref[...]) * jax.nn.sigmoid(_dot(y, wb_ref[...]))


def s5_layer(x, g, w_in, log_dt, a_re, a_im, b_re, b_im, c_re, c_im, d_skip, w_out, batch, seq):
    t, d = x.shape
    n_g, n_p = a_re.shape
    n_c = SSM_CH
    ck = SSM_CHUNK
    gc = n_g * n_c
    nblk = gc // LANES
    n_chunks = seq // ck
    sw = 4 * n_p * (LANES // n_c)
    m_op, b_op, c_op, coef_same, coef_cross = _s5_chunk_operators(
        log_dt, a_re, a_im, b_re, b_im, c_re, c_im, d_skip, ck)
    u = norm_matmul_strided(x, g, w_in, 0, gc, ck, batch, seq)
    tr = _tile(n_chunks, 512)
    u_spec = pl.BlockSpec((None, ck, tr, LANES), lambda k, b, n: (b, 0, n, k))
    gpb = LANES // n_c
    group_ops = lambda a: pl.BlockSpec((gpb,) + a.shape[1:], lambda k, b, n: (k, 0, 0))
    xin = pl.pallas_call(
        functools.partial(_s5_in_kernel, n_c=n_c),
        out_shape=jax.ShapeDtypeStruct((batch, n_chunks, nblk * sw), F32),
        grid=(nblk, batch, n_chunks // tr),
        in_specs=[u_spec, group_ops(b_op)],
        out_specs=pl.BlockSpec((None, tr, sw), lambda k, b, n: (b, n, k)),
        scratch_shapes=[pltpu.VMEM((ck * LANES, sw), BF16)],
        compiler_params=_cparams("arbitrary", "arbitrary", "arbitrary"),
        name="s5_chunk_inputs",
    )(u, b_op)
    coef_spec = pl.BlockSpec((None, 1, sw), lambda b, k, n: (k, 0, 0))
    xprev = pl.pallas_call(
        functools.partial(_s5_scan_kernel, half=sw // 2),
        out_shape=jax.ShapeDtypeStruct((batch, n_chunks, nblk * sw // 2), F32),
        grid=(batch, nblk, n_chunks // tr),
        in_specs=[pl.BlockSpec((None, tr, sw), lambda b, k, n: (b, n, k)), coef_spec, coef_spec],
        out_specs=pl.BlockSpec((None, tr, sw // 2), lambda b, k, n: (b, n, k)),
        scratch_shapes=[pltpu.VMEM((1, sw), F32)],
        compiler_params=_cparams("parallel", "parallel", "arbitrary"),
        name="s5_chunk_scan",
    )(xin, coef_same, coef_cross)
    y = pl.pallas_call(
        functools.partial(_s5_out_kernel, n_c=n_c),
        out_shape=jax.ShapeDtypeStruct((batch, ck, n_chunks, gc), BF16),
        grid=(nblk, batch, n_chunks // tr),
        in_specs=[u_spec, group_ops(m_op),
                  pl.BlockSpec((None, tr, sw // 2), lambda k, b, n: (b, n, k)), group_ops(c_op)],
        out_specs=u_spec,
        scratch_shapes=[pltpu.VMEM((ck * LANES, ck * LANES), BF16), pltpu.VMEM((sw // 2, ck * LANES), BF16)],
        compiler_params=_cparams("arbitrary", "arbitrary", "arbitrary"),
        name="s5_chunk_outputs",
    )(u, m_op, xprev, c_op)
    tm, tn2 = _tile(seq, 512), _tile(d, 512)
    nt, nj = seq // tm, d // tn2
    return pl.pallas_call(
        functools.partial(_glu_out_kernel, dilation=ck),
        out_shape=jax.ShapeDtypeStruct((t, d), F32),
        grid=(t // tm, nj),
        in_specs=[pl.BlockSpec((None, ck, tm // ck, gc), lambda i, j: (i // nt, 0, i % nt, 0)),
                  pl.BlockSpec((gc, tn2), lambda i, j: (0, j)),
                  pl.BlockSpec((gc, tn2), lambda i, j: (0, j + nj)),
                  pl.BlockSpec((tm, tn2), lambda i, j: (i, j))],
        out_specs=pl.BlockSpec((tm, tn2), lambda i, j: (i, j)),
        scratch_shapes=[pltpu.VMEM((tm, gc), BF16)],
        compiler_params=_cparams("parallel", "arbitrary"),
        name="s5_glu_out_proj",
    )(y, w_out, w_out, x)


def _shift_norm(x_ref, xp_ref, g_ref, first):
    g = g_ref[...]
    h = _rms(x_ref[...], g)
    prev = _rms(xp_ref[7:8, :], g)
    prev = jnp.where(first, 0.0, prev)
    row = lax.broadcasted_iota(jnp.int32, h.shape, 0)
    hp = jnp.where(row == 0, prev, pltpu.roll(h, 1, 0))
    return h, hp


def _rwkv_proj_kernel(x_ref, xp_ref, g_ref, mu_ref, w_ref, o_ref, h_ref, d_ref, l_ref, *, tm, seq):
    i, j, n = pl.program_id(0), pl.program_id(1), pl.program_id(2)

    @pl.when((j == 0) & (n == 0))
    def _():
        h, hp = _shift_norm(x_ref, xp_ref, g_ref, (i * tm) % seq == 0)
        h_ref[...] = h
        d_ref[...] = hp - h

    @pl.when(n == 0)
    def _():
        l_ref[...] = (h_ref[...] + d_ref[...] * mu_ref[...]).astype(BF16)

    o_ref[...] = _dot(l_ref[...], w_ref[...])


def _softplus(z):
    return jnp.maximum(z, 0.0) + jnp.log(1.0 + jnp.exp(-jnp.abs(z)))


def _rwkv_lora_kernel(x_ref, xp_ref, g_ref, mu_ref, w0_ref, w1_ref, w2_ref, a0_ref, a1_ref, a2_ref,
                      g1_ref, g2_ref, lw_ref, a_ref, gate_ref, *, tm, seq):
    h, hp = _shift_norm(x_ref, xp_ref, g_ref, (pl.program_id(0) * tm) % seq == 0)
    dlt = hp - h
    xw = (h + dlt * mu_ref[0:1, :]).astype(BF16)
    xa = (h + dlt * mu_ref[1:2, :]).astype(BF16)
    xg = (h + dlt * mu_ref[2:3, :]).astype(BF16)
    wl = w0_ref[...] + _dot(jnp.tanh(_dot(xw, w1_ref[...])).astype(BF16), w2_ref[...])
    w = -_softplus(-wl) - 0.5
    lw_ref[...] = -jnp.exp(w)
    a_ref[...] = jax.nn.sigmoid(a0_ref[...] + _dot(_dot(xa, a1_ref[...]).astype(BF16), a2_ref[...]))
    gate_ref[...] = _dot(jax.nn.sigmoid(_dot(xg, g1_ref[...])).astype(BF16), g2_ref[...])


def _rwkv_core_kernel(r_ref, k_ref, v_ref, lw_ref, a_ref, gate_ref, kk_ref, ka_ref, rk_ref, lnw_ref, lnb_ref,
                      o_ref, s_ref, lhs_ref, rhs_ref, bk_ref, v2_ref, dec_ref, y_ref):
    ck, hd = RWKV_CHUNK, RWKV_HEAD_DIM
    nb, tc, width = r_ref.shape
    nch, ck2 = tc // ck, 2 * ck
    seqs = [(bi, slice(pi * LANES, (pi + 1) * LANES)) for bi in range(nb) for pi in range(width // LANES)]

    @pl.when(pl.program_id(1) == 0)
    def _():
        s_ref[...] = jnp.zeros_like(s_ref)

    lane = lax.broadcasted_iota(jnp.int32, (1, 1, LANES), 2)
    head_a = lane < hd
    hrow = lax.broadcasted_iota(jnp.int32, (LANES, LANES), 0) // hd
    hcol = lax.broadcasted_iota(jnp.int32, (LANES, LANES), 1) // hd
    head_ones = jnp.where(hrow == hcol, 1.0, 0.0).astype(BF16)
    trow = lax.broadcasted_iota(jnp.int32, (nch, ck2, ck), 1)
    tcol = lax.broadcasted_iota(jnp.int32, (nch, ck2, ck), 2)
    sum_ops = jnp.where((tcol <= trow) | (trow >= ck), 1.0, 0.0).astype(BF16)

    def chunk_sums(x):
        hi, lo = _split2(x)
        bdot = lambda t: lax.dot_general(sum_ops, t, (((2,), (1,)), ((0,), (0,))), preferred_element_type=F32)
        both = bdot(hi) + bdot(lo)
        return both[:, :ck], both[:, ck:]

    def head_sum(x):
        hi, lo = _split2(x)
        return _dot(hi, head_ones) + _dot(lo, head_ones)

    def stack_heads(x):
        xb = x.astype(BF16)
        zero = jnp.zeros_like(xb)
        return jnp.concatenate([jnp.where(head_a, xb, zero), jnp.where(head_a, zero, xb)], axis=1)

    for si, (bi, ls) in enumerate(seqs):
        k_all, a_all = k_ref[bi, :, ls], a_ref[bi, :, ls]
        kk = k_all * kk_ref[:, ls]
        kk = kk * lax.rsqrt(jnp.maximum(head_sum(kk * kk), 1e-24))
        k2 = k_all * (1.0 + (a_all - 1.0) * ka_ref[:, ls])
        lw = lw_ref[bi, :, ls]
        by_chunk = lambda x: x.reshape(nch, ck, LANES)
        cs, tot = chunk_sums(by_chunk(lw))
        gam_inv, gam_rem = jnp.exp(-cs), jnp.exp(tot - cs)
        atm = by_chunk(-kk) * jnp.exp(cs - by_chunk(lw))
        rm = by_chunk(r_ref[bi, :, ls]) * jnp.exp(cs)
        b3, k3 = by_chunk(kk * a_all), by_chunk(k2)
        lhs_ref[si] = jnp.concatenate([stack_heads(atm), stack_heads(rm)], axis=1)
        rhs_ref[si] = jnp.concatenate([stack_heads(b3 * gam_inv), stack_heads(k3 * gam_inv)], axis=1)
        bk_ref[si] = jnp.concatenate([stack_heads(b3 * gam_rem), stack_heads(k3 * gam_rem)], axis=1)
        v2_ref[si] = stack_heads(by_chunk(v_ref[bi, :, ls]))
        dec_ref[si] = jnp.exp(tot[:, 0:1, :])

    row = lax.broadcasted_iota(jnp.int32, (ck2, ck2), 0)
    col = lax.broadcasted_iota(jnp.int32, (ck2, ck2), 1)
    incl = col <= row
    strict = col < row
    eye = jnp.where(row == col, 1.0, 0.0)
    n_seq = len(seqs)
    each = lambda f, *lists: [f(*args) for args in zip(*lists)]

    def chunk_step(ci, carry):
        lhs = [lhs_ref[si, ci] for si in range(n_seq)]
        gram = each(lambda l, si: _dot_nt(l, rhs_ref[si, ci]), lhs, range(n_seq))
        a_ab = each(lambda g: jnp.where(strict, g[:ck2, :ck2], 0.0), gram)
        a_lo = each(lambda g: jnp.concatenate([jnp.where(strict, g[:ck2, ck2:], 0.0),
                                               jnp.where(incl, g[ck2:, ck2:], 0.0)], axis=0).astype(BF16), gram)
        a_rb = each(lambda g: jnp.where(incl, g[ck2:, :ck2], 0.0).astype(BF16), gram)
        inv = each(lambda a: eye + a, a_ab)
        pw = each(lambda a: _dot(a.astype(BF16), a.astype(BF16)), a_ab)
        m = 2
        while 2 * m < ck:
            pwb = each(lambda p: p.astype(BF16), pw)
            both = each(lambda p, t: _dot(jnp.concatenate([p, t.astype(BF16)], axis=0), p), pwb, inv)
            pw = each(lambda z: z[:ck2], both)
            inv = each(lambda t, z: t + z[ck2:], inv, both)
            m *= 2
        inv = each(lambda t, p: t + _dot(t.astype(BF16), p.astype(BF16)), inv, pw)
        v2 = [v2_ref[si, ci] for si in range(n_seq)]
        av = each(_dot, a_lo, v2)
        s = [s_ref[si] for si in range(n_seq)]
        xs = each(lambda l, st: _dot_nt(l, st.astype(BF16)), lhs, s)
        u = each(lambda t, x, w: _dot(t.astype(BF16), (x[:ck2] + w[:ck2]).astype(BF16)).astype(BF16), inv, xs, av)
        y2 = each(lambda x, w, arb, ub: x[ck2:] + w[ck2:] + _dot(arb, ub), xs, av, a_rb, u)
        sl = pl.ds(pl.multiple_of(ci * ck, ck), ck)
        for si, (bi, ls) in enumerate(seqs):
            y_ref[bi, sl, ls] = y2[si][:ck] + y2[si][ck:]
            uv = jnp.concatenate([u[si], v2[si]], axis=0)
            s_ref[si] = s[si] * dec_ref[si, ci] + _dot_tn(uv, bk_ref[si, ci])
        return carry

    lax.fori_loop(0, nch, chunk_step, 0)

    for bi, ls in seqs:
        y = y_ref[bi, :, ls]
        mean = head_sum(y) * (1.0 / hd)
        yc = y - mean
        var = _dot((yc * yc).astype(BF16), head_ones) * (1.0 / hd)
        yn = yc * lax.rsqrt(var + RWKV_GN_EPS) * lnw_ref[:, ls] + lnb_ref[:, ls]
        k2 = k_ref[bi, :, ls] * (1.0 + (a_ref[bi, :, ls] - 1.0) * ka_ref[:, ls])
        bonus = head_sum(r_ref[bi, :, ls] * k2 * rk_ref[:, ls]) * v_ref[bi, :, ls]
        o_ref[bi, :, ls] = ((yn + bonus) * gate_ref[bi, :, ls]).astype(o_ref.dtype)


def _pad_to(a, axis, size):
    pad = [(0, 0)] * a.ndim
    pad[axis] = (0, size - a.shape[axis])
    return jnp.pad(a, pad)


def rwkv_layer(x, g, mu, w_rkv, w0, w1, w2, a0, a1, a2, g1, g2, k_k, k_a, r_k, ln_w, ln_b, w_o, batch, seq):
    t, d = x.shape
    row = lambda p: p.reshape(1, d).astype(F32)
    g2d = g.reshape(1, d)
    tn = _tile(d, 1024)
    prev_spec = lambda tm: pl.BlockSpec((8, d), (lambda i, *_: (jnp.maximum(i * (tm // 8) - 1, 0), 0)))
    tm = _tile(seq, 512)
    rkv = pl.pallas_call(
        functools.partial(_rwkv_proj_kernel, tm=tm, seq=seq),
        out_shape=jax.ShapeDtypeStruct((3, t, d), F32),
        grid=(t // tm, 3, d // tn),
        in_specs=[pl.BlockSpec((tm, d), lambda i, j, n: (i, 0)),
                  prev_spec(tm),
                  pl.BlockSpec((1, d), lambda i, j, n: (0, 0)),
                  pl.BlockSpec((None, 1, d), lambda i, j, n: (j, 0, 0)),
                  pl.BlockSpec((None, d, tn), lambda i, j, n: (j, 0, n))],
        out_specs=pl.BlockSpec((None, tm, tn), lambda i, j, n: (j, i, n)),
        scratch_shapes=[pltpu.VMEM((tm, d), F32), pltpu.VMEM((tm, d), F32), pltpu.VMEM((tm, d), BF16)],
        compiler_params=_cparams("parallel", "arbitrary", "arbitrary"),
        name="rwkv_rkv_proj",
    )(x, x, g2d, mu[:3].reshape(3, 1, d), w_rkv)

    pad_rank = lambda w_a, w_b: (_pad_to(w_a, 1, -(-w_a.shape[1] // LANES) * LANES).astype(BF16),
                                 _pad_to(w_b, 0, -(-w_b.shape[0] // LANES) * LANES).astype(BF16))
    w1p, w2p = pad_rank(w1, w2)
    a1p, a2p = pad_rank(a1, a2)
    g1p, g2p = pad_rank(g1, g2)
    full = lambda a: pl.BlockSpec(a.shape, lambda i: (0,) * a.ndim)
    tm = _tile(seq, 256)
    tok = pl.BlockSpec((tm, d), lambda i: (i, 0))
    lora_in = [x, x, g2d, mu[3:6], row(w0), w1p, w2p, row(a0), a1p, a2p, g1p, g2p]
    lw, a_gate, gate = pl.pallas_call(
        functools.partial(_rwkv_lora_kernel, tm=tm, seq=seq),
        out_shape=(jax.ShapeDtypeStruct((t, d), F32),) * 3,
        grid=(t // tm,),
        in_specs=[tok, prev_spec(tm)] + [full(a) for a in lora_in[2:]],
        out_specs=(tok, tok, tok),
        compiler_params=_cparams("parallel"),
        name="rwkv_lora",
    )(*lora_in)

    tc = _tile(seq, 512)
    wd = _tile(d, 4 * LANES)
    tokc = pl.BlockSpec((batch, tc, wd), lambda p, c: (0, c, p))
    rkvc = lambda which: pl.BlockSpec((None, batch, tc, wd), lambda p, c: (which, 0, c, p))
    par = pl.BlockSpec((1, wd), lambda p, c: (0, p))
    n_seq = batch * (wd // LANES)
    nch = tc // RWKV_CHUNK
    bsd = lambda a: a.reshape(batch, seq, d)
    rkv4 = rkv.reshape(3, batch, seq, d)
    mixed = pl.pallas_call(
        _rwkv_core_kernel,
        out_shape=jax.ShapeDtypeStruct((batch, seq, d), BF16),
        grid=(d // wd, seq // tc),
        in_specs=[rkvc(0), rkvc(1), rkvc(2), tokc, tokc, tokc, par, par, par, par, par],
        out_specs=tokc,
        scratch_shapes=[pltpu.VMEM((n_seq, LANES, LANES), F32)]
        + [pltpu.VMEM((n_seq, nch, 4 * RWKV_CHUNK, LANES), BF16)] * 3
        + [pltpu.VMEM((n_seq, nch, 2 * RWKV_CHUNK, LANES), BF16),
           pltpu.VMEM((n_seq, nch, 1, LANES), F32),
           pltpu.VMEM((batch, tc, wd), F32)],
        compiler_params=_cparams("parallel", "arbitrary"),
        name="rwkv_chunked_state",
    )(rkv4, rkv4, rkv4, bsd(lw), bsd(a_gate), bsd(gate), row(k_k), row(k_a), row(r_k), row(ln_w), row(ln_b))
    return matmul_residual(mixed.reshape(t, d), w_o, x)


def kernel(x, norm_mix, norm_mlp, norm_f, attn_w_qkv, attn_w_o, ssm_w_in, ssm_log_dt, ssm_a_re, ssm_a_im,
           ssm_b_re, ssm_b_im, ssm_c_re, ssm_c_im, ssm_d, ssm_w_out, rwkv_mu, rwkv_w_rkv, rwkv_w0, rwkv_w1,
           rwkv_w2, rwkv_a0, rwkv_a1, rwkv_a2, rwkv_g1, rwkv_g2, rwkv_k_k, rwkv_k_a, rwkv_r_k, rwkv_ln_w,
           rwkv_ln_b, rwkv_w_o, mlp_w1, mlp_w2):
    batch, seq, d = x.shape
    depth = norm_mix.shape[0]
    bf = lambda w: w.astype(BF16)
    h = x.reshape(batch * seq, d)
    ia = ib = ic = 0
    for layer in range(depth):
        kind = layer % 3
        if kind == 0:
            h = attention_layer(h, norm_mix[layer], bf(attn_w_qkv[ia]), bf(attn_w_o[ia]), batch, seq)
            ia += 1
        elif kind == 1:
            h = s5_layer(h, norm_mix[layer], bf(ssm_w_in[ib]), ssm_log_dt[ib], ssm_a_re[ib], ssm_a_im[ib],
                         ssm_b_re[ib], ssm_b_im[ib], ssm_c_re[ib], ssm_c_im[ib], ssm_d[ib],
                         bf(ssm_w_out[ib]), batch, seq)
            ib += 1
        else:
            h = rwkv_layer(h, norm_mix[layer], rwkv_mu[ic], bf(rwkv_w_rkv[ic]), rwkv_w0[ic], rwkv_w1[ic],
                           rwkv_w2[ic], rwkv_a0[ic], rwkv_a1[ic], rwkv_a2[ic], rwkv_g1[ic], rwkv_g2[ic],
                           rwkv_k_k[ic], rwkv_k_a[ic], rwkv_r_k[ic], rwkv_ln_w[ic], rwkv_ln_b[ic],
                           bf(rwkv_w_o[ic]), batch, seq)
            ic += 1
        g_final = norm_f if layer == depth - 1 else None
        h = mlp_residual(h, norm_mlp[layer], bf(mlp_w1[layer]), bf(mlp_w2[layer]), g_final)
    return h.reshape(batch, seq, d)
```

```python
import functools

import jax
import jax.numpy as jnp
from jax import lax
from jax.experimental import pallas as pl
from jax.experimental.pallas import tpu as pltpu

F32 = jnp.float32
BF16 = jnp.bfloat16
HIGHEST = lax.Precision.HIGHEST

NORM_EPS = 1e-5
LANES = 128
VMEM_LIMIT_BYTES = 56 * 2**20
MASK_VALUE = -1e30

ATTN_PATTERNS = ((128, 1), (512, 4), (2048, 16))
ATTN_BLOCK = 128
ATTN_HEAD_DIM = 128
SSM_CH = 16
SSM_CHUNK = 16
SSM_DT_MIN = 0.001
SSM_DT_MAX = 0.1
RWKV_HEAD_DIM = 64
RWKV_CHUNK = 64
RWKV_GN_EPS = RWKV_HEAD_DIM * 1e-5


def _cparams(*sem):
    return pltpu.CompilerParams(dimension_semantics=sem, vmem_limit_bytes=VMEM_LIMIT_BYTES)


def _tile(n, pref):
    t = min(n, pref)
    while n % t:
        t //= 2
    return t


def _rms(x, g):
    ms = jnp.mean(x * x, axis=-1, keepdims=True)
    return x * lax.rsqrt(ms + NORM_EPS) * g


def _dot(a, b):
    return jnp.dot(a, b, preferred_element_type=F32)


def _dot_nt(a, b):
    return lax.dot_general(a, b, (((1,), (1,)), ((), ())), preferred_element_type=F32)


def _dot_tn(a, b):
    return lax.dot_general(a, b, (((0,), (0,)), ((), ())), preferred_element_type=F32)


def _split2(x):
    hi = x.astype(BF16)
    return hi, (x - hi.astype(F32)).astype(BF16)


def _norm_matmul_strided_kernel(x_ref, g_ref, w_ref, o_ref, h_ref, *, dilation, sub):
    tm = h_ref.shape[0]
    per = sub // dilation

    @pl.when(pl.program_id(1) == 0)
    def _():
        if dilation > 1:
            new = lax.broadcasted_iota(jnp.int32, (sub, sub), 0)
            old = lax.broadcasted_iota(jnp.int32, (sub, sub), 1)
            perm = jnp.where(old == (new % per) * dilation + new // per, 1.0, 0.0).astype(BF16)
        for s in range(tm // sub):
            rows = slice(s * sub, (s + 1) * sub)
            h = _rms(x_ref[rows, :], g_ref[...]).astype(BF16)
            h_ref[rows, :] = _dot(perm, h).astype(BF16) if dilation > 1 else h

    y = _dot(h_ref[...], w_ref[...]).astype(o_ref.dtype)
    for s in range(tm // sub):
        o_ref[:, s * per:(s + 1) * per, :] = y[s * sub:(s + 1) * sub].reshape(dilation, per, y.shape[-1])


def norm_matmul_strided(x, g, w, col0, ncols, dilation, batch, seq, tm=1024, tn=1024, sub=512):
    t, d = x.shape
    tm, tn = _tile(seq, tm), _tile(ncols, tn)
    sub = min(sub, tm)
    assert col0 % tn == 0 and (sub // dilation) % 16 == 0
    nt = seq // tm
    return pl.pallas_call(
        functools.partial(_norm_matmul_strided_kernel, dilation=dilation, sub=sub),
        out_shape=jax.ShapeDtypeStruct((batch, dilation, seq // dilation, ncols), BF16),
        grid=(t // tm, ncols // tn),
        in_specs=[pl.BlockSpec((tm, d), lambda i, j: (i, 0)),
                  pl.BlockSpec((1, d), lambda i, j: (0, 0)),
                  pl.BlockSpec((d, tn), lambda i, j: (0, col0 // tn + j))],
        out_specs=pl.BlockSpec((None, dilation, tm // dilation, tn), lambda i, j: (i // nt, 0, i % nt, j)),
        scratch_shapes=[pltpu.VMEM((tm, d), BF16)],
        compiler_params=_cparams("parallel", "arbitrary"),
        name=f"norm_matmul_stride{dilation}",
    )(x, g.reshape(1, d), w)


def _matmul_res_kernel(a_ref, w_ref, r_ref, o_ref):
    o_ref[...] = r_ref[...] + _dot(a_ref[...], w_ref[...])


def matmul_residual(a, w, res, tm=1024, tn=1024):
    t, k = a.shape
    n = w.shape[1]
    tm, tn = _tile(t, tm), _tile(n, tn)
    return pl.pallas_call(
        _matmul_res_kernel,
        out_shape=jax.ShapeDtypeStruct((t, n), F32),
        grid=(t // tm, n // tn),
        in_specs=[pl.BlockSpec((tm, k), lambda i, j: (i, 0)),
                  pl.BlockSpec((k, tn), lambda i, j: (0, j)),
                  pl.BlockSpec((tm, tn), lambda i, j: (i, j))],
        out_specs=pl.BlockSpec((tm, tn), lambda i, j: (i, j)),
        compiler_params=_cparams("parallel", "parallel"),
        name="matmul_residual",
    )(a, w, res)


def _mlp_kernel(x_ref, g_ref, w1_ref, w2_ref, gf_ref, o_ref, h_ref, *, final_norm):
    f = pl.program_id(1)

    @pl.when(f == 0)
    def _():
        x = x_ref[...]
        h_ref[...] = _rms(x, g_ref[...]).astype(BF16)
        o_ref[...] = x

    a = _dot(h_ref[...], w1_ref[...])
    a = jnp.square(jnp.maximum(a, 0.0)).astype(BF16)
    o_ref[...] += _dot(a, w2_ref[...])

    if final_norm:
        @pl.when(f == pl.num_programs(1) - 1)
        def _():
            o_ref[...] = _rms(o_ref[...], gf_ref[...])


def mlp_residual(x, g, w1, w2, g_final=None, tm=1024, tf=512):
    t, d = x.shape
    ff = w1.shape[1]
    tm, tf = _tile(t, tm), _tile(ff, tf)
    final_norm = g_final is not None
    gf = (g_final if final_norm else g).reshape(1, d)
    return pl.pallas_call(
        functools.partial(_mlp_kernel, final_norm=final_norm),
        out_shape=jax.ShapeDtypeStruct((t, d), F32),
        grid=(t // tm, ff // tf),
        in_specs=[pl.BlockSpec((tm, d), lambda i, f: (i, 0)),
                  pl.BlockSpec((1, d), lambda i, f: (0, 0)),
                  pl.BlockSpec((d, tf), lambda i, f: (0, f)),
                  pl.BlockSpec((tf, d), lambda i, f: (f, 0)),
                  pl.BlockSpec((1, d), lambda i, f: (0, 0))],
        out_specs=pl.BlockSpec((tm, d), lambda i, f: (i, 0)),
        scratch_shapes=[pltpu.VMEM((tm, d), BF16)],
        compiler_params=_cparams("parallel", "arbitrary"),
        name="mlp_residual",
    )(x, g.reshape(1, d), w1, w2, gf)


def _attn_kernel(slope_ref, qkv_ref, o_ref, lse_ref, kvp_ref, *, heads, scale):
    blk, e = ATTN_BLOCK, ATTN_HEAD_DIM
    he = heads * e
    j = pl.program_id(2)

    @pl.when(j == 0)
    def _():
        kvp_ref[...] = jnp.zeros_like(kvp_ref)

    q_ref, kc_ref, vc_ref = qkv_ref.at[:, 0:he], qkv_ref.at[:, he:2 * he], qkv_ref.at[:, 2 * he:3 * he]
    kp_ref, vp_ref = kvp_ref.at[:, 0:he], kvp_ref.at[:, he:2 * he]
    qi = lax.broadcasted_iota(jnp.int32, (blk, blk), 0)
    kj = lax.broadcasted_iota(jnp.int32, (blk, blk), 1)
    log2e, ln2 = 1.4426950408889634, 0.6931471805599453
    dist_c = (qi - kj).astype(F32) * log2e
    dist_p = dist_c + float(blk) * log2e
    mask_c = jnp.where(kj <= qi, 0.0, MASK_VALUE)
    mask_p = jnp.where((kj >= qi) & (j > 0), 0.0, MASK_VALUE)
    lane = lax.broadcasted_iota(jnp.int32, (blk, LANES), 1)
    lse_tile = jnp.zeros((blk, LANES), F32)
    together = next(n for n in (8, 4, 2, 1) if heads % n == 0)
    outs = []
    for h0 in range(0, heads, together):
        hs = list(range(h0, h0 + together))
        sls = [slice(h * e, (h + 1) * e) for h in hs]
        qs = [q_ref[:, sl] for sl in sls]
        sc = [_dot_nt(q, kc_ref[:, sl]) for q, sl in zip(qs, sls)]
        sp = [_dot_nt(q, kp_ref[:, sl]) for q, sl in zip(qs, sls)]
        sc = [s * (scale * log2e) + (mask_c - slope_ref[h] * dist_c) for s, h in zip(sc, hs)]
        sp = [s * (scale * log2e) + (mask_p - slope_ref[h] * dist_p) for s, h in zip(sp, hs)]
        m = [jnp.max(jnp.maximum(a, b), axis=-1, keepdims=True) for a, b in zip(sc, sp)]
        pc = [jnp.exp2(a - mm) for a, mm in zip(sc, m)]
        pp = [jnp.exp2(b - mm) for b, mm in zip(sp, m)]
        den = [jnp.sum(a + b, axis=-1, keepdims=True) for a, b in zip(pc, pp)]
        o = [_dot(a.astype(BF16), vc_ref[:, sl]) + _dot(b.astype(BF16), vp_ref[:, sl])
             for a, b, sl in zip(pc, pp, sls)]
        for h, oo, dd, mm in zip(hs, o, den, m):
            outs.append((oo / dd).astype(o_ref.dtype))
            lse_tile = jnp.where(lane == h, mm * ln2 + jnp.log(dd), lse_tile)
    for h, oo in enumerate(outs):
        o_ref[:, h * e:(h + 1) * e] = oo
    lse_ref[...] = lse_tile
    kvp_ref[...] = qkv_ref[:, he:3 * he]


def _attn_group(qkv, slopes, group, dilation, batch, seq, heads):
    e, blk = ATTN_HEAD_DIM, ATTN_BLOCK
    he = heads * e
    sub = seq // dilation
    nb = sub // blk
    out, lse = pl.pallas_call(
        functools.partial(_attn_kernel, heads=heads, scale=e ** -0.5),
        out_shape=(jax.ShapeDtypeStruct((batch, dilation, sub, he), BF16),
                   jax.ShapeDtypeStruct((batch, dilation, sub, LANES), F32)),
        grid=(batch, dilation, nb),
        in_specs=[pl.BlockSpec(memory_space=pltpu.SMEM),
                  pl.BlockSpec((None, None, blk, 3 * he), lambda b, r, j: (b, r, j, 0))],
        out_specs=(pl.BlockSpec((None, None, blk, he), lambda b, r, j: (b, r, j, 0)),
                   pl.BlockSpec((None, None, blk, LANES), lambda b, r, j: (b, r, j, 0))),
        scratch_shapes=[pltpu.VMEM((blk, 2 * he), BF16)],
        compiler_params=_cparams("arbitrary", "arbitrary", "arbitrary"),
        name=f"dilated_attn_g{group}",
    )(slopes, qkv)
    natural = lambda a: a.transpose(0, 2, 1, 3).reshape(batch * seq, a.shape[-1])
    return natural(out), natural(lse)


def _attn_out_kernel(o0_ref, o1_ref, o2_ref, l0_ref, l1_ref, l2_ref, w_ref, r_ref, out_ref, m_ref, *, heads):
    e = ATTN_HEAD_DIM

    @pl.when(pl.program_id(1) == 0)
    def _():
        l0, l1, l2 = l0_ref[...], l1_ref[...], l2_ref[...]
        mx = jnp.maximum(jnp.maximum(l0, l1), l2)
        e0, e1, e2 = jnp.exp(l0 - mx), jnp.exp(l1 - mx), jnp.exp(l2 - mx)
        inv = 1.0 / (e0 + e1 + e2)
        src = lax.broadcasted_iota(jnp.int32, (LANES, heads * e), 0)
        dst = lax.broadcasted_iota(jnp.int32, (LANES, heads * e), 1)
        spread = jnp.where(src == dst // e, 1.0, 0.0).astype(BF16)

        per_lane = lambda w: _dot(w.astype(BF16), spread)

        acc = per_lane(e0 * inv) * o0_ref[...].astype(F32)
        acc += per_lane(e1 * inv) * o1_ref[...].astype(F32)
        acc += per_lane(e2 * inv) * o2_ref[...].astype(F32)
        m_ref[...] = acc.astype(BF16)

    out_ref[...] = r_ref[...] + _dot(m_ref[...], w_ref[...])


def attention_layer(x, g, w_qkv, w_o, batch, seq):
    t, d = x.shape
    n_dil = len(ATTN_PATTERNS)
    he = w_o.shape[0]
    heads = he // ATTN_HEAD_DIM
    n_sl = n_dil * heads
    slopes = (2.0 ** (-8.0 * jnp.arange(1, n_sl + 1, dtype=F32) / n_sl)).reshape(n_dil, heads)
    outs, lses = [], []
    for grp, (window, dilation) in enumerate(ATTN_PATTERNS):
        assert window // dilation == ATTN_BLOCK and (seq // dilation) % ATTN_BLOCK == 0
        qkv = norm_matmul_strided(x, g, w_qkv, grp * 3 * he, 3 * he, dilation, batch, seq)
        o, l = _attn_group(qkv, slopes[grp] * dilation, grp, dilation, batch, seq, heads)
        outs.append(o)
        lses.append(l)
    tm, tn = _tile(t, 512), _tile(d, 1024)
    ospec = pl.BlockSpec((tm, he), lambda i, j: (i, 0))
    lspec = pl.BlockSpec((tm, LANES), lambda i, j: (i, 0))
    return pl.pallas_call(
        functools.partial(_attn_out_kernel, heads=heads),
        out_shape=jax.ShapeDtypeStruct((t, d), F32),
        grid=(t // tm, d // tn),
        in_specs=[ospec, ospec, ospec, lspec, lspec, lspec,
                  pl.BlockSpec((he, tn), lambda i, j: (0, j)),
                  pl.BlockSpec((tm, tn), lambda i, j: (i, j))],
        out_specs=pl.BlockSpec((tm, tn), lambda i, j: (i, j)),
        scratch_shapes=[pltpu.VMEM((tm, he), BF16)],
        compiler_params=_cparams("parallel", "arbitrary"),
        name="attn_merge_out_proj",
    )(*outs, *lses, w_o, x)


def _s5_chunk_operators(log_dt, a_re, a_im, b_re, b_im, c_re, c_im, d_skip, chunk):
    hp = dict(precision=HIGHEST)
    n_g, n_p = a_re.shape
    n_c = b_re.shape[-1]
    dt = jnp.exp(log_dt)[:, None]
    mag = jnp.exp(dt * a_re)
    ab_re = mag * jnp.cos(dt * a_im)
    ab_im = mag * jnp.sin(dt * a_im)
    den = a_re * a_re + a_im * a_im
    zr = ab_re - 1.0
    cr = (zr * a_re + ab_im * a_im) / den
    ci = (ab_im * a_re - zr * a_im) / den
    bb_re = cr[..., None] * b_re - ci[..., None] * b_im
    bb_im = cr[..., None] * b_im + ci[..., None] * b_re
    pr, pi = [jnp.ones_like(ab_re)], [jnp.zeros_like(ab_re)]
    for _ in range(chunk):
        pr, pi = pr + [pr[-1] * ab_re - pi[-1] * ab_im], pi + [pr[-1] * ab_im + pi[-1] * ab_re]
    pr, pi = jnp.stack(pr), jnp.stack(pi)
    ce_re = c_re[None] * pr[:, :, None, :] - c_im[None] * pi[:, :, None, :]
    ce_im = c_re[None] * pi[:, :, None, :] + c_im[None] * pr[:, :, None, :]
    kern = (jnp.einsum('tgcp,gpd->tgcd', ce_re[:chunk], bb_re, **hp)
            - jnp.einsum('tgcp,gpd->tgcd', ce_im[:chunk], bb_im, **hp))
    kern = kern.at[0].add(jax.vmap(jnp.diag)(d_skip))
    gpb = LANES // n_c
    nblk = n_g // gpb
    lag = jnp.arange(chunk)[None, :] - jnp.arange(chunk)[:, None]
    m_op = jnp.where((lag >= 0)[:, :, None, None, None], kern[jnp.maximum(lag, 0)], 0.0)
    m_op = m_op.transpose(2, 0, 4, 1, 3).reshape(n_g, chunk * n_c, chunk * n_c)
    qr, qi = pr[chunk - 1::-1][:chunk], pi[chunk - 1::-1][:chunk]
    bo_re = qr[..., None] * bb_re[None] - qi[..., None] * bb_im[None]
    bo_im = qr[..., None] * bb_im[None] + qi[..., None] * bb_re[None]
    flat_b = lambda a: a.transpose(1, 0, 3, 2).reshape(n_g, chunk * n_c, n_p)
    bo_re, bo_im = flat_b(bo_re), flat_b(bo_im)
    b_op = jnp.concatenate([bo_re, bo_im, bo_im, bo_re], axis=-1)
    flat_c = lambda a: a.transpose(1, 3, 0, 2).reshape(n_g, n_p, chunk * n_c)
    c_op = jnp.concatenate([flat_c(ce_re[1:]), -flat_c(ce_im[1:])], axis=1)
    al_re, al_im = pr[chunk], pi[chunk]
    per_blk = lambda parts: jnp.concatenate(parts, axis=-1).reshape(nblk, 1, gpb * 2 * n_p)
    coef_same = jnp.concatenate([per_blk([al_re, al_re])] * 2, axis=-1)
    coef_cross = jnp.concatenate([per_blk([-al_im, al_im]), per_blk([al_im, -al_im])], axis=-1)
    return m_op.astype(BF16), b_op.astype(BF16), c_op.astype(BF16), coef_same, coef_cross


def _chunk_rows(u_ref):
    return jnp.concatenate([u_ref[l] for l in range(u_ref.shape[0])], axis=-1)


def _first_visit_of_block():
    return (pl.program_id(1) == 0) & (pl.program_id(2) == 0)


def _s5_in_kernel(u_ref, b_ref, o_ref, dense_ref, *, n_c):
    @pl.when(_first_visit_of_block())
    def _():
        gpb, rows, cols = b_ref.shape
        half = cols // 2
        dense_ref[...] = jnp.zeros_like(dense_ref)
        for gi in range(gpb):
            for l in range(rows // n_c):
                r0 = l * LANES + gi * n_c
                piece = b_ref[gi, l * n_c:(l + 1) * n_c, :]
                dense_ref[r0:r0 + n_c, gi * half:(gi + 1) * half] = piece[:, :half]
                dense_ref[r0:r0 + n_c, (gpb + gi) * half:(gpb + gi + 1) * half] = piece[:, half:]

    o_ref[...] = _dot(_chunk_rows(u_ref), dense_ref[...])


def _s5_scan_kernel(xin_ref, cs_ref, cc_ref, o_ref, st_ref, *, half):
    @pl.when(pl.program_id(2) == 0)
    def _():
        st_ref[...] = jnp.zeros_like(st_ref)

    cs, cc = cs_ref[...], cc_ref[...]

    def step(n, st):
        o_ref[pl.ds(n, 1), :] = st[:, :half]
        st_sw = jnp.concatenate([st[:, half:], st[:, :half]], axis=-1)
        return cs * st + cc * st_sw + xin_ref[pl.ds(n, 1), :]

    st_ref[...] = lax.fori_loop(0, xin_ref.shape[0], step, st_ref[...])


def _s5_out_kernel(u_ref, m_ref, xp_ref, c_ref, o_ref, mdense_ref, cdense_ref, *, n_c):
    @pl.when(_first_visit_of_block())
    def _():
        gpb, rows, cols = m_ref.shape
        n_q = c_ref.shape[1]
        src = lax.broadcasted_iota(jnp.int32, (cols, mdense_ref.shape[1]), 0)
        dst = lax.broadcasted_iota(jnp.int32, (cols, mdense_ref.shape[1]), 1)
        for gi in range(gpb):
            spread = jnp.where(dst == (src // n_c) * LANES + gi * n_c + src % n_c, 1.0, 0.0).astype(BF16)
            wide = _dot(m_ref[gi], spread).astype(BF16)
            for l in range(rows // n_c):
                r0 = l * LANES + gi * n_c
                mdense_ref[r0:r0 + n_c, :] = wide[l * n_c:(l + 1) * n_c, :]
            cdense_ref[gi * n_q:(gi + 1) * n_q, :] = _dot(c_ref[gi], spread).astype(BF16)

    y = _dot(_chunk_rows(u_ref), mdense_ref[...]) + _dot(xp_ref[...].astype(BF16), cdense_ref[...])
    y = jax.nn.gelu(y).astype(o_ref.dtype)
    for step in range(o_ref.shape[0]):
        o_ref[step] = y[:, step * LANES:(step + 1) * LANES]


def _glu_out_kernel(y_ref, wa_ref, wb_ref, r_ref, o_ref, yn_ref, *, dilation):
    @pl.when(pl.program_id(1) == 0)
    def _():
        tm, kdim = yn_ref.shape
        per = tm // dilation
        nat = lax.broadcasted_iota(jnp.int32, (tm, tm), 0)
        src = lax.broadcasted_iota(jnp.int32, (tm, tm), 1)
        perm = jnp.where(src == (nat % dilation) * per + nat // dilation, 1.0, 0.0).astype(BF16)
        yn_ref[...] = _dot(perm, y_ref[...].reshape(tm, kdim)).astype(BF16)

    y = yn_ref[...]
    o_ref[...] = r_ref[...] + _dot(y, wa_ref[...]) * jax.nn.sigmoid(_dot(y, wb_ref[...]))


def s5_layer(x, g, w_in, log_dt, a_re, a_im, b_re, b_im, c_re, c_im, d_skip, w_out, batch, seq):
    t, d = x.shape
    n_g, n_p = a_re.shape
    n_c = SSM_CH
    ck = SSM_CHUNK
    gc = n_g * n_c
    nblk = gc // LANES
    n_chunks = seq // ck
    sw = 4 * n_p * (LANES // n_c)
    m_op, b_op, c_op, coef_same, coef_cross = _s5_chunk_operators(
        log_dt, a_re, a_im, b_re, b_im, c_re, c_im, d_skip, ck)
    u = norm_matmul_strided(x, g, w_in, 0, gc, ck, batch, seq)
    tr = _tile(n_chunks, 512)
    u_spec = pl.BlockSpec((None, ck, tr, LANES), lambda k, b, n: (b, 0, n, k))
    gpb = LANES // n_c
    group_ops = lambda a: pl.BlockSpec((gpb,) + a.shape[1:], lambda k, b, n: (k, 0, 0))
    xin = pl.pallas_call(
        functools.partial(_s5_in_kernel, n_c=n_c),
        out_shape=jax.ShapeDtypeStruct((batch, n_chunks, nblk * sw), F32),
        grid=(nblk, batch, n_chunks // tr),
        in_specs=[u_spec, group_ops(b_op)],
        out_specs=pl.BlockSpec((None, tr, sw), lambda k, b, n: (b, n, k)),
        scratch_shapes=[pltpu.VMEM((ck * LANES, sw), BF16)],
        compiler_params=_cparams("arbitrary", "arbitrary", "arbitrary"),
        name="s5_chunk_inputs",
    )(u, b_op)
    coef_spec = pl.BlockSpec((None, 1, sw), lambda b, k, n: (k, 0, 0))
    xprev = pl.pallas_call(
        functools.partial(_s5_scan_kernel, half=sw // 2),
        out_shape=jax.ShapeDtypeStruct((batch, n_chunks, nblk * sw // 2), F32),
        grid=(batch, nblk, n_chunks // tr),
        in_specs=[pl.BlockSpec((None, tr, sw), lambda b, k, n: (b, n, k)), coef_spec, coef_spec],
        out_specs=pl.BlockSpec((None, tr, sw // 2), lambda b, k, n: (b, n, k)),
        scratch_shapes=[pltpu.VMEM((1, sw), F32)],
        compiler_params=_cparams("parallel", "parallel", "arbitrary"),
        name="s5_chunk_scan",
    )(xin, coef_same, coef_cross)
    y = pl.pallas_call(
        functools.partial(_s5_out_kernel, n_c=n_c),
        out_shape=jax.ShapeDtypeStruct((batch, ck, n_chunks, gc), BF16),
        grid=(nblk, batch, n_chunks // tr),
        in_specs=[u_spec, group_ops(m_op),
                  pl.BlockSpec((None, tr, sw // 2), lambda k, b, n: (b, n, k)), group_ops(c_op)],
        out_specs=u_spec,
        scratch_shapes=[pltpu.VMEM((ck * LANES, ck * LANES), BF16), pltpu.VMEM((sw // 2, ck * LANES), BF16)],
        compiler_params=_cparams("arbitrary", "arbitrary", "arbitrary"),
        name="s5_chunk_outputs",
    )(u, m_op, xprev, c_op)
    tm, tn2 = _tile(seq, 512), _tile(d, 512)
    nt, nj = seq // tm, d // tn2
    return pl.pallas_call(
        functools.partial(_glu_out_kernel, dilation=ck),
        out_shape=jax.ShapeDtypeStruct((t, d), F32),
        grid=(t // tm, nj),
        in_specs=[pl.BlockSpec((None, ck, tm // ck, gc), lambda i, j: (i // nt, 0, i % nt, 0)),
                  pl.BlockSpec((gc, tn2), lambda i, j: (0, j)),
                  pl.BlockSpec((gc, tn2), lambda i, j: (0, j + nj)),
                  pl.BlockSpec((tm, tn2), lambda i, j: (i, j))],
        out_specs=pl.BlockSpec((tm, tn2), lambda i, j: (i, j)),
        scratch_shapes=[pltpu.VMEM((tm, gc), BF16)],
        compiler_params=_cparams("parallel", "arbitrary"),
        name="s5_glu_out_proj",
    )(y, w_out, w_out, x)


def _shift_norm(x_ref, xp_ref, g_ref, first):
    g = g_ref[...]
    h = _rms(x_ref[...], g)
    prev = _rms(xp_ref[7:8, :], g)
    prev = jnp.where(first, 0.0, prev)
    row = lax.broadcasted_iota(jnp.int32, h.shape, 0)
    hp = jnp.where(row == 0, prev, pltpu.roll(h, 1, 0))
    return h, hp


def _rwkv_proj_kernel(x_ref, xp_ref, g_ref, mu_ref, w_ref, o_ref, h_ref, d_ref, l_ref, *, tm, seq):
    i, j, n = pl.program_id(0), pl.program_id(1), pl.program_id(2)

    @pl.when((j == 0) & (n == 0))
    def _():
        h, hp = _shift_norm(x_ref, xp_ref, g_ref, (i * tm) % seq == 0)
        h_ref[...] = h
        d_ref[...] = hp - h

    @pl.when(n == 0)
    def _():
        l_ref[...] = (h_ref[...] + d_ref[...] * mu_ref[...]).astype(BF16)

    o_ref[...] = _dot(l_ref[...], w_ref[...])


def _softplus(z):
    return jnp.maximum(z, 0.0) + jnp.log(1.0 + jnp.exp(-jnp.abs(z)))


def _rwkv_lora_kernel(x_ref, xp_ref, g_ref, mu_ref, w0_ref, w1_ref, w2_ref, a0_ref, a1_ref, a2_ref,
                      g1_ref, g2_ref, lw_ref, a_ref, gate_ref, *, tm, seq):
    h, hp = _shift_norm(x_ref, xp_ref, g_ref, (pl.program_id(0) * tm) % seq == 0)
    dlt = hp - h
    xw = (h + dlt * mu_ref[0:1, :]).astype(BF16)
    xa = (h + dlt * mu_ref[1:2, :]).astype(BF16)
    xg = (h + dlt * mu_ref[2:3, :]).astype(BF16)
    wl = w0_ref[...] + _dot(jnp.tanh(_dot(xw, w1_ref[...])).astype(BF16), w2_ref[...])
    w = -_softplus(-wl) - 0.5
    lw_ref[...] = -jnp.exp(w)
    a_ref[...] = jax.nn.sigmoid(a0_ref[...] + _dot(_dot(xa, a1_ref[...]).astype(BF16), a2_ref[...]))
    gate_ref[...] = _dot(jax.nn.sigmoid(_dot(xg, g1_ref[...])).astype(BF16), g2_ref[...])


def _rwkv_core_kernel(r_ref, k_ref, v_ref, lw_ref, a_ref, gate_ref, kk_ref, ka_ref, rk_ref, lnw_ref, lnb_ref,
                      o_ref, s_ref, lhs_ref, rhs_ref, bk_ref, v2_ref, dec_ref, y_ref):
    ck, hd = RWKV_CHUNK, RWKV_HEAD_DIM
    nb, tc, width = r_ref.shape
    nch, ck2 = tc // ck, 2 * ck
    seqs = [(bi, slice(pi * LANES, (pi + 1) * LANES)) for bi in range(nb) for pi in range(width // LANES)]

    @pl.when(pl.program_id(1) == 0)
    def _():
        s_ref[...] = jnp.zeros_like(s_ref)

    lane = lax.broadcasted_iota(jnp.int32, (1, 1, LANES), 2)
    head_a = lane < hd
    hrow = lax.broadcasted_iota(jnp.int32, (LANES, LANES), 0) // hd
    hcol = lax.broadcasted_iota(jnp.int32, (LANES, LANES), 1) // hd
    head_ones = jnp.where(hrow == hcol, 1.0, 0.0).astype(BF16)
    trow = lax.broadcasted_iota(jnp.int32, (nch, ck2, ck), 1)
    tcol = lax.broadcasted_iota(jnp.int32, (nch, ck2, ck), 2)
    sum_ops = jnp.where((tcol <= trow) | (trow >= ck), 1.0, 0.0).astype(BF16)

    def chunk_sums(x):
        hi, lo = _split2(x)
        bdot = lambda t: lax.dot_general(sum_ops, t, (((2,), (1,)), ((0,), (0,))), preferred_element_type=F32)
        both = bdot(hi) + bdot(lo)
        return both[:, :ck], both[:, ck:]

    def head_sum(x):
        hi, lo = _split2(x)
        return _dot(hi, head_ones) + _dot(lo, head_ones)

    def stack_heads(x):
        xb = x.astype(BF16)
        zero = jnp.zeros_like(xb)
        return jnp.concatenate([jnp.where(head_a, xb, zero), jnp.where(head_a, zero, xb)], axis=1)

    for si, (bi, ls) in enumerate(seqs):
        k_all, a_all = k_ref[bi, :, ls], a_ref[bi, :, ls]
        kk = k_all * kk_ref[:, ls]
        kk = kk * lax.rsqrt(jnp.maximum(head_sum(kk * kk), 1e-24))
        k2 = k_all * (1.0 + (a_all - 1.0) * ka_ref[:, ls])
        lw = lw_ref[bi, :, ls]
        by_chunk = lambda x: x.reshape(nch, ck, LANES)
        cs, tot = chunk_sums(by_chunk(lw))
        gam_inv, gam_rem = jnp.exp(-cs), jnp.exp(tot - cs)
        atm = by_chunk(-kk) * jnp.exp(cs - by_chunk(lw))
        rm = by_chunk(r_ref[bi, :, ls]) * jnp.exp(cs)
        b3, k3 = by_chunk(kk * a_all), by_chunk(k2)
        lhs_ref[si] = jnp.concatenate([stack_heads(atm), stack_heads(rm)], axis=1)
        rhs_ref[si] = jnp.concatenate([stack_heads(b3 * gam_inv), stack_heads(k3 * gam_inv)], axis=1)
        bk_ref[si] = jnp.concatenate([stack_heads(b3 * gam_rem), stack_heads(k3 * gam_rem)], axis=1)
        v2_ref[si] = stack_heads(by_chunk(v_ref[bi, :, ls]))
        dec_ref[si] = jnp.exp(tot[:, 0:1, :])

    row = lax.broadcasted_iota(jnp.int32, (ck2, ck2), 0)
    col = lax.broadcasted_iota(jnp.int32, (ck2, ck2), 1)
    incl = col <= row
    strict = col < row
    eye = jnp.where(row == col, 1.0, 0.0)
    n_seq = len(seqs)
    each = lambda f, *lists: [f(*args) for args in zip(*lists)]

    def chunk_step(ci, carry):
        lhs = [lhs_ref[si, ci] for si in range(n_seq)]
        gram = each(lambda l, si: _dot_nt(l, rhs_ref[si, ci]), lhs, range(n_seq))
        a_ab = each(lambda g: jnp.where(strict, g[:ck2, :ck2], 0.0), gram)
        a_lo = each(lambda g: jnp.concatenate([jnp.where(strict, g[:ck2, ck2:], 0.0),
                                               jnp.where(incl, g[ck2:, ck2:], 0.0)], axis=0).astype(BF16), gram)
        a_rb = each(lambda g: jnp.where(incl, g[ck2:, :ck2], 0.0).astype(BF16), gram)
        inv = each(lambda a: eye + a, a_ab)
        pw = each(lambda a: _dot(a.astype(BF16), a.astype(BF16)), a_ab)
        m = 2
        while 2 * m < ck:
            pwb = each(lambda p: p.astype(BF16), pw)
            both = each(lambda p, t: _dot(jnp.concatenate([p, t.astype(BF16)], axis=0), p), pwb, inv)
            pw = each(lambda z: z[:ck2], both)
            inv = each(lambda t, z: t + z[ck2:], inv, both)
            m *= 2
        inv = each(lambda t, p: t + _dot(t.astype(BF16), p.astype(BF16)), inv, pw)
        v2 = [v2_ref[si, ci] for si in range(n_seq)]
        av = each(_dot, a_lo, v2)
        s = [s_ref[si] for si in range(n_seq)]
        xs = each(lambda l, st: _dot_nt(l, st.astype(BF16)), lhs, s)
        u = each(lambda t, x, w: _dot(t.astype(BF16), (x[:ck2] + w[:ck2]).astype(BF16)).astype(BF16), inv, xs, av)
        y2 = each(lambda x, w, arb, ub: x[ck2:] + w[ck2:] + _dot(arb, ub), xs, av, a_rb, u)
        sl = pl.ds(pl.multiple_of(ci * ck, ck), ck)
        for si, (bi, ls) in enumerate(seqs):
            y_ref[bi, sl, ls] = y2[si][:ck] + y2[si][ck:]
            uv = jnp.concatenate([u[si], v2[si]], axis=0)
            s_ref[si] = s[si] * dec_ref[si, ci] + _dot_tn(uv, bk_ref[si, ci])
        return carry

    lax.fori_loop(0, nch, chunk_step, 0)

    for bi, ls in seqs:
        y = y_ref[bi, :, ls]
        mean = head_sum(y) * (1.0 / hd)
        yc = y - mean
        var = _dot((yc * yc).astype(BF16), head_ones) * (1.0 / hd)
        yn = yc * lax.rsqrt(var + RWKV_GN_EPS) * lnw_ref[:, ls] + lnb_ref[:, ls]
        k2 = k_ref[bi, :, ls] * (1.0 + (a_ref[bi, :, ls] - 1.0) * ka_ref[:, ls])
        bonus = head_sum(r_ref[bi, :, ls] * k2 * rk_ref[:, ls]) * v_ref[bi, :, ls]
        o_ref[bi, :, ls] = ((yn + bonus) * gate_ref[bi, :, ls]).astype(o_ref.dtype)


def _pad_to(a, axis, size):
    pad = [(0, 0)] * a.ndim
    pad[axis] = (0, size - a.shape[axis])
    return jnp.pad(a, pad)


def rwkv_layer(x, g, mu, w_rkv, w0, w1, w2, a0, a1, a2, g1, g2, k_k, k_a, r_k, ln_w, ln_b, w_o, batch, seq):
    t, d = x.shape
    row = lambda p: p.reshape(1, d).astype(F32)
    g2d = g.reshape(1, d)
    tn = _tile(d, 1024)
    prev_spec = lambda tm: pl.BlockSpec((8, d), (lambda i, *_: (jnp.maximum(i * (tm // 8) - 1, 0), 0)))
    tm = _tile(seq, 512)
    rkv = pl.pallas_call(
        functools.partial(_rwkv_proj_kernel, tm=tm, seq=seq),
        out_shape=jax.ShapeDtypeStruct((3, t, d), F32),
        grid=(t // tm, 3, d // tn),
        in_specs=[pl.BlockSpec((tm, d), lambda i, j, n: (i, 0)),
                  prev_spec(tm),
                  pl.BlockSpec((1, d), lambda i, j, n: (0, 0)),
                  pl.BlockSpec((None, 1, d), lambda i, j, n: (j, 0, 0)),
                  pl.BlockSpec((None, d, tn), lambda i, j, n: (j, 0, n))],
        out_specs=pl.BlockSpec((None, tm, tn), lambda i, j, n: (j, i, n)),
        scratch_shapes=[pltpu.VMEM((tm, d), F32), pltpu.VMEM((tm, d), F32), pltpu.VMEM((tm, d), BF16)],
        compiler_params=_cparams("parallel", "arbitrary", "arbitrary"),
        name="rwkv_rkv_proj",
    )(x, x, g2d, mu[:3].reshape(3, 1, d), w_rkv)

    pad_rank = lambda w_a, w_b: (_pad_to(w_a, 1, -(-w_a.shape[1] // LANES) * LANES).astype(BF16),
                                 _pad_to(w_b, 0, -(-w_b.shape[0] // LANES) * LANES).astype(BF16))
    w1p, w2p = pad_rank(w1, w2)
    a1p, a2p = pad_rank(a1, a2)
    g1p, g2p = pad_rank(g1, g2)
    full = lambda a: pl.BlockSpec(a.shape, lambda i: (0,) * a.ndim)
    tm = _tile(seq, 256)
    tok = pl.BlockSpec((tm, d), lambda i: (i, 0))
    lora_in = [x, x, g2d, mu[3:6], row(w0), w1p, w2p, row(a0), a1p, a2p, g1p, g2p]
    lw, a_gate, gate = pl.pallas_call(
        functools.partial(_rwkv_lora_kernel, tm=tm, seq=seq),
        out_shape=(jax.ShapeDtypeStruct((t, d), F32),) * 3,
        grid=(t // tm,),
        in_specs=[tok, prev_spec(tm)] + [full(a) for a in lora_in[2:]],
        out_specs=(tok, tok, tok),
        compiler_params=_cparams("parallel"),
        name="rwkv_lora",
    )(*lora_in)

    tc = _tile(seq, 512)
    wd = _tile(d, 4 * LANES)
    tokc = pl.BlockSpec((batch, tc, wd), lambda p, c: (0, c, p))
    rkvc = lambda which: pl.BlockSpec((None, batch, tc, wd), lambda p, c: (which, 0, c, p))
    par = pl.BlockSpec((1, wd), lambda p, c: (0, p))
    n_seq = batch * (wd // LANES)
    nch = tc // RWKV_CHUNK
    bsd = lambda a: a.reshape(batch, seq, d)
    rkv4 = rkv.reshape(3, batch, seq, d)
    mixed = pl.pallas_call(
        _rwkv_core_kernel,
        out_shape=jax.ShapeDtypeStruct((batch, seq, d), BF16),
        grid=(d // wd, seq // tc),
        in_specs=[rkvc(0), rkvc(1), rkvc(2), tokc, tokc, tokc, par, par, par, par, par],
        out_specs=tokc,
        scratch_shapes=[pltpu.VMEM((n_seq, LANES, LANES), F32)]
        + [pltpu.VMEM((n_seq, nch, 4 * RWKV_CHUNK, LANES), BF16)] * 3
        + [pltpu.VMEM((n_seq, nch, 2 * RWKV_CHUNK, LANES), BF16),
           pltpu.VMEM((n_seq, nch, 1, LANES), F32),
           pltpu.VMEM((batch, tc, wd), F32)],
        compiler_params=_cparams("parallel", "arbitrary"),
        name="rwkv_chunked_state",
    )(rkv4, rkv4, rkv4, bsd(lw), bsd(a_gate), bsd(gate), row(k_k), row(k_a), row(r_k), row(ln_w), row(ln_b))
    return matmul_residual(mixed.reshape(t, d), w_o, x)


def kernel(x, norm_mix, norm_mlp, norm_f, attn_w_qkv, attn_w_o, ssm_w_in, ssm_log_dt, ssm_a_re, ssm_a_im,
           ssm_b_re, ssm_b_im, ssm_c_re, ssm_c_im, ssm_d, ssm_w_out, rwkv_mu, rwkv_w_rkv, rwkv_w0, rwkv_w1,
           rwkv_w2, rwkv_a0, rwkv_a1, rwkv_a2, rwkv_g1, rwkv_g2, rwkv_k_k, rwkv_k_a, rwkv_r_k, rwkv_ln_w,
           rwkv_ln_b, rwkv_w_o, mlp_w1, mlp_w2):
    batch, seq, d = x.shape
    depth = norm_mix.shape[0]
    bf = lambda w: w.astype(BF16)
    h = x.reshape(batch * seq, d)
    ia = ib = ic = 0
    for layer in range(depth):
        kind = layer % 3
        if kind == 0:
            h = attention_layer(h, norm_mix[layer], bf(attn_w_qkv[ia]), bf(attn_w_o[ia]), batch, seq)
            ia += 1
        elif kind == 1:
            h = s5_layer(h, norm_mix[layer], bf(ssm_w_in[ib]), ssm_log_dt[ib], ssm_a_re[ib], ssm_a_im[ib],
                         ssm_b_re[ib], ssm_b_im[ib], ssm_c_re[ib], ssm_c_im[ib], ssm_d[ib],
                         bf(ssm_w_out[ib]), batch, seq)
            ib += 1
        else:
            h = rwkv_layer(h, norm_mix[layer], rwkv_mu[ic], bf(rwkv_w_rkv[ic]), rwkv_w0[ic], rwkv_w1[ic],
                           rwkv_w2[ic], rwkv_a0[ic], rwkv_a1[ic], rwkv_a2[ic], rwkv_g1[ic], rwkv_g2[ic],
                           rwkv_k_k[ic], rwkv_k_a[ic], rwkv_r_k[ic], rwkv_ln_w[ic], rwkv_ln_b[ic],
                           bf(rwkv_w_o[ic]), batch, seq)
            ic += 1
        g_final = norm_f if layer == depth - 1 else None
        h = mlp_residual(h, norm_mlp[layer], bf(mlp_w1[layer]), bf(mlp_w2[layer]), g_final)
    return h.reshape(batch, seq, d)
```

```python
import functools

import jax
import jax.numpy as jnp
from jax import lax
from jax.experimental import pallas as pl
from jax.experimental.pallas import tpu as pltpu

F32 = jnp.float32
BF16 = jnp.bfloat16

NORM_EPS = 1e-5
LANES = 128
VMEM_LIMIT_BYTES = 56 * 2**20
MASK_VALUE = -1e30

ATTN_PATTERNS = ((128, 1), (512, 4), (2048, 16))
ATTN_BLOCK = 128
ATTN_HEAD_DIM = 128
SSM_CH = 16
SSM_CHUNK = 16
RWKV_HEAD_DIM = 64
RWKV_CHUNK = 64
RWKV_GN_EPS = RWKV_HEAD_DIM * 1e-5


def _cparams(*sem):
    return pltpu.CompilerParams(dimension_semantics=sem, vmem_limit_bytes=VMEM_LIMIT_BYTES)


def _tile(n, pref):
    t = min(n, pref)
    while n % t:
        t //= 2
    return t


def _rms(x, g):
    ms = jnp.mean(x * x, axis=-1, keepdims=True)
    return x * lax.rsqrt(ms + NORM_EPS) * g


def _dot(a, b):
    return jnp.dot(a, b, preferred_element_type=F32)


def _dot_nt(a, b):
    return lax.dot_general(a, b, (((1,), (1,)), ((), ())), preferred_element_type=F32)


def _dot_tn(a, b):
    return lax.dot_general(a, b, (((0,), (0,)), ((), ())), preferred_element_type=F32)


def _split2(x):
    hi = x.astype(BF16)
    return hi, (x - hi.astype(F32)).astype(BF16)


def _norm_matmul_strided_kernel(x_ref, g_ref, w_ref, o_ref, h_ref, *, dilation, sub):
    tm = h_ref.shape[0]
    per = sub // dilation

    @pl.when(pl.program_id(1) == 0)
    def _():
        if dilation > 1:
            new = lax.broadcasted_iota(jnp.int32, (sub, sub), 0)
            old = lax.broadcasted_iota(jnp.int32, (sub, sub), 1)
            perm = jnp.where(old == (new % per) * dilation + new // per, 1.0, 0.0).astype(BF16)
        for s in range(tm // sub):
            rows = slice(s * sub, (s + 1) * sub)
            h = _rms(x_ref[rows, :], g_ref[...]).astype(BF16)
            h_ref[rows, :] = _dot(perm, h).astype(BF16) if dilation > 1 else h

    y = _dot(h_ref[...], w_ref[...]).astype(o_ref.dtype)
    for s in range(tm // sub):
        o_ref[:, s * per:(s + 1) * per, :] = y[s * sub:(s + 1) * sub].reshape(dilation, per, y.shape[-1])


def norm_matmul_strided(x, g, w, col0, ncols, dilation, batch, seq, tm=1024, tn=1024, sub=512):
    t, d = x.shape
    tm, tn = _tile(seq, tm), _tile(ncols, tn)
    sub = min(sub, tm)
    assert col0 % tn == 0 and (sub // dilation) % 16 == 0
    nt = seq // tm
    return pl.pallas_call(
        functools.partial(_norm_matmul_strided_kernel, dilation=dilation, sub=sub),
        out_shape=jax.ShapeDtypeStruct((batch, dilation, seq // dilation, ncols), BF16),
        grid=(t // tm, ncols // tn),
        in_specs=[pl.BlockSpec((tm, d), lambda i, j: (i, 0)),
                  pl.BlockSpec((1, d), lambda i, j: (0, 0)),
                  pl.BlockSpec((d, tn), lambda i, j: (0, col0 // tn + j))],
        out_specs=pl.BlockSpec((None, dilation, tm // dilation, tn), lambda i, j: (i // nt, 0, i % nt, j)),
        scratch_shapes=[pltpu.VMEM((tm, d), BF16)],
        compiler_params=_cparams("parallel", "arbitrary"),
        name=f"norm_matmul_stride{dilation}",
    )(x, g.reshape(1, d), w)


def _matmul_res_kernel(a_ref, w_ref, r_ref, o_ref):
    o_ref[...] = r_ref[...] + _dot(a_ref[...], w_ref[...])


def matmul_residual(a, w, res, tm=1024, tn=1024):
    t, k = a.shape
    n = w.shape[1]
    tm, tn = _tile(t, tm), _tile(n, tn)
    return pl.pallas_call(
        _matmul_res_kernel,
        out_shape=jax.ShapeDtypeStruct((t, n), F32),
        grid=(t // tm, n // tn),
        in_specs=[pl.BlockSpec((tm, k), lambda i, j: (i, 0)),
                  pl.BlockSpec((k, tn), lambda i, j: (0, j)),
                  pl.BlockSpec((tm, tn), lambda i, j: (i, j))],
        out_specs=pl.BlockSpec((tm, tn), lambda i, j: (i, j)),
        compiler_params=_cparams("parallel", "parallel"),
        name="matmul_residual",
    )(a, w, res)


def _mlp_kernel(x_ref, g_ref, w1_ref, w2_ref, gf_ref, o_ref, h_ref, *, final_norm):
    f = pl.program_id(1)

    @pl.when(f == 0)
    def _():
        x = x_ref[...]
        h_ref[...] = _rms(x, g_ref[...]).astype(BF16)
        o_ref[...] = x

    a = _dot(h_ref[...], w1_ref[...])
    a = jnp.square(jnp.maximum(a, 0.0)).astype(BF16)
    o_ref[...] += _dot(a, w2_ref[...])

    if final_norm:
        @pl.when(f == pl.num_programs(1) - 1)
        def _():
            o_ref[...] = _rms(o_ref[...], gf_ref[...])


def mlp_residual(x, g, w1, w2, g_final=None, tm=1024, tf=512):
    t, d = x.shape
    ff = w1.shape[1]
    tm, tf = _tile(t, tm), _tile(ff, tf)
    final_norm = g_final is not None
    gf = (g_final if final_norm else g).reshape(1, d)
    return pl.pallas_call(
        functools.partial(_mlp_kernel, final_norm=final_norm),
        out_shape=jax.ShapeDtypeStruct((t, d), F32),
        grid=(t // tm, ff // tf),
        in_specs=[pl.BlockSpec((tm, d), lambda i, f: (i, 0)),
                  pl.BlockSpec((1, d), lambda i, f: (0, 0)),
                  pl.BlockSpec((d, tf), lambda i, f: (0, f)),
                  pl.BlockSpec((tf, d), lambda i, f: (f, 0)),
                  pl.BlockSpec((1, d), lambda i, f: (0, 0))],
        out_specs=pl.BlockSpec((tm, d), lambda i, f: (i, 0)),
        scratch_shapes=[pltpu.VMEM((tm, d), BF16)],
        compiler_params=_cparams("parallel", "arbitrary"),
        name="mlp_residual",
    )(x, g.reshape(1, d), w1, w2, gf)


def _attn_kernel(slope_ref, qkv_ref, o_ref, lse_ref, kvp_ref, *, heads, scale):
    blk, e = ATTN_BLOCK, ATTN_HEAD_DIM
    he = heads * e
    j = pl.program_id(2)

    @pl.when(j == 0)
    def _():
        kvp_ref[...] = jnp.zeros_like(kvp_ref)

    q_ref, kc_ref, vc_ref = qkv_ref.at[:, 0:he], qkv_ref.at[:, he:2 * he], qkv_ref.at[:, 2 * he:3 * he]
    kp_ref, vp_ref = kvp_ref.at[:, 0:he], kvp_ref.at[:, he:2 * he]
    qi = lax.broadcasted_iota(jnp.int32, (blk, blk), 0)
    kj = lax.broadcasted_iota(jnp.int32, (blk, blk), 1)
    log2e, ln2 = 1.4426950408889634, 0.6931471805599453
    dist_c = (qi - kj).astype(F32) * log2e
    dist_p = dist_c + float(blk) * log2e
    mask_c = jnp.where(kj <= qi, 0.0, MASK_VALUE)
    mask_p = jnp.where((kj >= qi) & (j > 0), 0.0, MASK_VALUE)
    lane = lax.broadcasted_iota(jnp.int32, (blk, LANES), 1)
    lse_tile = jnp.zeros((blk, LANES), F32)
    together = next(n for n in (8, 4, 2, 1) if heads % n == 0)
    outs = []
    for h0 in range(0, heads, together):
        hs = list(range(h0, h0 + together))
        sls = [slice(h * e, (h + 1) * e) for h in hs]
        qs = [q_ref[:, sl] for sl in sls]
        sc = [_dot_nt(q, kc_ref[:, sl]) for q, sl in zip(qs, sls)]
        sp = [_dot_nt(q, kp_ref[:, sl]) for q, sl in zip(qs, sls)]
        sc = [s * (scale * log2e) + (mask_c - slope_ref[h] * dist_c) for s, h in zip(sc, hs)]
        sp = [s * (scale * log2e) + (mask_p - slope_ref[h] * dist_p) for s, h in zip(sp, hs)]
        m = [jnp.max(jnp.maximum(a, b), axis=-1, keepdims=True) for a, b in zip(sc, sp)]
        pc = [jnp.exp2(a - mm) for a, mm in zip(sc, m)]
        pp = [jnp.exp2(b - mm) for b, mm in zip(sp, m)]
        den = [jnp.sum(a + b, axis=-1, keepdims=True) for a, b in zip(pc, pp)]
        o = [_dot(a.astype(BF16), vc_ref[:, sl]) + _dot(b.astype(BF16), vp_ref[:, sl])
             for a, b, sl in zip(pc, pp, sls)]
        for h, oo, dd, mm in zip(hs, o, den, m):
            outs.append((oo / dd).astype(o_ref.dtype))
            lse_tile = jnp.where(lane == h, mm * ln2 + jnp.log(dd), lse_tile)
    for h, oo in enumerate(outs):
        o_ref[:, h * e:(h + 1) * e] = oo
    lse_ref[...] = lse_tile
    kvp_ref[...] = qkv_ref[:, he:3 * he]


def _attn_group(qkv, slopes, group, dilation, batch, seq, heads):
    e, blk = ATTN_HEAD_DIM, ATTN_BLOCK
    he = heads * e
    sub = seq // dilation
    nb = sub // blk
    out, lse = pl.pallas_call(
        functools.partial(_attn_kernel, heads=heads, scale=e ** -0.5),
        out_shape=(jax.ShapeDtypeStruct((batch, dilation, sub, he), BF16),
                   jax.ShapeDtypeStruct((batch, dilation, sub, LANES), F32)),
        grid=(batch, dilation, nb),
        in_specs=[pl.BlockSpec(memory_space=pltpu.SMEM),
                  pl.BlockSpec((None, None, blk, 3 * he), lambda b, r, j: (b, r, j, 0))],
        out_specs=(pl.BlockSpec((None, None, blk, he), lambda b, r, j: (b, r, j, 0)),
                   pl.BlockSpec((None, None, blk, LANES), lambda b, r, j: (b, r, j, 0))),
        scratch_shapes=[pltpu.VMEM((blk, 2 * he), BF16)],
        compiler_params=_cparams("arbitrary", "arbitrary", "arbitrary"),
        name=f"dilated_attn_g{group}",
    )(slopes, qkv)
    natural = lambda a: a.transpose(0, 2, 1, 3).reshape(batch * seq, a.shape[-1])
    return natural(out), natural(lse)


def _attn_out_kernel(o0_ref, o1_ref, o2_ref, l0_ref, l1_ref, l2_ref, w_ref, r_ref, out_ref, m_ref, *, heads):
    e = ATTN_HEAD_DIM

    @pl.when(pl.program_id(1) == 0)
    def _():
        l0, l1, l2 = l0_ref[...], l1_ref[...], l2_ref[...]
        mx = jnp.maximum(jnp.maximum(l0, l1), l2)
        e0, e1, e2 = jnp.exp(l0 - mx), jnp.exp(l1 - mx), jnp.exp(l2 - mx)
        inv = 1.0 / (e0 + e1 + e2)
        src = lax.broadcasted_iota(jnp.int32, (LANES, heads * e), 0)
        dst = lax.broadcasted_iota(jnp.int32, (LANES, heads * e), 1)
        spread = jnp.where(src == dst // e, 1.0, 0.0).astype(BF16)

        per_lane = lambda w: _dot(w.astype(BF16), spread)

        acc = per_lane(e0 * inv) * o0_ref[...].astype(F32)
        acc += per_lane(e1 * inv) * o1_ref[...].astype(F32)
        acc += per_lane(e2 * inv) * o2_ref[...].astype(F32)
        m_ref[...] = acc.astype(BF16)

    out_ref[...] = r_ref[...] + _dot(m_ref[...], w_ref[...])


def attention_layer(x, g, w_qkv, w_o, batch, seq):
    t, d = x.shape
    n_dil = len(ATTN_PATTERNS)
    he = w_o.shape[0]
    heads = he // ATTN_HEAD_DIM
    n_sl = n_dil * heads
    slopes = (2.0 ** (-8.0 * jnp.arange(1, n_sl + 1, dtype=F32) / n_sl)).reshape(n_dil, heads)
    outs, lses = [], []
    for grp, (window, dilation) in enumerate(ATTN_PATTERNS):
        assert window // dilation == ATTN_BLOCK and (seq // dilation) % ATTN_BLOCK == 0
        qkv = norm_matmul_strided(x, g, w_qkv, grp * 3 * he, 3 * he, dilation, batch, seq)
        o, l = _attn_group(qkv, slopes[grp] * dilation, grp, dilation, batch, seq, heads)
        outs.append(o)
        lses.append(l)
    tm, tn = _tile(t, 512), _tile(d, 1024)
    ospec = pl.BlockSpec((tm, he), lambda i, j: (i, 0))
    lspec = pl.BlockSpec((tm, LANES), lambda i, j: (i, 0))
    return pl.pallas_call(
        functools.partial(_attn_out_kernel, heads=heads),
        out_shape=jax.ShapeDtypeStruct((t, d), F32),
        grid=(t // tm, d // tn),
        in_specs=[ospec, ospec, ospec, lspec, lspec, lspec,
                  pl.BlockSpec((he, tn), lambda i, j: (0, j)),
                  pl.BlockSpec((tm, tn), lambda i, j: (i, j))],
        out_specs=pl.BlockSpec((tm, tn), lambda i, j: (i, j)),
        scratch_shapes=[pltpu.VMEM((tm, he), BF16)],
        compiler_params=_cparams("parallel", "arbitrary"),
        name="attn_merge_out_proj",
    )(*outs, *lses, w_o, x)


def _s5_chunk_operators(log_dt, a_re, a_im, b_re, b_im, c_re, c_im, d_skip, chunk):
    n_g, n_p = a_re.shape
    n_c = b_re.shape[-1]
    a_re, a_im = a_re.T, a_im.T
    b_re, b_im = b_re.transpose(1, 2, 0), b_im.transpose(1, 2, 0)
    c_re, c_im = c_re.transpose(1, 2, 0), c_im.transpose(1, 2, 0)
    dt = jnp.exp(log_dt)[None, :]
    mag = jnp.exp(dt * a_re)
    ab_re = mag * jnp.cos(dt * a_im)
    ab_im = mag * jnp.sin(dt * a_im)
    den = a_re * a_re + a_im * a_im
    zr = ab_re - 1.0
    cr = (zr * a_re + ab_im * a_im) / den
    ci = (ab_im * a_re - zr * a_im) / den
    bb_re = cr[:, None] * b_re - ci[:, None] * b_im
    bb_im = cr[:, None] * b_im + ci[:, None] * b_re
    pr, pi = [jnp.ones_like(ab_re)], [jnp.zeros_like(ab_re)]
    for _ in range(chunk):
        pr, pi = pr + [pr[-1] * ab_re - pi[-1] * ab_im], pi + [pr[-1] * ab_im + pi[-1] * ab_re]
    pr, pi = jnp.stack(pr), jnp.stack(pi)
    ce_re = c_re[None] * pr[:, None] - c_im[None] * pi[:, None]
    ce_im = c_re[None] * pi[:, None] + c_im[None] * pr[:, None]
    kern = jnp.sum(ce_re[:chunk, :, :, None] * bb_re[None, None] - ce_im[:chunk, :, :, None] * bb_im[None, None],
                   axis=2)
    kern = kern.at[0].add(jnp.eye(n_c, dtype=F32)[:, :, None] * d_skip.T[:, None, :])
    group_first = lambda a, *shape: a.astype(BF16).reshape(-1, n_g).T.reshape((n_g,) + shape)
    toep = jnp.stack([jnp.pad(kern[:chunk - l], ((l, 0), (0, 0), (0, 0), (0, 0))) for l in range(chunk)])
    m_op = group_first(toep.transpose(0, 3, 1, 2, 4), chunk * n_c, chunk * n_c)
    qr, qi = pr[chunk - 1::-1][:chunk, :, None], pi[chunk - 1::-1][:chunk, :, None]
    bo_re = (qr * bb_re[None] - qi * bb_im[None]).transpose(0, 2, 1, 3)
    bo_im = (qr * bb_im[None] + qi * bb_re[None]).transpose(0, 2, 1, 3)
    b_op = group_first(jnp.concatenate([bo_re, bo_im, bo_im, bo_re], axis=2), chunk * n_c, 4 * n_p)
    by_state = lambda a: a.transpose(2, 0, 1, 3)
    c_op = group_first(jnp.concatenate([by_state(ce_re[1:]), -by_state(ce_im[1:])], axis=0), 2 * n_p, chunk * n_c)
    gpb = LANES // n_c
    nblk = n_g // gpb
    al_re, al_im = pr[chunk].T, pi[chunk].T
    per_blk = lambda parts: jnp.concatenate(parts, axis=-1).reshape(nblk, 1, gpb * 2 * n_p)
    coef_same = jnp.concatenate([per_blk([al_re, al_re])] * 2, axis=-1)
    coef_cross = jnp.concatenate([per_blk([-al_im, al_im]), per_blk([al_im, -al_im])], axis=-1)
    return m_op, b_op, c_op, coef_same, coef_cross


def _chunk_rows(u_ref):
    return jnp.concatenate([u_ref[l] for l in range(u_ref.shape[0])], axis=-1)


def _first_visit_of_block():
    return (pl.program_id(1) == 0) & (pl.program_id(2) == 0)


def _s5_in_kernel(u_ref, b_ref, o_ref, dense_ref, *, n_c):
    @pl.when(_first_visit_of_block())
    def _():
        gpb, rows, cols = b_ref.shape
        half = cols // 2
        dense_ref[...] = jnp.zeros_like(dense_ref)
        for gi in range(gpb):
            for l in range(rows // n_c):
                r0 = l * LANES + gi * n_c
                piece = b_ref[gi, l * n_c:(l + 1) * n_c, :]
                dense_ref[r0:r0 + n_c, gi * half:(gi + 1) * half] = piece[:, :half]
                dense_ref[r0:r0 + n_c, (gpb + gi) * half:(gpb + gi + 1) * half] = piece[:, half:]

    o_ref[...] = _dot(_chunk_rows(u_ref), dense_ref[...])


def _s5_scan_kernel(xin_ref, cs_ref, cc_ref, o_ref, st_ref, *, half):
    @pl.when(pl.program_id(2) == 0)
    def _():
        st_ref[...] = jnp.zeros_like(st_ref)

    cs, cc = cs_ref[...], cc_ref[...]

    def step(n, st):
        o_ref[pl.ds(n, 1), :] = st[:, :half]
        st_sw = jnp.concatenate([st[:, half:], st[:, :half]], axis=-1)
        return cs * st + cc * st_sw + xin_ref[pl.ds(n, 1), :]

    st_ref[...] = lax.fori_loop(0, xin_ref.shape[0], step, st_ref[...])


def _s5_out_kernel(u_ref, m_ref, xp_ref, c_ref, o_ref, mdense_ref, cdense_ref, *, n_c):
    @pl.when(_first_visit_of_block())
    def _():
        gpb, rows, cols = m_ref.shape
        n_q = c_ref.shape[1]
        src = lax.broadcasted_iota(jnp.int32, (cols, mdense_ref.shape[1]), 0)
        dst = lax.broadcasted_iota(jnp.int32, (cols, mdense_ref.shape[1]), 1)
        for gi in range(gpb):
            spread = jnp.where(dst == (src // n_c) * LANES + gi * n_c + src % n_c, 1.0, 0.0).astype(BF16)
            wide = _dot(m_ref[gi], spread).astype(BF16)
            for l in range(rows // n_c):
                r0 = l * LANES + gi * n_c
                mdense_ref[r0:r0 + n_c, :] = wide[l * n_c:(l + 1) * n_c, :]
            cdense_ref[gi * n_q:(gi + 1) * n_q, :] = _dot(c_ref[gi], spread).astype(BF16)

    y = _dot(_chunk_rows(u_ref), mdense_ref[...]) + _dot(xp_ref[...].astype(BF16), cdense_ref[...])
    y = jax.nn.gelu(y).astype(o_ref.dtype)
    for step in range(o_ref.shape[0]):
        o_ref[step] = y[:, step * LANES:(step + 1) * LANES]


def _glu_out_kernel(y_ref, wa_ref, wb_ref, r_ref, o_ref, yn_ref, *, dilation):
    @pl.when(pl.program_id(1) == 0)
    def _():
        tm, kdim = yn_ref.shape
        per = tm // dilation
        nat = lax.broadcasted_iota(jnp.int32, (tm, tm), 0)
        src = lax.broadcasted_iota(jnp.int32, (tm, tm), 1)
        perm = jnp.where(src == (nat % dilation) * per + nat // dilation, 1.0, 0.0).astype(BF16)
        yn_ref[...] = _dot(perm, y_ref[...].reshape(tm, kdim)).astype(BF16)

    y = yn_ref[...]
    o_ref[...] = r_ref[...] + _dot(y, wa_ref[...]) * jax.nn.sigmoid(_dot(y, wb_ref[...]))


def s5_layer(x, g, w_in, log_dt, a_re, a_im, b_re, b_im, c_re, c_im, d_skip, w_out, batch, seq):
    t, d = x.shape
    n_g, n_p = a_re.shape
    n_c = SSM_CH
    ck = SSM_CHUNK
    gc = n_g * n_c
    nblk = gc // LANES
    n_chunks = seq // ck
    sw = 4 * n_p * (LANES // n_c)
    m_op, b_op, c_op, coef_same, coef_cross = _s5_chunk_operators(
        log_dt, a_re, a_im, b_re, b_im, c_re, c_im, d_skip, ck)
    u = norm_matmul_strided(x, g, w_in, 0, gc, ck, batch, seq)
    tr = _tile(n_chunks, 512)
    u_spec = pl.BlockSpec((None, ck, tr, LANES), lambda k, b, n: (b, 0, n, k))
    gpb = LANES // n_c
    group_ops = lambda a: pl.BlockSpec((gpb,) + a.shape[1:], lambda k, b, n: (k, 0, 0))
    xin = pl.pallas_call(
        functools.partial(_s5_in_kernel, n_c=n_c),
        out_shape=jax.ShapeDtypeStruct((batch, n_chunks, nblk * sw), F32),
        grid=(nblk, batch, n_chunks // tr),
        in_specs=[u_spec, group_ops(b_op)],
        out_specs=pl.BlockSpec((None, tr, sw), lambda k, b, n: (b, n, k)),
        scratch_shapes=[pltpu.VMEM((ck * LANES, sw), BF16)],
        compiler_params=_cparams("arbitrary", "arbitrary", "arbitrary"),
        name="s5_chunk_inputs",
    )(u, b_op)
    coef_spec = pl.BlockSpec((None, 1, sw), lambda b, k, n: (k, 0, 0))
    xprev = pl.pallas_call(
        functools.partial(_s5_scan_kernel, half=sw // 2),
        out_shape=jax.ShapeDtypeStruct((batch, n_chunks, nblk * sw // 2), F32),
        grid=(batch, nblk, n_chunks // tr),
        in_specs=[pl.BlockSpec((None, tr, sw), lambda b, k, n: (b, n, k)), coef_spec, coef_spec],
        out_specs=pl.BlockSpec((None, tr, sw // 2), lambda b, k, n: (b, n, k)),
        scratch_shapes=[pltpu.VMEM((1, sw), F32)],
        compiler_params=_cparams("parallel", "parallel", "arbitrary"),
        name="s5_chunk_scan",
    )(xin, coef_same, coef_cross)
    y = pl.pallas_call(
        functools.partial(_s5_out_kernel, n_c=n_c),
        out_shape=jax.ShapeDtypeStruct((batch, ck, n_chunks, gc), BF16),
        grid=(nblk, batch, n_chunks // tr),
        in_specs=[u_spec, group_ops(m_op),
                  pl.BlockSpec((None, tr, sw // 2), lambda k, b, n: (b, n, k)), group_ops(c_op)],
        out_specs=u_spec,
        scratch_shapes=[pltpu.VMEM((ck * LANES, ck * LANES), BF16), pltpu.VMEM((sw // 2, ck * LANES), BF16)],
        compiler_params=_cparams("arbitrary", "arbitrary", "arbitrary"),
        name="s5_chunk_outputs",
    )(u, m_op, xprev, c_op)
    tm, tn2 = _tile(seq, 512), _tile(d, 512)
    nt, nj = seq // tm, d // tn2
    return pl.pallas_call(
        functools.partial(_glu_out_kernel, dilation=ck),
        out_shape=jax.ShapeDtypeStruct((t, d), F32),
        grid=(t // tm, nj),
        in_specs=[pl.BlockSpec((None, ck, tm // ck, gc), lambda i, j: (i // nt, 0, i % nt, 0)),
                  pl.BlockSpec((gc, tn2), lambda i, j: (0, j)),
                  pl.BlockSpec((gc, tn2), lambda i, j: (0, j + nj)),
                  pl.BlockSpec((tm, tn2), lambda i, j: (i, j))],
        out_specs=pl.BlockSpec((tm, tn2), lambda i, j: (i, j)),
        scratch_shapes=[pltpu.VMEM((tm, gc), BF16)],
        compiler_params=_cparams("parallel", "arbitrary"),
        name="s5_glu_out_proj",
    )(y, w_out, w_out, x)


def _shift_norm(x_ref, xp_ref, g_ref, first):
    g = g_ref[...]
    h = _rms(x_ref[...], g)
    prev = _rms(xp_ref[7:8, :], g)
    prev = jnp.where(first, 0.0, prev)
    row = lax.broadcasted_iota(jnp.int32, h.shape, 0)
    hp = jnp.where(row == 0, prev, pltpu.roll(h, 1, 0))
    return h, hp


def _rwkv_proj_kernel(x_ref, xp_ref, g_ref, mu_ref, w_ref, o_ref, h_ref, d_ref, l_ref, *, tm, seq):
    i, j, n = pl.program_id(0), pl.program_id(1), pl.program_id(2)

    @pl.when((j == 0) & (n == 0))
    def _():
        h, hp = _shift_norm(x_ref, xp_ref, g_ref, (i * tm) % seq == 0)
        h_ref[...] = h
        d_ref[...] = hp - h

    @pl.when(n == 0)
    def _():
        l_ref[...] = (h_ref[...] + d_ref[...] * mu_ref[...]).astype(BF16)

    o_ref[...] = _dot(l_ref[...], w_ref[...])


def _softplus(z):
    return jnp.maximum(z, 0.0) + jnp.log(1.0 + jnp.exp(-jnp.abs(z)))


def _rwkv_lora_kernel(x_ref, xp_ref, g_ref, mu_ref, w0_ref, w1_ref, w2_ref, a0_ref, a1_ref, a2_ref,
                      g1_ref, g2_ref, lw_ref, a_ref, gate_ref, *, tm, seq):
    h, hp = _shift_norm(x_ref, xp_ref, g_ref, (pl.program_id(0) * tm) % seq == 0)
    dlt = hp - h
    xw = (h + dlt * mu_ref[0:1, :]).astype(BF16)
    xa = (h + dlt * mu_ref[1:2, :]).astype(BF16)
    xg = (h + dlt * mu_ref[2:3, :]).astype(BF16)
    wl = w0_ref[...] + _dot(jnp.tanh(_dot(xw, w1_ref[...])).astype(BF16), w2_ref[...])
    w = -_softplus(-wl) - 0.5
    lw_ref[...] = -jnp.exp(w)
    a_ref[...] = jax.nn.sigmoid(a0_ref[...] + _dot(_dot(xa, a1_ref[...]).astype(BF16), a2_ref[...]))
    gate_ref[...] = _dot(jax.nn.sigmoid(_dot(xg, g1_ref[...])).astype(BF16), g2_ref[...])


def _rwkv_core_kernel(r_ref, k_ref, v_ref, lw_ref, a_ref, gate_ref, kk_ref, ka_ref, rk_ref, lnw_ref, lnb_ref,
                      o_ref, s_ref, lhs_ref, rhs_ref, bk_ref, v2_ref, dec_ref, y_ref):
    ck, hd = RWKV_CHUNK, RWKV_HEAD_DIM
    nb, tc, width = r_ref.shape
    nch, ck2 = tc // ck, 2 * ck
    seqs = [(bi, slice(pi * LANES, (pi + 1) * LANES)) for bi in range(nb) for pi in range(width // LANES)]

    @pl.when(pl.program_id(1) == 0)
    def _():
        s_ref[...] = jnp.zeros_like(s_ref)

    lane = lax.broadcasted_iota(jnp.int32, (1, 1, LANES), 2)
    head_a = lane < hd
    hrow = lax.broadcasted_iota(jnp.int32, (LANES, LANES), 0) // hd
    hcol = lax.broadcasted_iota(jnp.int32, (LANES, LANES), 1) // hd
    head_ones = jnp.where(hrow == hcol, 1.0, 0.0).astype(BF16)
    trow = lax.broadcasted_iota(jnp.int32, (nch, ck2, ck), 1)
    tcol = lax.broadcasted_iota(jnp.int32, (nch, ck2, ck), 2)
    sum_ops = jnp.where((tcol <= trow) | (trow >= ck), 1.0, 0.0).astype(BF16)

    def chunk_sums(x):
        hi, lo = _split2(x)
        bdot = lambda t: lax.dot_general(sum_ops, t, (((2,), (1,)), ((0,), (0,))), preferred_element_type=F32)
        both = bdot(hi) + bdot(lo)
        return both[:, :ck], both[:, ck:]

    def head_sum(x):
        hi, lo = _split2(x)
        return _dot(hi, head_ones) + _dot(lo, head_ones)

    def stack_heads(x):
        xb = x.astype(BF16)
        zero = jnp.zeros_like(xb)
        return jnp.concatenate([jnp.where(head_a, xb, zero), jnp.where(head_a, zero, xb)], axis=1)

    for si, (bi, ls) in enumerate(seqs):
        k_all, a_all = k_ref[bi, :, ls], a_ref[bi, :, ls]
        kk = k_all * kk_ref[:, ls]
        kk = kk * lax.rsqrt(jnp.maximum(head_sum(kk * kk), 1e-24))
        k2 = k_all * (1.0 + (a_all - 1.0) * ka_ref[:, ls])
        lw = lw_ref[bi, :, ls]
        by_chunk = lambda x: x.reshape(nch, ck, LANES)
        cs, tot = chunk_sums(by_chunk(lw))
        gam_inv, gam_rem = jnp.exp(-cs), jnp.exp(tot - cs)
        atm = by_chunk(-kk) * jnp.exp(cs - by_chunk(lw))
        rm = by_chunk(r_ref[bi, :, ls]) * jnp.exp(cs)
        b3, k3 = by_chunk(kk * a_all), by_chunk(k2)
        lhs_ref[si] = jnp.concatenate([stack_heads(atm), stack_heads(rm)], axis=1)
        rhs_ref[si] = jnp.concatenate([stack_heads(b3 * gam_inv), stack_heads(k3 * gam_inv)], axis=1)
        bk_ref[si] = jnp.concatenate([stack_heads(b3 * gam_rem), stack_heads(k3 * gam_rem)], axis=1)
        v2_ref[si] = stack_heads(by_chunk(v_ref[bi, :, ls]))
        dec_ref[si] = jnp.exp(tot[:, 0:1, :])

    row = lax.broadcasted_iota(jnp.int32, (ck2, ck2), 0)
    col = lax.broadcasted_iota(jnp.int32, (ck2, ck2), 1)
    incl = col <= row
    strict = col < row
    eye = jnp.where(row == col, 1.0, 0.0)
    n_seq = len(seqs)
    each = lambda f, *lists: [f(*args) for args in zip(*lists)]

    def chunk_step(ci, carry):
        lhs = [lhs_ref[si, ci] for si in range(n_seq)]
        gram = each(lambda l, si: _dot_nt(l, rhs_ref[si, ci]), lhs, range(n_seq))
        a_ab = each(lambda g: jnp.where(strict, g[:ck2, :ck2], 0.0), gram)
        a_lo = each(lambda g: jnp.concatenate([jnp.where(strict, g[:ck2, ck2:], 0.0),
                                               jnp.where(incl, g[ck2:, ck2:], 0.0)], axis=0).astype(BF16), gram)
        a_rb = each(lambda g: jnp.where(incl, g[ck2:, :ck2], 0.0).astype(BF16), gram)
        inv = each(lambda a: eye + a, a_ab)
        pw = each(lambda a: _dot(a.astype(BF16), a.astype(BF16)), a_ab)
        m = 2
        while 2 * m < ck:
            pwb = each(lambda p: p.astype(BF16), pw)
            both = each(lambda p, t: _dot(jnp.concatenate([p, t.astype(BF16)], axis=0), p), pwb, inv)
            pw = each(lambda z: z[:ck2], both)
            inv = each(lambda t, z: t + z[ck2:], inv, both)
            m *= 2
        inv = each(lambda t, p: t + _dot(t.astype(BF16), p.astype(BF16)), inv, pw)
        v2 = [v2_ref[si, ci] for si in range(n_seq)]
        av = each(_dot, a_lo, v2)
        s = [s_ref[si] for si in range(n_seq)]
        xs = each(lambda l, st: _dot_nt(l, st.astype(BF16)), lhs, s)
        u = each(lambda t, x, w: _dot(t.astype(BF16), (x[:ck2] + w[:ck2]).astype(BF16)).astype(BF16), inv, xs, av)
        y2 = each(lambda x, w, arb, ub: x[ck2:] + w[ck2:] + _dot(arb, ub), xs, av, a_rb, u)
        sl = pl.ds(pl.multiple_of(ci * ck, ck), ck)
        for si, (bi, ls) in enumerate(seqs):
            y_ref[bi, sl, ls] = y2[si][:ck] + y2[si][ck:]
            uv = jnp.concatenate([u[si], v2[si]], axis=0)
            s_ref[si] = s[si] * dec_ref[si, ci] + _dot_tn(uv, bk_ref[si, ci])
        return carry

    lax.fori_loop(0, nch, chunk_step, 0)

    for bi, ls in seqs:
        y = y_ref[bi, :, ls]
        mean = head_sum(y) * (1.0 / hd)
        yc = y - mean
        var = _dot((yc * yc).astype(BF16), head_ones) * (1.0 / hd)
        yn = yc * lax.rsqrt(var + RWKV_GN_EPS) * lnw_ref[:, ls] + lnb_ref[:, ls]
        k2 = k_ref[bi, :, ls] * (1.0 + (a_ref[bi, :, ls] - 1.0) * ka_ref[:, ls])
        bonus = head_sum(r_ref[bi, :, ls] * k2 * rk_ref[:, ls]) * v_ref[bi, :, ls]
        o_ref[bi, :, ls] = ((yn + bonus) * gate_ref[bi, :, ls]).astype(o_ref.dtype)


def _pad_to(a, axis, size):
    pad = [(0, 0)] * a.ndim
    pad[axis] = (0, size - a.shape[axis])
    return jnp.pad(a, pad)


def rwkv_layer(x, g, mu, w_rkv, w0, w1, w2, a0, a1, a2, g1, g2, k_k, k_a, r_k, ln_w, ln_b, w_o, batch, seq):
    t, d = x.shape
    row = lambda p: p.reshape(1, d).astype(F32)
    g2d = g.reshape(1, d)
    tn = _tile(d, 1024)
    prev_spec = lambda tm: pl.BlockSpec((8, d), (lambda i, *_: (jnp.maximum(i * (tm // 8) - 1, 0), 0)))
    tm = _tile(seq, 512)
    rkv = pl.pallas_call(
        functools.partial(_rwkv_proj_kernel, tm=tm, seq=seq),
        out_shape=jax.ShapeDtypeStruct((3, t, d), F32),
        grid=(t // tm, 3, d // tn),
        in_specs=[pl.BlockSpec((tm, d), lambda i, j, n: (i, 0)),
                  prev_spec(tm),
                  pl.BlockSpec((1, d), lambda i, j, n: (0, 0)),
                  pl.BlockSpec((None, 1, d), lambda i, j, n: (j, 0, 0)),
                  pl.BlockSpec((None, d, tn), lambda i, j, n: (j, 0, n))],
        out_specs=pl.BlockSpec((None, tm, tn), lambda i, j, n: (j, i, n)),
        scratch_shapes=[pltpu.VMEM((tm, d), F32), pltpu.VMEM((tm, d), F32), pltpu.VMEM((tm, d), BF16)],
        compiler_params=_cparams("parallel", "arbitrary", "arbitrary"),
        name="rwkv_rkv_proj",
    )(x, x, g2d, mu[:3].reshape(3, 1, d), w_rkv)

    pad_rank = lambda w_a, w_b: (_pad_to(w_a, 1, -(-w_a.shape[1] // LANES) * LANES).astype(BF16),
                                 _pad_to(w_b, 0, -(-w_b.shape[0] // LANES) * LANES).astype(BF16))
    w1p, w2p = pad_rank(w1, w2)
    a1p, a2p = pad_rank(a1, a2)
    g1p, g2p = pad_rank(g1, g2)
    full = lambda a: pl.BlockSpec(a.shape, lambda i: (0,) * a.ndim)
    tm = _tile(seq, 256)
    tok = pl.BlockSpec((tm, d), lambda i: (i, 0))
    lora_in = [x, x, g2d, mu[3:6], row(w0), w1p, w2p, row(a0), a1p, a2p, g1p, g2p]
    lw, a_gate, gate = pl.pallas_call(
        functools.partial(_rwkv_lora_kernel, tm=tm, seq=seq),
        out_shape=(jax.ShapeDtypeStruct((t, d), F32),) * 3,
        grid=(t // tm,),
        in_specs=[tok, prev_spec(tm)] + [full(a) for a in lora_in[2:]],
        out_specs=(tok, tok, tok),
        compiler_params=_cparams("parallel"),
        name="rwkv_lora",
    )(*lora_in)

    tc = _tile(seq, 512)
    wd = _tile(d, 4 * LANES)
    tokc = pl.BlockSpec((batch, tc, wd), lambda p, c: (0, c, p))
    rkvc = lambda which: pl.BlockSpec((None, batch, tc, wd), lambda p, c: (which, 0, c, p))
    par = pl.BlockSpec((1, wd), lambda p, c: (0, p))
    n_seq = batch * (wd // LANES)
    nch = tc // RWKV_CHUNK
    bsd = lambda a: a.reshape(batch, seq, d)
    rkv4 = rkv.reshape(3, batch, seq, d)
    mixed = pl.pallas_call(
        _rwkv_core_kernel,
        out_shape=jax.ShapeDtypeStruct((batch, seq, d), BF16),
        grid=(d // wd, seq // tc),
        in_specs=[rkvc(0), rkvc(1), rkvc(2), tokc, tokc, tokc, par, par, par, par, par],
        out_specs=tokc,
        scratch_shapes=[pltpu.VMEM((n_seq, LANES, LANES), F32)]
        + [pltpu.VMEM((n_seq, nch, 4 * RWKV_CHUNK, LANES), BF16)] * 3
        + [pltpu.VMEM((n_seq, nch, 2 * RWKV_CHUNK, LANES), BF16),
           pltpu.VMEM((n_seq, nch, 1, LANES), F32),
           pltpu.VMEM((batch, tc, wd), F32)],
        compiler_params=_cparams("parallel", "arbitrary"),
        name="rwkv_chunked_state",
    )(rkv4, rkv4, rkv4, bsd(lw), bsd(a_gate), bsd(gate), row(k_k), row(k_a), row(r_k), row(ln_w), row(ln_b))
    return matmul_residual(mixed.reshape(t, d), w_o, x)


def kernel(x, norm_mix, norm_mlp, norm_f, attn_w_qkv, attn_w_o, ssm_w_in, ssm_log_dt, ssm_a_re, ssm_a_im,
           ssm_b_re, ssm_b_im, ssm_c_re, ssm_c_im, ssm_d, ssm_w_out, rwkv_mu, rwkv_w_rkv, rwkv_w0, rwkv_w1,
           rwkv_w2, rwkv_a0, rwkv_a1, rwkv_a2, rwkv_g1, rwkv_g2, rwkv_k_k, rwkv_k_a, rwkv_r_k, rwkv_ln_w,
           rwkv_ln_b, rwkv_w_o, mlp_w1, mlp_w2):
    batch, seq, d = x.shape
    depth = norm_mix.shape[0]
    bf = lambda w: w.astype(BF16)
    h = x.reshape(batch * seq, d)
    ia = ib = ic = 0
    for layer in range(depth):
        kind = layer % 3
        if kind == 0:
            h = attention_layer(h, norm_mix[layer], bf(attn_w_qkv[ia]), bf(attn_w_o[ia]), batch, seq)
            ia += 1
        elif kind == 1:
            h = s5_layer(h, norm_mix[layer], bf(ssm_w_in[ib]), ssm_log_dt[ib], ssm_a_re[ib], ssm_a_im[ib],
                         ssm_b_re[ib], ssm_b_im[ib], ssm_c_re[ib], ssm_c_im[ib], ssm_d[ib],
                         bf(ssm_w_out[ib]), batch, seq)
            ib += 1
        else:
            h = rwkv_layer(h, norm_mix[layer], rwkv_mu[ic], bf(rwkv_w_rkv[ic]), rwkv_w0[ic], rwkv_w1[ic],
                           rwkv_w2[ic], rwkv_a0[ic], rwkv_a1[ic], rwkv_a2[ic], rwkv_g1[ic], rwkv_g2[ic],
                           rwkv_k_k[ic], rwkv_k_a[ic], rwkv_r_k[ic], rwkv_ln_w[ic], rwkv_ln_b[ic],
                           bf(rwkv_w_o[ic]), batch, seq)
            ic += 1
        g_final = norm_f if layer == depth - 1 else None
        h = mlp_residual(h, norm_mlp[layer], bf(mlp_w1[layer]), bf(mlp_w2[layer]), g_final)
    return h.reshape(batch, seq, d)
```

```python
import functools

import jax
import jax.numpy as jnp
from jax import lax
from jax.experimental import pallas as pl
from jax.experimental.pallas import tpu as pltpu

F32 = jnp.float32
BF16 = jnp.bfloat16

NORM_EPS = 1e-5
LANES = 128
VMEM_LIMIT_BYTES = 56 * 2**20
MASK_VALUE = -1e30

ATTN_PATTERNS = ((128, 1), (512, 4), (2048, 16))
ATTN_BLOCK = 128
ATTN_HEAD_DIM = 128
SSM_CH = 16
SSM_CHUNK = 16
RWKV_HEAD_DIM = 64
RWKV_CHUNK = 64
RWKV_GN_EPS = RWKV_HEAD_DIM * 1e-5


def _cparams(*sem):
    return pltpu.CompilerParams(dimension_semantics=sem, vmem_limit_bytes=VMEM_LIMIT_BYTES)


def _tile(n, pref):
    t = min(n, pref)
    while n % t:
        t //= 2
    return t


def _rms(x, g):
    ms = jnp.mean(x * x, axis=-1, keepdims=True)
    return x * lax.rsqrt(ms + NORM_EPS) * g


def _dot(a, b):
    return jnp.dot(a, b, preferred_element_type=F32)


def _dot_nt(a, b):
    return lax.dot_general(a, b, (((1,), (1,)), ((), ())), preferred_element_type=F32)


def _dot_tn(a, b):
    return lax.dot_general(a, b, (((0,), (0,)), ((), ())), preferred_element_type=F32)


def _split2(x):
    hi = x.astype(BF16)
    return hi, (x - hi.astype(F32)).astype(BF16)


def _norm_matmul_strided_kernel(x_ref, g_ref, w_ref, o_ref, h_ref, *, dilation, sub):
    tm = h_ref.shape[0]
    per = sub // dilation

    @pl.when(pl.program_id(1) == 0)
    def _():
        if dilation > 1:
            new = lax.broadcasted_iota(jnp.int32, (sub, sub), 0)
            old = lax.broadcasted_iota(jnp.int32, (sub, sub), 1)
            perm = jnp.where(old == (new % per) * dilation + new // per, 1.0, 0.0).astype(BF16)
        for s in range(tm // sub):
            rows = slice(s * sub, (s + 1) * sub)
            h = _rms(x_ref[rows, :], g_ref[...]).astype(BF16)
            h_ref[rows, :] = _dot(perm, h).astype(BF16) if dilation > 1 else h

    y = _dot(h_ref[...], w_ref[...]).astype(o_ref.dtype)
    for s in range(tm // sub):
        o_ref[:, s * per:(s + 1) * per, :] = y[s * sub:(s + 1) * sub].reshape(dilation, per, y.shape[-1])


def norm_matmul_strided(x, g, w, col0, ncols, dilation, batch, seq, tm=1024, tn=1024, sub=512):
    t, d = x.shape
    tm, tn = _tile(seq, tm), _tile(ncols, tn)
    sub = min(sub, tm)
    assert col0 % tn == 0 and (sub // dilation) % 16 == 0
    nt = seq // tm
    return pl.pallas_call(
        functools.partial(_norm_matmul_strided_kernel, dilation=dilation, sub=sub),
        out_shape=jax.ShapeDtypeStruct((batch, dilation, seq // dilation, ncols), BF16),
        grid=(t // tm, ncols // tn),
        in_specs=[pl.BlockSpec((tm, d), lambda i, j: (i, 0)),
                  pl.BlockSpec((1, d), lambda i, j: (0, 0)),
                  pl.BlockSpec((d, tn), lambda i, j: (0, col0 // tn + j))],
        out_specs=pl.BlockSpec((None, dilation, tm // dilation, tn), lambda i, j: (i // nt, 0, i % nt, j)),
        scratch_shapes=[pltpu.VMEM((tm, d), BF16)],
        compiler_params=_cparams("parallel", "arbitrary"),
        name=f"norm_matmul_stride{dilation}",
    )(x, g.reshape(1, d), w)


def _matmul_res_kernel(a_ref, w_ref, r_ref, o_ref):
    o_ref[...] = r_ref[...] + _dot(a_ref[...], w_ref[...])


def matmul_residual(a, w, res, tm=1024, tn=1024):
    t, k = a.shape
    n = w.shape[1]
    tm, tn = _tile(t, tm), _tile(n, tn)
    return pl.pallas_call(
        _matmul_res_kernel,
        out_shape=jax.ShapeDtypeStruct((t, n), F32),
        grid=(t // tm, n // tn),
        in_specs=[pl.BlockSpec((tm, k), lambda i, j: (i, 0)),
                  pl.BlockSpec((k, tn), lambda i, j: (0, j)),
                  pl.BlockSpec((tm, tn), lambda i, j: (i, j))],
        out_specs=pl.BlockSpec((tm, tn), lambda i, j: (i, j)),
        compiler_params=_cparams("parallel", "parallel"),
        name="matmul_residual",
    )(a, w, res)


def _mlp_kernel(x_ref, g_ref, w1_ref, w2_ref, gf_ref, o_ref, h_ref, *, final_norm):
    f = pl.program_id(1)

    @pl.when(f == 0)
    def _():
        x = x_ref[...]
        h_ref[...] = _rms(x, g_ref[...]).astype(BF16)
        o_ref[...] = x

    a = _dot(h_ref[...], w1_ref[...])
    a = jnp.square(jnp.maximum(a, 0.0)).astype(BF16)
    o_ref[...] += _dot(a, w2_ref[...])

    if final_norm:
        @pl.when(f == pl.num_programs(1) - 1)
        def _():
            o_ref[...] = _rms(o_ref[...], gf_ref[...])


def mlp_residual(x, g, w1, w2, g_final=None, tm=1024, tf=512):
    t, d = x.shape
    ff = w1.shape[1]
    tm, tf = _tile(t, tm), _tile(ff, tf)
    final_norm = g_final is not None
    gf = (g_final if final_norm else g).reshape(1, d)
    return pl.pallas_call(
        functools.partial(_mlp_kernel, final_norm=final_norm),
        out_shape=jax.ShapeDtypeStruct((t, d), F32),
        grid=(t // tm, ff // tf),
        in_specs=[pl.BlockSpec((tm, d), lambda i, f: (i, 0)),
                  pl.BlockSpec((1, d), lambda i, f: (0, 0)),
                  pl.BlockSpec((d, tf), lambda i, f: (0, f)),
                  pl.BlockSpec((tf, d), lambda i, f: (f, 0)),
                  pl.BlockSpec((1, d), lambda i, f: (0, 0))],
        out_specs=pl.BlockSpec((tm, d), lambda i, f: (i, 0)),
        scratch_shapes=[pltpu.VMEM((tm, d), BF16)],
        compiler_params=_cparams("parallel", "arbitrary"),
        name="mlp_residual",
    )(x, g.reshape(1, d), w1, w2, gf)


def _attn_kernel(slope_ref, qkv_ref, o_ref, lse_ref, kvp_ref, *, heads, scale):
    blk, e = ATTN_BLOCK, ATTN_HEAD_DIM
    he = heads * e
    j = pl.program_id(2)

    @pl.when(j == 0)
    def _():
        kvp_ref[...] = jnp.zeros_like(kvp_ref)

    q_ref, kc_ref, vc_ref = qkv_ref.at[:, 0:he], qkv_ref.at[:, he:2 * he], qkv_ref.at[:, 2 * he:3 * he]
    kp_ref, vp_ref = kvp_ref.at[:, 0:he], kvp_ref.at[:, he:2 * he]
    qi = lax.broadcasted_iota(jnp.int32, (blk, blk), 0)
    kj = lax.broadcasted_iota(jnp.int32, (blk, blk), 1)
    log2e, ln2 = 1.4426950408889634, 0.6931471805599453
    dist_c = (qi - kj).astype(F32) * log2e
    dist_p = dist_c + float(blk) * log2e
    mask_c = jnp.where(kj <= qi, 0.0, MASK_VALUE)
    mask_p = jnp.where((kj >= qi) & (j > 0), 0.0, MASK_VALUE)
    lane = lax.broadcasted_iota(jnp.int32, (blk, LANES), 1)
    lse_tile = jnp.zeros((blk, LANES), F32)
    together = next(n for n in (8, 4, 2, 1) if heads % n == 0)
    outs = []
    for h0 in range(0, heads, together):
        hs = list(range(h0, h0 + together))
        sls = [slice(h * e, (h + 1) * e) for h in hs]
        qs = [q_ref[:, sl] for sl in sls]
        sc = [_dot_nt(q, kc_ref[:, sl]) for q, sl in zip(qs, sls)]
        sp = [_dot_nt(q, kp_ref[:, sl]) for q, sl in zip(qs, sls)]
        sc = [s * (scale * log2e) + (mask_c - slope_ref[h] * dist_c) for s, h in zip(sc, hs)]
        sp = [s * (scale * log2e) + (mask_p - slope_ref[h] * dist_p) for s, h in zip(sp, hs)]
        m = [jnp.max(jnp.maximum(a, b), axis=-1, keepdims=True) for a, b in zip(sc, sp)]
        pc = [jnp.exp2(a - mm) for a, mm in zip(sc, m)]
        pp = [jnp.exp2(b - mm) for b, mm in zip(sp, m)]
        den = [jnp.sum(a + b, axis=-1, keepdims=True) for a, b in zip(pc, pp)]
        o = [_dot(a.astype(BF16), vc_ref[:, sl]) + _dot(b.astype(BF16), vp_ref[:, sl])
             for a, b, sl in zip(pc, pp, sls)]
        for h, oo, dd, mm in zip(hs, o, den, m):
            outs.append((oo / dd).astype(o_ref.dtype))
            lse_tile = jnp.where(lane == h, mm * ln2 + jnp.log(dd), lse_tile)
    for h, oo in enumerate(outs):
        o_ref[:, h * e:(h + 1) * e] = oo
    lse_ref[...] = lse_tile
    kvp_ref[...] = qkv_ref[:, he:3 * he]


def _attn_group(qkv, slopes, group, dilation, batch, seq, heads):
    e, blk = ATTN_HEAD_DIM, ATTN_BLOCK
    he = heads * e
    sub = seq // dilation
    nb = sub // blk
    out, lse = pl.pallas_call(
        functools.partial(_attn_kernel, heads=heads, scale=e ** -0.5),
        out_shape=(jax.ShapeDtypeStruct((batch, dilation, sub, he), BF16),
                   jax.ShapeDtypeStruct((batch, dilation, sub, LANES), F32)),
        grid=(batch, dilation, nb),
        in_specs=[pl.BlockSpec(memory_space=pltpu.SMEM),
                  pl.BlockSpec((None, None, blk, 3 * he), lambda b, r, j: (b, r, j, 0))],
        out_specs=(pl.BlockSpec((None, None, blk, he), lambda b, r, j: (b, r, j, 0)),
                   pl.BlockSpec((None, None, blk, LANES), lambda b, r, j: (b, r, j, 0))),
        scratch_shapes=[pltpu.VMEM((blk, 2 * he), BF16)],
        compiler_params=_cparams("arbitrary", "arbitrary", "arbitrary"),
        name=f"dilated_attn_g{group}",
    )(slopes, qkv)
    natural = lambda a: a.transpose(0, 2, 1, 3).reshape(batch * seq, a.shape[-1])
    return natural(out), natural(lse)


def _attn_out_kernel(o0_ref, o1_ref, o2_ref, l0_ref, l1_ref, l2_ref, w_ref, r_ref, out_ref, m_ref, *, heads):
    e = ATTN_HEAD_DIM

    @pl.when(pl.program_id(1) == 0)
    def _():
        l0, l1, l2 = l0_ref[...], l1_ref[...], l2_ref[...]
        mx = jnp.maximum(jnp.maximum(l0, l1), l2)
        e0, e1, e2 = jnp.exp(l0 - mx), jnp.exp(l1 - mx), jnp.exp(l2 - mx)
        inv = 1.0 / (e0 + e1 + e2)
        src = lax.broadcasted_iota(jnp.int32, (LANES, heads * e), 0)
        dst = lax.broadcasted_iota(jnp.int32, (LANES, heads * e), 1)
        spread = jnp.where(src == dst // e, 1.0, 0.0).astype(BF16)

        per_lane = lambda w: _dot(w.astype(BF16), spread)

        acc = per_lane(e0 * inv) * o0_ref[...].astype(F32)
        acc += per_lane(e1 * inv) * o1_ref[...].astype(F32)
        acc += per_lane(e2 * inv) * o2_ref[...].astype(F32)
        m_ref[...] = acc.astype(BF16)

    out_ref[...] = r_ref[...] + _dot(m_ref[...], w_ref[...])


def attention_layer(x, g, w_qkv, w_o, batch, seq):
    t, d = x.shape
    n_dil = len(ATTN_PATTERNS)
    he = w_o.shape[0]
    heads = he // ATTN_HEAD_DIM
    n_sl = n_dil * heads
    slopes = (2.0 ** (-8.0 * jnp.arange(1, n_sl + 1, dtype=F32) / n_sl)).reshape(n_dil, heads)
    outs, lses = [], []
    for grp, (window, dilation) in enumerate(ATTN_PATTERNS):
        assert window // dilation == ATTN_BLOCK and (seq // dilation) % ATTN_BLOCK == 0
        qkv = norm_matmul_strided(x, g, w_qkv, grp * 3 * he, 3 * he, dilation, batch, seq)
        o, l = _attn_group(qkv, slopes[grp] * dilation, grp, dilation, batch, seq, heads)
        outs.append(o)
        lses.append(l)
    tm, tn = _tile(t, 512), _tile(d, 1024)
    ospec = pl.BlockSpec((tm, he), lambda i, j: (i, 0))
    lspec = pl.BlockSpec((tm, LANES), lambda i, j: (i, 0))
    return pl.pallas_call(
        functools.partial(_attn_out_kernel, heads=heads),
        out_shape=jax.ShapeDtypeStruct((t, d), F32),
        grid=(t // tm, d // tn),
        in_specs=[ospec, ospec, ospec, lspec, lspec, lspec,
                  pl.BlockSpec((he, tn), lambda i, j: (0, j)),
                  pl.BlockSpec((tm, tn), lambda i, j: (i, j))],
        out_specs=pl.BlockSpec((tm, tn), lambda i, j: (i, j)),
        scratch_shapes=[pltpu.VMEM((tm, he), BF16)],
        compiler_params=_cparams("parallel", "arbitrary"),
        name="attn_merge_out_proj",
    )(*outs, *lses, w_o, x)


def _s5_chunk_operators(log_dt, a_re, a_im, b_re, b_im, c_re, c_im, d_skip, chunk):
    n_g, n_p = a_re.shape
    n_c = b_re.shape[-1]
    a_re, a_im = a_re.T, a_im.T
    b_re, b_im = b_re.transpose(1, 2, 0), b_im.transpose(1, 2, 0)
    c_re, c_im = c_re.transpose(1, 2, 0), c_im.transpose(1, 2, 0)
    dt = jnp.exp(log_dt)[None, :]
    mag = jnp.exp(dt * a_re)
    ab_re = mag * jnp.cos(dt * a_im)
    ab_im = mag * jnp.sin(dt * a_im)
    den = a_re * a_re + a_im * a_im
    zr = ab_re - 1.0
    cr = (zr * a_re + ab_im * a_im) / den
    ci = (ab_im * a_re - zr * a_im) / den
    bb_re = cr[:, None] * b_re - ci[:, None] * b_im
    bb_im = cr[:, None] * b_im + ci[:, None] * b_re
    pr, pi = [jnp.ones_like(ab_re)], [jnp.zeros_like(ab_re)]
    for _ in range(chunk):
        pr, pi = pr + [pr[-1] * ab_re - pi[-1] * ab_im], pi + [pr[-1] * ab_im + pi[-1] * ab_re]
    pr, pi = jnp.stack(pr), jnp.stack(pi)
    ce_re = c_re[None] * pr[:, None] - c_im[None] * pi[:, None]
    ce_im = c_re[None] * pi[:, None] + c_im[None] * pr[:, None]
    kern = jnp.sum(ce_re[:chunk, :, :, None] * bb_re[None, None] - ce_im[:chunk, :, :, None] * bb_im[None, None],
                   axis=2)
    kern = kern.at[0].add(jnp.eye(n_c, dtype=F32)[:, :, None] * d_skip.T[:, None, :])
    group_first = lambda a, *shape: a.astype(BF16).reshape(-1, n_g).T.reshape((n_g,) + shape)
    toep = jnp.stack([jnp.pad(kern[:chunk - l], ((l, 0), (0, 0), (0, 0), (0, 0))) for l in range(chunk)])
    m_op = group_first(toep.transpose(0, 3, 1, 2, 4), chunk * n_c, chunk * n_c)
    qr, qi = pr[chunk - 1::-1][:chunk, :, None], pi[chunk - 1::-1][:chunk, :, None]
    bo_re = (qr * bb_re[None] - qi * bb_im[None]).transpose(0, 2, 1, 3)
    bo_im = (qr * bb_im[None] + qi * bb_re[None]).transpose(0, 2, 1, 3)
    b_op = group_first(jnp.concatenate([bo_re, bo_im, bo_im, bo_re], axis=2), chunk * n_c, 4 * n_p)
    by_state = lambda a: a.transpose(2, 0, 1, 3)
    c_op = group_first(jnp.concatenate([by_state(ce_re[1:]), -by_state(ce_im[1:])], axis=0), 2 * n_p, chunk * n_c)
    gpb = LANES // n_c
    nblk = n_g // gpb
    al_re, al_im = pr[chunk].T, pi[chunk].T
    per_blk = lambda parts: jnp.concatenate(parts, axis=-1).reshape(nblk, 1, gpb * 2 * n_p)
    coef_same = jnp.concatenate([per_blk([al_re, al_re])] * 2, axis=-1)
    coef_cross = jnp.concatenate([per_blk([-al_im, al_im]), per_blk([al_im, -al_im])], axis=-1)
    return m_op, b_op, c_op, coef_same, coef_cross


def _chunk_rows(u_ref):
    return jnp.concatenate([u_ref[l] for l in range(u_ref.shape[0])], axis=-1)


def _first_visit_of_block():
    return (pl.program_id(1) == 0) & (pl.program_id(2) == 0)


def _s5_in_kernel(u_ref, b_ref, o_ref, dense_ref, *, n_c):
    @pl.when(_first_visit_of_block())
    def _():
        gpb, rows, cols = b_ref.shape
        half = cols // 2
        dense_ref[...] = jnp.zeros_like(dense_ref)
        for gi in range(gpb):
            for l in range(rows // n_c):
                r0 = l * LANES + gi * n_c
                piece = b_ref[gi, l * n_c:(l + 1) * n_c, :]
                dense_ref[r0:r0 + n_c, gi * half:(gi + 1) * half] = piece[:, :half]
                dense_ref[r0:r0 + n_c, (gpb + gi) * half:(gpb + gi + 1) * half] = piece[:, half:]

    o_ref[...] = _dot(_chunk_rows(u_ref), dense_ref[...])


def _s5_scan_kernel(xin_ref, cs_ref, cc_ref, o_ref, st_ref, *, half):
    @pl.when(pl.program_id(2) == 0)
    def _():
        st_ref[...] = jnp.zeros_like(st_ref)

    cs, cc = cs_ref[...], cc_ref[...]

    def step(n, st):
        o_ref[pl.ds(n, 1), :] = st[:, :half]
        st_sw = jnp.concatenate([st[:, half:], st[:, :half]], axis=-1)
        return cs * st + cc * st_sw + xin_ref[pl.ds(n, 1), :]

    st_ref[...] = lax.fori_loop(0, xin_ref.shape[0], step, st_ref[...])


def _s5_out_kernel(u_ref, m_ref, xp_ref, c_ref, o_ref, mdense_ref, cdense_ref, *, n_c):
    @pl.when(_first_visit_of_block())
    def _():
        gpb, rows, cols = m_ref.shape
        n_q = c_ref.shape[1]
        src = lax.broadcasted_iota(jnp.int32, (cols, mdense_ref.shape[1]), 0)
        dst = lax.broadcasted_iota(jnp.int32, (cols, mdense_ref.shape[1]), 1)
        for gi in range(gpb):
            spread = jnp.where(dst == (src // n_c) * LANES + gi * n_c + src % n_c, 1.0, 0.0).astype(BF16)
            wide = _dot(m_ref[gi], spread).astype(BF16)
            for l in range(rows // n_c):
                r0 = l * LANES + gi * n_c
                mdense_ref[r0:r0 + n_c, :] = wide[l * n_c:(l + 1) * n_c, :]
            cdense_ref[gi * n_q:(gi + 1) * n_q, :] = _dot(c_ref[gi], spread).astype(BF16)

    y = _dot(_chunk_rows(u_ref), mdense_ref[...]) + _dot(xp_ref[...].astype(BF16), cdense_ref[...])
    y = jax.nn.gelu(y).astype(o_ref.dtype)
    for step in range(o_ref.shape[0]):
        o_ref[step] = y[:, step * LANES:(step + 1) * LANES]


def _glu_out_kernel(y_ref, wa_ref, wb_ref, r_ref, o_ref, yn_ref, *, dilation, sub):
    @pl.when(pl.program_id(1) == 0)
    def _():
        tm, kdim = yn_ref.shape
        per = sub // dilation
        nat = lax.broadcasted_iota(jnp.int32, (sub, sub), 0)
        src = lax.broadcasted_iota(jnp.int32, (sub, sub), 1)
        perm = jnp.where(src == (nat % dilation) * per + nat // dilation, 1.0, 0.0).astype(BF16)
        for s in range(tm // sub):
            grouped = y_ref[:, s * per:(s + 1) * per, :].reshape(sub, kdim)
            yn_ref[s * sub:(s + 1) * sub, :] = _dot(perm, grouped).astype(BF16)

    y = yn_ref[...]
    o_ref[...] = r_ref[...] + _dot(y, wa_ref[...]) * jax.nn.sigmoid(_dot(y, wb_ref[...]))


def s5_layer(x, g, w_in, log_dt, a_re, a_im, b_re, b_im, c_re, c_im, d_skip, w_out, batch, seq):
    t, d = x.shape
    n_g, n_p = a_re.shape
    n_c = SSM_CH
    ck = SSM_CHUNK
    gc = n_g * n_c
    nblk = gc // LANES
    n_chunks = seq // ck
    sw = 4 * n_p * (LANES // n_c)
    m_op, b_op, c_op, coef_same, coef_cross = _s5_chunk_operators(
        log_dt, a_re, a_im, b_re, b_im, c_re, c_im, d_skip, ck)
    u = norm_matmul_strided(x, g, w_in, 0, gc, ck, batch, seq)
    tr = _tile(n_chunks, 512)
    u_spec = pl.BlockSpec((None, ck, tr, LANES), lambda k, b, n: (b, 0, n, k))
    gpb = LANES // n_c
    group_ops = lambda a: pl.BlockSpec((gpb,) + a.shape[1:], lambda k, b, n: (k, 0, 0))
    xin = pl.pallas_call(
        functools.partial(_s5_in_kernel, n_c=n_c),
        out_shape=jax.ShapeDtypeStruct((batch, n_chunks, nblk * sw), F32),
        grid=(nblk, batch, n_chunks // tr),
        in_specs=[u_spec, group_ops(b_op)],
        out_specs=pl.BlockSpec((None, tr, sw), lambda k, b, n: (b, n, k)),
        scratch_shapes=[pltpu.VMEM((ck * LANES, sw), BF16)],
        compiler_params=_cparams("arbitrary", "arbitrary", "arbitrary"),
        name="s5_chunk_inputs",
    )(u, b_op)
    coef_spec = pl.BlockSpec((None, 1, sw), lambda b, k, n: (k, 0, 0))
    xprev = pl.pallas_call(
        functools.partial(_s5_scan_kernel, half=sw // 2),
        out_shape=jax.ShapeDtypeStruct((batch, n_chunks, nblk * sw // 2), F32),
        grid=(batch, nblk, n_chunks // tr),
        in_specs=[pl.BlockSpec((None, tr, sw), lambda b, k, n: (b, n, k)), coef_spec, coef_spec],
        out_specs=pl.BlockSpec((None, tr, sw // 2), lambda b, k, n: (b, n, k)),
        scratch_shapes=[pltpu.VMEM((1, sw), F32)],
        compiler_params=_cparams("parallel", "parallel", "arbitrary"),
        name="s5_chunk_scan",
    )(xin, coef_same, coef_cross)
    y = pl.pallas_call(
        functools.partial(_s5_out_kernel, n_c=n_c),
        out_shape=jax.ShapeDtypeStruct((batch, ck, n_chunks, gc), BF16),
        grid=(nblk, batch, n_chunks // tr),
        in_specs=[u_spec, group_ops(m_op),
                  pl.BlockSpec((None, tr, sw // 2), lambda k, b, n: (b, n, k)), group_ops(c_op)],
        out_specs=u_spec,
        scratch_shapes=[pltpu.VMEM((ck * LANES, ck * LANES), BF16), pltpu.VMEM((sw // 2, ck * LANES), BF16)],
        compiler_params=_cparams("arbitrary", "arbitrary", "arbitrary"),
        name="s5_chunk_outputs",
    )(u, m_op, xprev, c_op)
    tm, tn2 = _tile(seq, 1024), _tile(d, 512)
    nt, nj = seq // tm, d // tn2
    return pl.pallas_call(
        functools.partial(_glu_out_kernel, dilation=ck, sub=min(tm, 512)),
        out_shape=jax.ShapeDtypeStruct((t, d), F32),
        grid=(t // tm, nj),
        in_specs=[pl.BlockSpec((None, ck, tm // ck, gc), lambda i, j: (i // nt, 0, i % nt, 0)),
                  pl.BlockSpec((gc, tn2), lambda i, j: (0, j)),
                  pl.BlockSpec((gc, tn2), lambda i, j: (0, j + nj)),
                  pl.BlockSpec((tm, tn2), lambda i, j: (i, j))],
        out_specs=pl.BlockSpec((tm, tn2), lambda i, j: (i, j)),
        scratch_shapes=[pltpu.VMEM((tm, gc), BF16)],
        compiler_params=_cparams("parallel", "arbitrary"),
        name="s5_glu_out_proj",
    )(y, w_out, w_out, x)


def _shift_norm(x_ref, xp_ref, g_ref, first):
    g = g_ref[...]
    h = _rms(x_ref[...], g)
    prev = _rms(xp_ref[7:8, :], g)
    prev = jnp.where(first, 0.0, prev)
    row = lax.broadcasted_iota(jnp.int32, h.shape, 0)
    hp = jnp.where(row == 0, prev, pltpu.roll(h, 1, 0))
    return h, hp


def _rwkv_proj_kernel(x_ref, xp_ref, g_ref, mu_ref, w_ref, o_ref, h_ref, d_ref, l_ref, *, tm, seq):
    i, j, n = pl.program_id(0), pl.program_id(1), pl.program_id(2)

    @pl.when((j == 0) & (n == 0))
    def _():
        h, hp = _shift_norm(x_ref, xp_ref, g_ref, (i * tm) % seq == 0)
        h_ref[...] = h
        d_ref[...] = hp - h

    @pl.when(n == 0)
    def _():
        l_ref[...] = (h_ref[...] + d_ref[...] * mu_ref[...]).astype(BF16)

    o_ref[...] = _dot(l_ref[...], w_ref[...])


def _softplus(z):
    return jnp.maximum(z, 0.0) + jnp.log(1.0 + jnp.exp(-jnp.abs(z)))


def _rwkv_lora_kernel(x_ref, xp_ref, g_ref, mu_ref, w0_ref, w1_ref, w2_ref, a0_ref, a1_ref, a2_ref,
                      g1_ref, g2_ref, lw_ref, a_ref, gate_ref, *, tm, seq):
    h, hp = _shift_norm(x_ref, xp_ref, g_ref, (pl.program_id(0) * tm) % seq == 0)
    dlt = hp - h
    xw = (h + dlt * mu_ref[0:1, :]).astype(BF16)
    xa = (h + dlt * mu_ref[1:2, :]).astype(BF16)
    xg = (h + dlt * mu_ref[2:3, :]).astype(BF16)
    wl = w0_ref[...] + _dot(jnp.tanh(_dot(xw, w1_ref[...])).astype(BF16), w2_ref[...])
    w = -_softplus(-wl) - 0.5
    lw_ref[...] = -jnp.exp(w)
    a_ref[...] = jax.nn.sigmoid(a0_ref[...] + _dot(_dot(xa, a1_ref[...]).astype(BF16), a2_ref[...]))
    gate_ref[...] = _dot(jax.nn.sigmoid(_dot(xg, g1_ref[...])).astype(BF16), g2_ref[...])


def _rwkv_core_kernel(r_ref, k_ref, v_ref, lw_ref, a_ref, gate_ref, kk_ref, ka_ref, rk_ref, lnw_ref, lnb_ref,
                      o_ref, s_ref, lhs_ref, rhs_ref, bk_ref, v2_ref, dec_ref, y_ref):
    ck, hd = RWKV_CHUNK, RWKV_HEAD_DIM
    nb, tc, width = r_ref.shape
    nch, ck2 = tc // ck, 2 * ck
    seqs = [(bi, slice(pi * LANES, (pi + 1) * LANES)) for bi in range(nb) for pi in range(width // LANES)]

    @pl.when(pl.program_id(1) == 0)
    def _():
        s_ref[...] = jnp.zeros_like(s_ref)

    lane = lax.broadcasted_iota(jnp.int32, (1, 1, LANES), 2)
    head_a = lane < hd
    hrow = lax.broadcasted_iota(jnp.int32, (LANES, LANES), 0) // hd
    hcol = lax.broadcasted_iota(jnp.int32, (LANES, LANES), 1) // hd
    head_ones = jnp.where(hrow == hcol, 1.0, 0.0).astype(BF16)
    trow = lax.broadcasted_iota(jnp.int32, (nch, ck2, ck), 1)
    tcol = lax.broadcasted_iota(jnp.int32, (nch, ck2, ck), 2)
    sum_ops = jnp.where((tcol <= trow) | (trow >= ck), 1.0, 0.0).astype(BF16)

    def chunk_sums(x):
        hi, lo = _split2(x)
        bdot = lambda t: lax.dot_general(sum_ops, t, (((2,), (1,)), ((0,), (0,))), preferred_element_type=F32)
        both = bdot(hi) + bdot(lo)
        return both[:, :ck], both[:, ck:]

    def head_sum(x):
        hi, lo = _split2(x)
        return _dot(hi, head_ones) + _dot(lo, head_ones)

    def stack_heads(x):
        xb = x.astype(BF16)
        zero = jnp.zeros_like(xb)
        return jnp.concatenate([jnp.where(head_a, xb, zero), jnp.where(head_a, zero, xb)], axis=1)

    for si, (bi, ls) in enumerate(seqs):
        k_all, a_all = k_ref[bi, :, ls], a_ref[bi, :, ls]
        kk = k_all * kk_ref[:, ls]
        kk = kk * lax.rsqrt(jnp.maximum(head_sum(kk * kk), 1e-24))
        k2 = k_all * (1.0 + (a_all - 1.0) * ka_ref[:, ls])
        lw = lw_ref[bi, :, ls]
        by_chunk = lambda x: x.reshape(nch, ck, LANES)
        cs, tot = chunk_sums(by_chunk(lw))
        gam_inv, gam_rem = jnp.exp(-cs), jnp.exp(tot - cs)
        atm = by_chunk(-kk) * jnp.exp(cs - by_chunk(lw))
        rm = by_chunk(r_ref[bi, :, ls]) * jnp.exp(cs)
        b3, k3 = by_chunk(kk * a_all), by_chunk(k2)
        lhs_ref[si] = jnp.concatenate([stack_heads(atm), stack_heads(rm)], axis=1)
        rhs_ref[si] = jnp.concatenate([stack_heads(b3 * gam_inv), stack_heads(k3 * gam_inv)], axis=1)
        bk_ref[si] = jnp.concatenate([stack_heads(b3 * gam_rem), stack_heads(k3 * gam_rem)], axis=1)
        v2_ref[si] = stack_heads(by_chunk(v_ref[bi, :, ls]))
        dec_ref[si] = jnp.exp(tot[:, 0:1, :])

    row = lax.broadcasted_iota(jnp.int32, (ck2, ck2), 0)
    col = lax.broadcasted_iota(jnp.int32, (ck2, ck2), 1)
    incl = col <= row
    strict = col < row
    eye = jnp.where(row == col, 1.0, 0.0)
    n_seq = len(seqs)
    each = lambda f, *lists: [f(*args) for args in zip(*lists)]

    def chunk_step(ci, carry):
        lhs = [lhs_ref[si, ci] for si in range(n_seq)]
        gram = each(lambda l, si: _dot_nt(l, rhs_ref[si, ci]), lhs, range(n_seq))
        a_ab = each(lambda g: jnp.where(strict, g[:ck2, :ck2], 0.0), gram)
        a_lo = each(lambda g: jnp.concatenate([jnp.where(strict, g[:ck2, ck2:], 0.0),
                                               jnp.where(incl, g[ck2:, ck2:], 0.0)], axis=0).astype(BF16), gram)
        a_rb = each(lambda g: jnp.where(incl, g[ck2:, :ck2], 0.0).astype(BF16), gram)
        inv = each(lambda a: eye + a, a_ab)
        pw = each(lambda a: _dot(a.astype(BF16), a.astype(BF16)), a_ab)
        m = 2
        while 2 * m < ck:
            pwb = each(lambda p: p.astype(BF16), pw)
            both = each(lambda p, t: _dot(jnp.concatenate([p, t.astype(BF16)], axis=0), p), pwb, inv)
            pw = each(lambda z: z[:ck2], both)
            inv = each(lambda t, z: t + z[ck2:], inv, both)
            m *= 2
        inv = each(lambda t, p: t + _dot(t.astype(BF16), p.astype(BF16)), inv, pw)
        v2 = [v2_ref[si, ci] for si in range(n_seq)]
        av = each(_dot, a_lo, v2)
        s = [s_ref[si] for si in range(n_seq)]
        xs = each(lambda l, st: _dot_nt(l, st.astype(BF16)), lhs, s)
        u = each(lambda t, x, w: _dot(t.astype(BF16), (x[:ck2] + w[:ck2]).astype(BF16)).astype(BF16), inv, xs, av)
        y2 = each(lambda x, w, arb, ub: x[ck2:] + w[ck2:] + _dot(arb, ub), xs, av, a_rb, u)
        sl = pl.ds(pl.multiple_of(ci * ck, ck), ck)
        for si, (bi, ls) in enumerate(seqs):
            y_ref[bi, sl, ls] = y2[si][:ck] + y2[si][ck:]
            uv = jnp.concatenate([u[si], v2[si]], axis=0)
            s_ref[si] = s[si] * dec_ref[si, ci] + _dot_tn(uv, bk_ref[si, ci])
        return carry

    lax.fori_loop(0, nch, chunk_step, 0)

    for bi, ls in seqs:
        y = y_ref[bi, :, ls]
        mean = head_sum(y) * (1.0 / hd)
        yc = y - mean
        var = _dot((yc * yc).astype(BF16), head_ones) * (1.0 / hd)
        yn = yc * lax.rsqrt(var + RWKV_GN_EPS) * lnw_ref[:, ls] + lnb_ref[:, ls]
        k2 = k_ref[bi, :, ls] * (1.0 + (a_ref[bi, :, ls] - 1.0) * ka_ref[:, ls])
        bonus = head_sum(r_ref[bi, :, ls] * k2 * rk_ref[:, ls]) * v_ref[bi, :, ls]
        o_ref[bi, :, ls] = ((yn + bonus) * gate_ref[bi, :, ls]).astype(o_ref.dtype)


def _pad_to(a, axis, size):
    pad = [(0, 0)] * a.ndim
    pad[axis] = (0, size - a.shape[axis])
    return jnp.pad(a, pad)


def rwkv_layer(x, g, mu, w_rkv, w0, w1, w2, a0, a1, a2, g1, g2, k_k, k_a, r_k, ln_w, ln_b, w_o, batch, seq):
    t, d = x.shape
    row = lambda p: p.reshape(1, d).astype(F32)
    g2d = g.reshape(1, d)
    tn = _tile(d, 512)
    prev_spec = lambda tm: pl.BlockSpec((8, d), (lambda i, *_: (jnp.maximum(i * (tm // 8) - 1, 0), 0)))
    tm = _tile(seq, 1024)
    rkv = pl.pallas_call(
        functools.partial(_rwkv_proj_kernel, tm=tm, seq=seq),
        out_shape=jax.ShapeDtypeStruct((3, t, d), F32),
        grid=(t // tm, 3, d // tn),
        in_specs=[pl.BlockSpec((tm, d), lambda i, j, n: (i, 0)),
                  prev_spec(tm),
                  pl.BlockSpec((1, d), lambda i, j, n: (0, 0)),
                  pl.BlockSpec((None, 1, d), lambda i, j, n: (j, 0, 0)),
                  pl.BlockSpec((None, d, tn), lambda i, j, n: (j, 0, n))],
        out_specs=pl.BlockSpec((None, tm, tn), lambda i, j, n: (j, i, n)),
        scratch_shapes=[pltpu.VMEM((tm, d), F32), pltpu.VMEM((tm, d), F32), pltpu.VMEM((tm, d), BF16)],
        compiler_params=_cparams("parallel", "arbitrary", "arbitrary"),
        name="rwkv_rkv_proj",
    )(x, x, g2d, mu[:3].reshape(3, 1, d), w_rkv)

    pad_rank = lambda w_a, w_b: (_pad_to(w_a, 1, -(-w_a.shape[1] // LANES) * LANES).astype(BF16),
                                 _pad_to(w_b, 0, -(-w_b.shape[0] // LANES) * LANES).astype(BF16))
    w1p, w2p = pad_rank(w1, w2)
    a1p, a2p = pad_rank(a1, a2)
    g1p, g2p = pad_rank(g1, g2)
    full = lambda a: pl.BlockSpec(a.shape, lambda i: (0,) * a.ndim)
    tm = _tile(seq, 256)
    tok = pl.BlockSpec((tm, d), lambda i: (i, 0))
    lora_in = [x, x, g2d, mu[3:6], row(w0), w1p, w2p, row(a0), a1p, a2p, g1p, g2p]
    lw, a_gate, gate = pl.pallas_call(
        functools.partial(_rwkv_lora_kernel, tm=tm, seq=seq),
        out_shape=(jax.ShapeDtypeStruct((t, d), F32),) * 3,
        grid=(t // tm,),
        in_specs=[tok, prev_spec(tm)] + [full(a) for a in lora_in[2:]],
        out_specs=(tok, tok, tok),
        compiler_params=_cparams("parallel"),
        name="rwkv_lora",
    )(*lora_in)

    tc = _tile(seq, 512)
    wd = _tile(d, 4 * LANES)
    tokc = pl.BlockSpec((batch, tc, wd), lambda p, c: (0, c, p))
    rkvc = lambda which: pl.BlockSpec((None, batch, tc, wd), lambda p, c: (which, 0, c, p))
    par = pl.BlockSpec((1, wd), lambda p, c: (0, p))
    n_seq = batch * (wd // LANES)
    nch = tc // RWKV_CHUNK
    bsd = lambda a: a.reshape(batch, seq, d)
    rkv4 = rkv.reshape(3, batch, seq, d)
    mixed = pl.pallas_call(
        _rwkv_core_kernel,
        out_shape=jax.ShapeDtypeStruct((batch, seq, d), BF16),
        grid=(d // wd, seq // tc),
        in_specs=[rkvc(0), rkvc(1), rkvc(2), tokc, tokc, tokc, par, par, par, par, par],
        out_specs=tokc,
        scratch_shapes=[pltpu.VMEM((n_seq, LANES, LANES), F32)]
        + [pltpu.VMEM((n_seq, nch, 4 * RWKV_CHUNK, LANES), BF16)] * 3
        + [pltpu.VMEM((n_seq, nch, 2 * RWKV_CHUNK, LANES), BF16),
           pltpu.VMEM((n_seq, nch, 1, LANES), F32),
           pltpu.VMEM((batch, tc, wd), F32)],
        compiler_params=_cparams("parallel", "arbitrary"),
        name="rwkv_chunked_state",
    )(rkv4, rkv4, rkv4, bsd(lw), bsd(a_gate), bsd(gate), row(k_k), row(k_a), row(r_k), row(ln_w), row(ln_b))
    return matmul_residual(mixed.reshape(t, d), w_o, x)


def kernel(x, norm_mix, norm_mlp, norm_f, attn_w_qkv, attn_w_o, ssm_w_in, ssm_log_dt, ssm_a_re, ssm_a_im,
           ssm_b_re, ssm_b_im, ssm_c_re, ssm_c_im, ssm_d, ssm_w_out, rwkv_mu, rwkv_w_rkv, rwkv_w0, rwkv_w1,
           rwkv_w2, rwkv_a0, rwkv_a1, rwkv_a2, rwkv_g1, rwkv_g2, rwkv_k_k, rwkv_k_a, rwkv_r_k, rwkv_ln_w,
           rwkv_ln_b, rwkv_w_o, mlp_w1, mlp_w2):
    batch, seq, d = x.shape
    depth = norm_mix.shape[0]
    bf = lambda w: w.astype(BF16)
    h = x.reshape(batch * seq, d)
    ia = ib = ic = 0
    for layer in range(depth):
        kind = layer % 3
        if kind == 0:
            h = attention_layer(h, norm_mix[layer], bf(attn_w_qkv[ia]), bf(attn_w_o[ia]), batch, seq)
            ia += 1
        elif kind == 1:
            h = s5_layer(h, norm_mix[layer], bf(ssm_w_in[ib]), ssm_log_dt[ib], ssm_a_re[ib], ssm_a_im[ib],
                         ssm_b_re[ib], ssm_b_im[ib], ssm_c_re[ib], ssm_c_im[ib], ssm_d[ib],
                         bf(ssm_w_out[ib]), batch, seq)
            ib += 1
        else:
            h = rwkv_layer(h, norm_mix[layer], rwkv_mu[ic], bf(rwkv_w_rkv[ic]), rwkv_w0[ic], rwkv_w1[ic],
                           rwkv_w2[ic], rwkv_a0[ic], rwkv_a1[ic], rwkv_a2[ic], rwkv_g1[ic], rwkv_g2[ic],
                           rwkv_k_k[ic], rwkv_k_a[ic], rwkv_r_k[ic], rwkv_ln_w[ic], rwkv_ln_b[ic],
                           bf(rwkv_w_o[ic]), batch, seq)
            ic += 1
        g_final = norm_f if layer == depth - 1 else None
        h = mlp_residual(h, norm_mlp[layer], bf(mlp_w1[layer]), bf(mlp_w2[layer]), g_final)
    return h.reshape(batch, seq, d)
```

```python
import functools

import jax
import jax.numpy as jnp
from jax import lax
from jax.experimental import pallas as pl
from jax.experimental.pallas import tpu as pltpu

F32 = jnp.float32
BF16 = jnp.bfloat16

NORM_EPS = 1e-5
LANES = 128
VMEM_LIMIT_BYTES = 56 * 2**20
MASK_VALUE = -1e30

ATTN_PATTERNS = ((128, 1), (512, 4), (2048, 16))
ATTN_BLOCK = 128
ATTN_HEAD_DIM = 128
SSM_CH = 16
SSM_CHUNK = 16
RWKV_HEAD_DIM = 64
RWKV_CHUNK = 64
RWKV_GN_EPS = RWKV_HEAD_DIM * 1e-5


def _cparams(*sem):
    return pltpu.CompilerParams(dimension_semantics=sem, vmem_limit_bytes=VMEM_LIMIT_BYTES)


def _tile(n, pref):
    t = min(n, pref)
    while n % t:
        t //= 2
    return t


def _rms(x, g):
    ms = jnp.mean(x * x, axis=-1, keepdims=True)
    return x * lax.rsqrt(ms + NORM_EPS) * g


def _dot(a, b):
    return jnp.dot(a, b, preferred_element_type=F32)


def _dot_nt(a, b):
    return lax.dot_general(a, b, (((1,), (1,)), ((), ())), preferred_element_type=F32)


def _dot_tn(a, b):
    return lax.dot_general(a, b, (((0,), (0,)), ((), ())), preferred_element_type=F32)


def _split2(x):
    hi = x.astype(BF16)
    return hi, (x - hi.astype(F32)).astype(BF16)


def _norm_matmul_strided_kernel(x_ref, g_ref, w_ref, o_ref, h_ref, *, dilation, sub):
    tm = h_ref.shape[0]
    per = sub // dilation

    @pl.when(pl.program_id(1) == 0)
    def _():
        if dilation > 1:
            new = lax.broadcasted_iota(jnp.int32, (sub, sub), 0)
            old = lax.broadcasted_iota(jnp.int32, (sub, sub), 1)
            perm = jnp.where(old == (new % per) * dilation + new // per, 1.0, 0.0).astype(BF16)
        for s in range(tm // sub):
            rows = slice(s * sub, (s + 1) * sub)
            h = _rms(x_ref[rows, :], g_ref[...]).astype(BF16)
            h_ref[rows, :] = _dot(perm, h).astype(BF16) if dilation > 1 else h

    y = _dot(h_ref[...], w_ref[...]).astype(o_ref.dtype)
    for s in range(tm // sub):
        o_ref[:, s * per:(s + 1) * per, :] = y[s * sub:(s + 1) * sub].reshape(dilation, per, y.shape[-1])


def norm_matmul_strided(x, g, w, col0, ncols, dilation, batch, seq, tm=1024, tn=1024, sub=512):
    t, d = x.shape
    tm, tn = _tile(seq, tm), _tile(ncols, tn)
    sub = min(sub, tm)
    assert col0 % tn == 0 and (sub // dilation) % 16 == 0
    nt = seq // tm
    return pl.pallas_call(
        functools.partial(_norm_matmul_strided_kernel, dilation=dilation, sub=sub),
        out_shape=jax.ShapeDtypeStruct((batch, dilation, seq // dilation, ncols), BF16),
        grid=(t // tm, ncols // tn),
        in_specs=[pl.BlockSpec((tm, d), lambda i, j: (i, 0)),
                  pl.BlockSpec((1, d), lambda i, j: (0, 0)),
                  pl.BlockSpec((d, tn), lambda i, j: (0, col0 // tn + j))],
        out_specs=pl.BlockSpec((None, dilation, tm // dilation, tn), lambda i, j: (i // nt, 0, i % nt, j)),
        scratch_shapes=[pltpu.VMEM((tm, d), BF16)],
        compiler_params=_cparams("parallel", "arbitrary"),
        name=f"norm_matmul_stride{dilation}",
    )(x, g.reshape(1, d), w)


def _matmul_res_kernel(a_ref, w_ref, r_ref, o_ref):
    o_ref[...] = r_ref[...] + _dot(a_ref[...], w_ref[...])


def matmul_residual(a, w, res, tm=1024, tn=1024):
    t, k = a.shape
    n = w.shape[1]
    tm, tn = _tile(t, tm), _tile(n, tn)
    return pl.pallas_call(
        _matmul_res_kernel,
        out_shape=jax.ShapeDtypeStruct((t, n), F32),
        grid=(t // tm, n // tn),
        in_specs=[pl.BlockSpec((tm, k), lambda i, j: (i, 0)),
                  pl.BlockSpec((k, tn), lambda i, j: (0, j)),
                  pl.BlockSpec((tm, tn), lambda i, j: (i, j))],
        out_specs=pl.BlockSpec((tm, tn), lambda i, j: (i, j)),
        compiler_params=_cparams("parallel", "parallel"),
        name="matmul_residual",
    )(a, w, res)


def _mlp_kernel(x_ref, g_ref, w1_ref, w2_ref, gf_ref, o_ref, h_ref, *, final_norm):
    f = pl.program_id(1)

    @pl.when(f == 0)
    def _():
        x = x_ref[...]
        h_ref[...] = _rms(x, g_ref[...]).astype(BF16)
        o_ref[...] = x

    a = _dot(h_ref[...], w1_ref[...])
    a = jnp.square(jnp.maximum(a, 0.0)).astype(BF16)
    o_ref[...] += _dot(a, w2_ref[...])

    if final_norm:
        @pl.when(f == pl.num_programs(1) - 1)
        def _():
            o_ref[...] = _rms(o_ref[...], gf_ref[...])


def mlp_residual(x, g, w1, w2, g_final=None, tm=1024, tf=512):
    t, d = x.shape
    ff = w1.shape[1]
    tm, tf = _tile(t, tm), _tile(ff, tf)
    final_norm = g_final is not None
    gf = (g_final if final_norm else g).reshape(1, d)
    return pl.pallas_call(
        functools.partial(_mlp_kernel, final_norm=final_norm),
        out_shape=jax.ShapeDtypeStruct((t, d), F32),
        grid=(t // tm, ff // tf),
        in_specs=[pl.BlockSpec((tm, d), lambda i, f: (i, 0)),
                  pl.BlockSpec((1, d), lambda i, f: (0, 0)),
                  pl.BlockSpec((d, tf), lambda i, f: (0, f)),
                  pl.BlockSpec((tf, d), lambda i, f: (f, 0)),
                  pl.BlockSpec((1, d), lambda i, f: (0, 0))],
        out_specs=pl.BlockSpec((tm, d), lambda i, f: (i, 0)),
        scratch_shapes=[pltpu.VMEM((tm, d), BF16)],
        compiler_params=_cparams("parallel", "arbitrary"),
        name="mlp_residual",
    )(x, g.reshape(1, d), w1, w2, gf)


def _attn_kernel(slope_ref, qkv_ref, o_ref, lse_ref, kvp_ref, *, heads, scale):
    blk, e = ATTN_BLOCK, ATTN_HEAD_DIM
    he = heads * e
    j = pl.program_id(2)

    @pl.when(j == 0)
    def _():
        kvp_ref[...] = jnp.zeros_like(kvp_ref)

    qi = lax.broadcasted_iota(jnp.int32, (blk, blk), 0)
    kj = lax.broadcasted_iota(jnp.int32, (blk, blk), 1)
    log2e, ln2 = 1.4426950408889634, 0.6931471805599453
    dist_c = (qi - kj).astype(F32) * log2e
    dist_p = dist_c + float(blk) * log2e
    mask_c = jnp.where(kj <= qi, 0.0, MASK_VALUE)
    mask_p = jnp.where(kj >= qi, 0.0, MASK_VALUE)
    mask_first = jnp.where((kj >= qi) & (j > 0), 0.0, MASK_VALUE)
    lane = lax.broadcasted_iota(jnp.int32, (blk, LANES), 1)
    together = next(n for n in (8, 4, 2, 1) if heads % n == 0)
    for sb in range(qkv_ref.shape[0] // blk):
        rows = slice(sb * blk, (sb + 1) * blk)
        q_ref, kc_ref, vc_ref = (qkv_ref.at[rows, c * he:(c + 1) * he] for c in range(3))
        if sb == 0:
            kp_ref, vp_ref, mask_prev = kvp_ref.at[:, 0:he], kvp_ref.at[:, he:2 * he], mask_first
        else:
            before = slice((sb - 1) * blk, sb * blk)
            kp_ref, vp_ref, mask_prev = qkv_ref.at[before, he:2 * he], qkv_ref.at[before, 2 * he:3 * he], mask_p
        lse_tile = jnp.zeros((blk, LANES), F32)
        outs = []
        for h0 in range(0, heads, together):
            hs = list(range(h0, h0 + together))
            sls = [slice(h * e, (h + 1) * e) for h in hs]
            qs = [q_ref[:, sl] for sl in sls]
            sc = [_dot_nt(q, kc_ref[:, sl]) for q, sl in zip(qs, sls)]
            sp = [_dot_nt(q, kp_ref[:, sl]) for q, sl in zip(qs, sls)]
            sc = [s * (scale * log2e) + (mask_c - slope_ref[h] * dist_c) for s, h in zip(sc, hs)]
            sp = [s * (scale * log2e) + (mask_prev - slope_ref[h] * dist_p) for s, h in zip(sp, hs)]
            m = [jnp.max(jnp.maximum(a, b), axis=-1, keepdims=True) for a, b in zip(sc, sp)]
            pc = [jnp.exp2(a - mm) for a, mm in zip(sc, m)]
            pp = [jnp.exp2(b - mm) for b, mm in zip(sp, m)]
            den = [jnp.sum(a + b, axis=-1, keepdims=True) for a, b in zip(pc, pp)]
            o = [_dot(a.astype(BF16), vc_ref[:, sl]) + _dot(b.astype(BF16), vp_ref[:, sl])
                 for a, b, sl in zip(pc, pp, sls)]
            for h, oo, dd, mm in zip(hs, o, den, m):
                outs.append((oo / dd).astype(o_ref.dtype))
                lse_tile = jnp.where(lane == h, mm * ln2 + jnp.log(dd), lse_tile)
        for h, oo in enumerate(outs):
            o_ref[rows, h * e:(h + 1) * e] = oo
        lse_ref[rows, :] = lse_tile
    kvp_ref[...] = qkv_ref[qkv_ref.shape[0] - blk:, he:3 * he]


def _attn_group(qkv, slopes, group, dilation, batch, seq, heads):
    e, blk = ATTN_HEAD_DIM, ATTN_BLOCK
    he = heads * e
    sub = seq // dilation
    rows = _tile(sub, 2 * blk)
    out, lse = pl.pallas_call(
        functools.partial(_attn_kernel, heads=heads, scale=e ** -0.5),
        out_shape=(jax.ShapeDtypeStruct((batch, dilation, sub, he), BF16),
                   jax.ShapeDtypeStruct((batch, dilation, sub, LANES), F32)),
        grid=(batch, dilation, sub // rows),
        in_specs=[pl.BlockSpec(memory_space=pltpu.SMEM),
                  pl.BlockSpec((None, None, rows, 3 * he), lambda b, r, j: (b, r, j, 0))],
        out_specs=(pl.BlockSpec((None, None, rows, he), lambda b, r, j: (b, r, j, 0)),
                   pl.BlockSpec((None, None, rows, LANES), lambda b, r, j: (b, r, j, 0))),
        scratch_shapes=[pltpu.VMEM((blk, 2 * he), BF16)],
        compiler_params=_cparams("arbitrary", "arbitrary", "arbitrary"),
        name=f"dilated_attn_g{group}",
    )(slopes, qkv)
    natural = lambda a: a.transpose(0, 2, 1, 3).reshape(batch * seq, a.shape[-1])
    return natural(out), natural(lse)


def _attn_out_kernel(o0_ref, o1_ref, o2_ref, l0_ref, l1_ref, l2_ref, w_ref, r_ref, out_ref, m_ref, *, heads):
    e = ATTN_HEAD_DIM

    @pl.when(pl.program_id(1) == 0)
    def _():
        l0, l1, l2 = l0_ref[...], l1_ref[...], l2_ref[...]
        mx = jnp.maximum(jnp.maximum(l0, l1), l2)
        e0, e1, e2 = jnp.exp(l0 - mx), jnp.exp(l1 - mx), jnp.exp(l2 - mx)
        inv = 1.0 / (e0 + e1 + e2)
        src = lax.broadcasted_iota(jnp.int32, (LANES, heads * e), 0)
        dst = lax.broadcasted_iota(jnp.int32, (LANES, heads * e), 1)
        spread = jnp.where(src == dst // e, 1.0, 0.0).astype(BF16)

        per_lane = lambda w: _dot(w.astype(BF16), spread)

        acc = per_lane(e0 * inv) * o0_ref[...].astype(F32)
        acc += per_lane(e1 * inv) * o1_ref[...].astype(F32)
        acc += per_lane(e2 * inv) * o2_ref[...].astype(F32)
        m_ref[...] = acc.astype(BF16)

    out_ref[...] = r_ref[...] + _dot(m_ref[...], w_ref[...])


def attention_layer(x, g, w_qkv, w_o, batch, seq):
    t, d = x.shape
    n_dil = len(ATTN_PATTERNS)
    he = w_o.shape[0]
    heads = he // ATTN_HEAD_DIM
    n_sl = n_dil * heads
    slopes = (2.0 ** (-8.0 * jnp.arange(1, n_sl + 1, dtype=F32) / n_sl)).reshape(n_dil, heads)
    outs, lses = [], []
    for grp, (window, dilation) in enumerate(ATTN_PATTERNS):
        assert window // dilation == ATTN_BLOCK and (seq // dilation) % ATTN_BLOCK == 0
        qkv = norm_matmul_strided(x, g, w_qkv, grp * 3 * he, 3 * he, dilation, batch, seq)
        o, l = _attn_group(qkv, slopes[grp] * dilation, grp, dilation, batch, seq, heads)
        outs.append(o)
        lses.append(l)
    tm, tn = _tile(t, 512), _tile(d, 1024)
    ospec = pl.BlockSpec((tm, he), lambda i, j: (i, 0))
    lspec = pl.BlockSpec((tm, LANES), lambda i, j: (i, 0))
    return pl.pallas_call(
        functools.partial(_attn_out_kernel, heads=heads),
        out_shape=jax.ShapeDtypeStruct((t, d), F32),
        grid=(t // tm, d // tn),
        in_specs=[ospec, ospec, ospec, lspec, lspec, lspec,
                  pl.BlockSpec((he, tn), lambda i, j: (0, j)),
                  pl.BlockSpec((tm, tn), lambda i, j: (i, j))],
        out_specs=pl.BlockSpec((tm, tn), lambda i, j: (i, j)),
        scratch_shapes=[pltpu.VMEM((tm, he), BF16)],
        compiler_params=_cparams("parallel", "arbitrary"),
        name="attn_merge_out_proj",
    )(*outs, *lses, w_o, x)


def _s5_chunk_operators(log_dt, a_re, a_im, b_re, b_im, c_re, c_im, d_skip, chunk):
    n_g, n_p = a_re.shape
    n_c = b_re.shape[-1]
    a_re, a_im = a_re.T, a_im.T
    b_re, b_im = b_re.transpose(1, 2, 0), b_im.transpose(1, 2, 0)
    c_re, c_im = c_re.transpose(1, 2, 0), c_im.transpose(1, 2, 0)
    dt = jnp.exp(log_dt)[None, :]
    mag = jnp.exp(dt * a_re)
    ab_re = mag * jnp.cos(dt * a_im)
    ab_im = mag * jnp.sin(dt * a_im)
    den = a_re * a_re + a_im * a_im
    zr = ab_re - 1.0
    cr = (zr * a_re + ab_im * a_im) / den
    ci = (ab_im * a_re - zr * a_im) / den
    bb_re = cr[:, None] * b_re - ci[:, None] * b_im
    bb_im = cr[:, None] * b_im + ci[:, None] * b_re
    pr, pi = [jnp.ones_like(ab_re)], [jnp.zeros_like(ab_re)]
    for _ in range(chunk):
        pr, pi = pr + [pr[-1] * ab_re - pi[-1] * ab_im], pi + [pr[-1] * ab_im + pi[-1] * ab_re]
    pr, pi = jnp.stack(pr), jnp.stack(pi)
    ce_re = c_re[None] * pr[:, None] - c_im[None] * pi[:, None]
    ce_im = c_re[None] * pi[:, None] + c_im[None] * pr[:, None]
    kern = jnp.sum(ce_re[:chunk, :, :, None] * bb_re[None, None] - ce_im[:chunk, :, :, None] * bb_im[None, None],
                   axis=2)
    kern = kern.at[0].add(jnp.eye(n_c, dtype=F32)[:, :, None] * d_skip.T[:, None, :])
    group_first = lambda a, *shape: a.astype(BF16).reshape(-1, n_g).T.reshape((n_g,) + shape)
    toep = jnp.stack([jnp.pad(kern[:chunk - l], ((l, 0), (0, 0), (0, 0), (0, 0))) for l in range(chunk)])
    m_op = group_first(toep.transpose(0, 3, 1, 2, 4), chunk * n_c, chunk * n_c)
    qr, qi = pr[chunk - 1::-1][:chunk, :, None], pi[chunk - 1::-1][:chunk, :, None]
    bo_re = (qr * bb_re[None] - qi * bb_im[None]).transpose(0, 2, 1, 3)
    bo_im = (qr * bb_im[None] + qi * bb_re[None]).transpose(0, 2, 1, 3)
    b_op = group_first(jnp.concatenate([bo_re, bo_im, bo_im, bo_re], axis=2), chunk * n_c, 4 * n_p)
    by_state = lambda a: a.transpose(2, 0, 1, 3)
    c_op = group_first(jnp.concatenate([by_state(ce_re[1:]), -by_state(ce_im[1:])], axis=0), 2 * n_p, chunk * n_c)
    gpb = LANES // n_c
    nblk = n_g // gpb
    al_re, al_im = pr[chunk].T, pi[chunk].T
    per_blk = lambda parts: jnp.concatenate(parts, axis=-1).reshape(nblk, 1, gpb * 2 * n_p)
    coef_same = jnp.concatenate([per_blk([al_re, al_re])] * 2, axis=-1)
    coef_cross = jnp.concatenate([per_blk([-al_im, al_im]), per_blk([al_im, -al_im])], axis=-1)
    return m_op, b_op, c_op, coef_same, coef_cross


def _chunk_rows(u_ref):
    return jnp.concatenate([u_ref[l] for l in range(u_ref.shape[0])], axis=-1)


def _first_visit_of_block():
    return (pl.program_id(1) == 0) & (pl.program_id(2) == 0)


def _s5_in_kernel(u_ref, b_ref, o_ref, dense_ref, *, n_c):
    @pl.when(_first_visit_of_block())
    def _():
        gpb, rows, cols = b_ref.shape
        half = cols // 2
        dense_ref[...] = jnp.zeros_like(dense_ref)
        for gi in range(gpb):
            for l in range(rows // n_c):
                r0 = l * LANES + gi * n_c
                piece = b_ref[gi, l * n_c:(l + 1) * n_c, :]
                dense_ref[r0:r0 + n_c, gi * half:(gi + 1) * half] = piece[:, :half]
                dense_ref[r0:r0 + n_c, (gpb + gi) * half:(gpb + gi + 1) * half] = piece[:, half:]

    o_ref[...] = _dot(_chunk_rows(u_ref), dense_ref[...])


def _s5_scan_kernel(xin_ref, cs_ref, cc_ref, o_ref, st_ref, *, half):
    @pl.when(pl.program_id(2) == 0)
    def _():
        st_ref[...] = jnp.zeros_like(st_ref)

    cs, cc = cs_ref[...], cc_ref[...]

    def step(n, st):
        o_ref[pl.ds(n, 1), :] = st[:, :half]
        st_sw = jnp.concatenate([st[:, half:], st[:, :half]], axis=-1)
        return cs * st + cc * st_sw + xin_ref[pl.ds(n, 1), :]

    st_ref[...] = lax.fori_loop(0, xin_ref.shape[0], step, st_ref[...])


def _s5_out_kernel(u_ref, m_ref, xp_ref, c_ref, o_ref, mdense_ref, cdense_ref, *, n_c):
    @pl.when(_first_visit_of_block())
    def _():
        gpb, rows, cols = m_ref.shape
        n_q = c_ref.shape[1]
        src = lax.broadcasted_iota(jnp.int32, (cols, mdense_ref.shape[1]), 0)
        dst = lax.broadcasted_iota(jnp.int32, (cols, mdense_ref.shape[1]), 1)
        for gi in range(gpb):
            spread = jnp.where(dst == (src // n_c) * LANES + gi * n_c + src % n_c, 1.0, 0.0).astype(BF16)
            wide = _dot(m_ref[gi], spread).astype(BF16)
            for l in range(rows // n_c):
                r0 = l * LANES + gi * n_c
                mdense_ref[r0:r0 + n_c, :] = wide[l * n_c:(l + 1) * n_c, :]
            cdense_ref[gi * n_q:(gi + 1) * n_q, :] = _dot(c_ref[gi], spread).astype(BF16)

    y = _dot(_chunk_rows(u_ref), mdense_ref[...]) + _dot(xp_ref[...].astype(BF16), cdense_ref[...])
    y = jax.nn.gelu(y).astype(o_ref.dtype)
    for step in range(o_ref.shape[0]):
        o_ref[step] = y[:, step * LANES:(step + 1) * LANES]


def _glu_out_kernel(y_ref, wa_ref, wb_ref, r_ref, o_ref, yn_ref, *, dilation, sub):
    @pl.when(pl.program_id(1) == 0)
    def _():
        tm, kdim = yn_ref.shape
        per = sub // dilation
        nat = lax.broadcasted_iota(jnp.int32, (sub, sub), 0)
        src = lax.broadcasted_iota(jnp.int32, (sub, sub), 1)
        perm = jnp.where(src == (nat % dilation) * per + nat // dilation, 1.0, 0.0).astype(BF16)
        for s in range(tm // sub):
            grouped = y_ref[:, s * per:(s + 1) * per, :].reshape(sub, kdim)
            yn_ref[s * sub:(s + 1) * sub, :] = _dot(perm, grouped).astype(BF16)

    y = yn_ref[...]
    o_ref[...] = r_ref[...] + _dot(y, wa_ref[...]) * jax.nn.sigmoid(_dot(y, wb_ref[...]))


def s5_layer(x, g, w_in, log_dt, a_re, a_im, b_re, b_im, c_re, c_im, d_skip, w_out, batch, seq):
    t, d = x.shape
    n_g, n_p = a_re.shape
    n_c = SSM_CH
    ck = SSM_CHUNK
    gc = n_g * n_c
    nblk = gc // LANES
    n_chunks = seq // ck
    sw = 4 * n_p * (LANES // n_c)
    m_op, b_op, c_op, coef_same, coef_cross = _s5_chunk_operators(
        log_dt, a_re, a_im, b_re, b_im, c_re, c_im, d_skip, ck)
    u = norm_matmul_strided(x, g, w_in, 0, gc, ck, batch, seq)
    tr = _tile(n_chunks, 512)
    u_spec = pl.BlockSpec((None, ck, tr, LANES), lambda k, b, n: (b, 0, n, k))
    gpb = LANES // n_c
    group_ops = lambda a: pl.BlockSpec((gpb,) + a.shape[1:], lambda k, b, n: (k, 0, 0))
    xin = pl.pallas_call(
        functools.partial(_s5_in_kernel, n_c=n_c),
        out_shape=jax.ShapeDtypeStruct((batch, n_chunks, nblk * sw), F32),
        grid=(nblk, batch, n_chunks // tr),
        in_specs=[u_spec, group_ops(b_op)],
        out_specs=pl.BlockSpec((None, tr, sw), lambda k, b, n: (b, n, k)),
        scratch_shapes=[pltpu.VMEM((ck * LANES, sw), BF16)],
        compiler_params=_cparams("arbitrary", "arbitrary", "arbitrary"),
        name="s5_chunk_inputs",
    )(u, b_op)
    coef_spec = pl.BlockSpec((None, 1, sw), lambda b, k, n: (k, 0, 0))
    xprev = pl.pallas_call(
        functools.partial(_s5_scan_kernel, half=sw // 2),
        out_shape=jax.ShapeDtypeStruct((batch, n_chunks, nblk * sw // 2), F32),
        grid=(batch, nblk, n_chunks // tr),
        in_specs=[pl.BlockSpec((None, tr, sw), lambda b, k, n: (b, n, k)), coef_spec, coef_spec],
        out_specs=pl.BlockSpec((None, tr, sw // 2), lambda b, k, n: (b, n, k)),
        scratch_shapes=[pltpu.VMEM((1, sw), F32)],
        compiler_params=_cparams("parallel", "parallel", "arbitrary"),
        name="s5_chunk_scan",
    )(xin, coef_same, coef_cross)
    y = pl.pallas_call(
        functools.partial(_s5_out_kernel, n_c=n_c),
        out_shape=jax.ShapeDtypeStruct((batch, ck, n_chunks, gc), BF16),
        grid=(nblk, batch, n_chunks // tr),
        in_specs=[u_spec, group_ops(m_op),
                  pl.BlockSpec((None, tr, sw // 2), lambda k, b, n: (b, n, k)), group_ops(c_op)],
        out_specs=u_spec,
        scratch_shapes=[pltpu.VMEM((ck * LANES, ck * LANES), BF16), pltpu.VMEM((sw // 2, ck * LANES), BF16)],
        compiler_params=_cparams("arbitrary", "arbitrary", "arbitrary"),
        name="s5_chunk_outputs",
    )(u, m_op, xprev, c_op)
    tm, tn2 = _tile(seq, 1024), _tile(d, 512)
    nt, nj = seq // tm, d // tn2
    return pl.pallas_call(
        functools.partial(_glu_out_kernel, dilation=ck, sub=min(tm, 512)),
        out_shape=jax.ShapeDtypeStruct((t, d), F32),
        grid=(t // tm, nj),
        in_specs=[pl.BlockSpec((None, ck, tm // ck, gc), lambda i, j: (i // nt, 0, i % nt, 0)),
                  pl.BlockSpec((gc, tn2), lambda i, j: (0, j)),
                  pl.BlockSpec((gc, tn2), lambda i, j: (0, j + nj)),
                  pl.BlockSpec((tm, tn2), lambda i, j: (i, j))],
        out_specs=pl.BlockSpec((tm, tn2), lambda i, j: (i, j)),
        scratch_shapes=[pltpu.VMEM((tm, gc), BF16)],
        compiler_params=_cparams("parallel", "arbitrary"),
        name="s5_glu_out_proj",
    )(y, w_out, w_out, x)


def _shift_norm(x_ref, xp_ref, g_ref, first):
    g = g_ref[...]
    h = _rms(x_ref[...], g)
    prev = _rms(xp_ref[7:8, :], g)
    prev = jnp.where(first, 0.0, prev)
    row = lax.broadcasted_iota(jnp.int32, h.shape, 0)
    hp = jnp.where(row == 0, prev, pltpu.roll(h, 1, 0))
    return h, hp


def _rwkv_proj_kernel(x_ref, xp_ref, g_ref, mu_ref, w_ref, o_ref, h_ref, d_ref, l_ref, *, tm, seq):
    i, j, n = pl.program_id(0), pl.program_id(1), pl.program_id(2)

    @pl.when((j == 0) & (n == 0))
    def _():
        h, hp = _shift_norm(x_ref, xp_ref, g_ref, (i * tm) % seq == 0)
        h_ref[...] = h
        d_ref[...] = hp - h

    @pl.when(n == 0)
    def _():
        l_ref[...] = (h_ref[...] + d_ref[...] * mu_ref[...]).astype(BF16)

    o_ref[...] = _dot(l_ref[...], w_ref[...])


def _softplus(z):
    return jnp.maximum(z, 0.0) + jnp.log(1.0 + jnp.exp(-jnp.abs(z)))


def _rwkv_lora_kernel(x_ref, xp_ref, g_ref, mu_ref, w0_ref, w1_ref, w2_ref, a0_ref, a1_ref, a2_ref,
                      g1_ref, g2_ref, lw_ref, a_ref, gate_ref, *, tm, seq):
    h, hp = _shift_norm(x_ref, xp_ref, g_ref, (pl.program_id(0) * tm) % seq == 0)
    dlt = hp - h
    xw = (h + dlt * mu_ref[0:1, :]).astype(BF16)
    xa = (h + dlt * mu_ref[1:2, :]).astype(BF16)
    xg = (h + dlt * mu_ref[2:3, :]).astype(BF16)
    wl = w0_ref[...] + _dot(jnp.tanh(_dot(xw, w1_ref[...])).astype(BF16), w2_ref[...])
    w = -_softplus(-wl) - 0.5
    lw_ref[...] = -jnp.exp(w)
    a_ref[...] = jax.nn.sigmoid(a0_ref[...] + _dot(_dot(xa, a1_ref[...]).astype(BF16), a2_ref[...]))
    gate_ref[...] = _dot(jax.nn.sigmoid(_dot(xg, g1_ref[...])).astype(BF16), g2_ref[...])


def _rwkv_core_kernel(r_ref, k_ref, v_ref, lw_ref, a_ref, gate_ref, kk_ref, ka_ref, rk_ref, lnw_ref, lnb_ref,
                      o_ref, s_ref, lhs_ref, rhs_ref, bk_ref, v2_ref, dec_ref, y_ref):
    ck, hd = RWKV_CHUNK, RWKV_HEAD_DIM
    nb, tc, width = r_ref.shape
    nch, ck2 = tc // ck, 2 * ck
    seqs = [(bi, slice(pi * LANES, (pi + 1) * LANES)) for bi in range(nb) for pi in range(width // LANES)]

    @pl.when(pl.program_id(1) == 0)
    def _():
        s_ref[...] = jnp.zeros_like(s_ref)

    lane = lax.broadcasted_iota(jnp.int32, (1, 1, LANES), 2)
    head_a = lane < hd
    hrow = lax.broadcasted_iota(jnp.int32, (LANES, LANES), 0) // hd
    hcol = lax.broadcasted_iota(jnp.int32, (LANES, LANES), 1) // hd
    head_ones = jnp.where(hrow == hcol, 1.0, 0.0).astype(BF16)
    trow = lax.broadcasted_iota(jnp.int32, (nch, ck2, ck), 1)
    tcol = lax.broadcasted_iota(jnp.int32, (nch, ck2, ck), 2)
    sum_ops = jnp.where((tcol <= trow) | (trow >= ck), 1.0, 0.0).astype(BF16)

    def chunk_sums(x):
        hi, lo = _split2(x)
        bdot = lambda t: lax.dot_general(sum_ops, t, (((2,), (1,)), ((0,), (0,))), preferred_element_type=F32)
        both = bdot(hi) + bdot(lo)
        return both[:, :ck], both[:, ck:]

    def head_sum(x):
        hi, lo = _split2(x)
        return _dot(hi, head_ones) + _dot(lo, head_ones)

    def stack_heads(x):
        xb = x.astype(BF16)
        zero = jnp.zeros_like(xb)
        return jnp.concatenate([jnp.where(head_a, xb, zero), jnp.where(head_a, zero, xb)], axis=1)

    for si, (bi, ls) in enumerate(seqs):
        k_all, a_all = k_ref[bi, :, ls], a_ref[bi, :, ls]
        kk = k_all * kk_ref[:, ls]
        kk = kk * lax.rsqrt(jnp.maximum(head_sum(kk * kk), 1e-24))
        k2 = k_all * (1.0 + (a_all - 1.0) * ka_ref[:, ls])
        lw = lw_ref[bi, :, ls]
        by_chunk = lambda x: x.reshape(nch, ck, LANES)
        cs, tot = chunk_sums(by_chunk(lw))
        gam_inv, gam_rem = jnp.exp(-cs), jnp.exp(tot - cs)
        atm = by_chunk(-kk) * jnp.exp(cs - by_chunk(lw))
        rm = by_chunk(r_ref[bi, :, ls]) * jnp.exp(cs)
        b3, k3 = by_chunk(kk * a_all), by_chunk(k2)
        lhs_ref[si] = jnp.concatenate([stack_heads(atm), stack_heads(rm)], axis=1)
        rhs_ref[si] = jnp.concatenate([stack_heads(b3 * gam_inv), stack_heads(k3 * gam_inv)], axis=1)
        bk_ref[si] = jnp.concatenate([stack_heads(b3 * gam_rem), stack_heads(k3 * gam_rem)], axis=1)
        v2_ref[si] = stack_heads(by_chunk(v_ref[bi, :, ls]))
        dec_ref[si] = jnp.exp(tot[:, 0:1, :])

    row = lax.broadcasted_iota(jnp.int32, (ck2, ck2), 0)
    col = lax.broadcasted_iota(jnp.int32, (ck2, ck2), 1)
    incl = col <= row
    strict = col < row
    eye = jnp.where(row == col, 1.0, 0.0)
    n_seq = len(seqs)
    each = lambda f, *lists: [f(*args) for args in zip(*lists)]

    def chunk_step(ci, carry):
        lhs = [lhs_ref[si, ci] for si in range(n_seq)]
        gram = each(lambda l, si: _dot_nt(l, rhs_ref[si, ci]), lhs, range(n_seq))
        a_ab = each(lambda g: jnp.where(strict, g[:ck2, :ck2], 0.0), gram)
        a_lo = each(lambda g: jnp.concatenate([jnp.where(strict, g[:ck2, ck2:], 0.0),
                                               jnp.where(incl, g[ck2:, ck2:], 0.0)], axis=0).astype(BF16), gram)
        a_rb = each(lambda g: jnp.where(incl, g[ck2:, :ck2], 0.0).astype(BF16), gram)
        inv = each(lambda a: eye + a, a_ab)
        pw = each(lambda a: _dot(a.astype(BF16), a.astype(BF16)), a_ab)
        m = 2
        while 2 * m < ck:
            pwb = each(lambda p: p.astype(BF16), pw)
            both = each(lambda p, t: _dot(jnp.concatenate([p, t.astype(BF16)], axis=0), p), pwb, inv)
            pw = each(lambda z: z[:ck2], both)
            inv = each(lambda t, z: t + z[ck2:], inv, both)
            m *= 2
        inv = each(lambda t, p: t + _dot(t.astype(BF16), p.astype(BF16)), inv, pw)
        v2 = [v2_ref[si, ci] for si in range(n_seq)]
        av = each(_dot, a_lo, v2)
        s = [s_ref[si] for si in range(n_seq)]
        xs = each(lambda l, st: _dot_nt(l, st.astype(BF16)), lhs, s)
        u = each(lambda t, x, w: _dot(t.astype(BF16), (x[:ck2] + w[:ck2]).astype(BF16)).astype(BF16), inv, xs, av)
        y2 = each(lambda x, w, arb, ub: x[ck2:] + w[ck2:] + _dot(arb, ub), xs, av, a_rb, u)
        sl = pl.ds(pl.multiple_of(ci * ck, ck), ck)
        for si, (bi, ls) in enumerate(seqs):
            y_ref[bi, sl, ls] = y2[si][:ck] + y2[si][ck:]
            uv = jnp.concatenate([u[si], v2[si]], axis=0)
            s_ref[si] = s[si] * dec_ref[si, ci] + _dot_tn(uv, bk_ref[si, ci])
        return carry

    lax.fori_loop(0, nch, chunk_step, 0)

    for bi, ls in seqs:
        y = y_ref[bi, :, ls]
        mean = head_sum(y) * (1.0 / hd)
        yc = y - mean
        var = _dot((yc * yc).astype(BF16), head_ones) * (1.0 / hd)
        yn = yc * lax.rsqrt(var + RWKV_GN_EPS) * lnw_ref[:, ls] + lnb_ref[:, ls]
        k2 = k_ref[bi, :, ls] * (1.0 + (a_ref[bi, :, ls] - 1.0) * ka_ref[:, ls])
        bonus = head_sum(r_ref[bi, :, ls] * k2 * rk_ref[:, ls]) * v_ref[bi, :, ls]
        o_ref[bi, :, ls] = ((yn + bonus) * gate_ref[bi, :, ls]).astype(o_ref.dtype)


def _pad_to(a, axis, size):
    pad = [(0, 0)] * a.ndim
    pad[axis] = (0, size - a.shape[axis])
    return jnp.pad(a, pad)


def rwkv_layer(x, g, mu, w_rkv, w0, w1, w2, a0, a1, a2, g1, g2, k_k, k_a, r_k, ln_w, ln_b, w_o, batch, seq):
    t, d = x.shape
    row = lambda p: p.reshape(1, d).astype(F32)
    g2d = g.reshape(1, d)
    tn = _tile(d, 512)
    prev_spec = lambda tm: pl.BlockSpec((8, d), (lambda i, *_: (jnp.maximum(i * (tm // 8) - 1, 0), 0)))
    tm = _tile(seq, 1024)
    rkv = pl.pallas_call(
        functools.partial(_rwkv_proj_kernel, tm=tm, seq=seq),
        out_shape=jax.ShapeDtypeStruct((3, t, d), F32),
        grid=(t // tm, 3, d // tn),
        in_specs=[pl.BlockSpec((tm, d), lambda i, j, n: (i, 0)),
                  prev_spec(tm),
                  pl.BlockSpec((1, d), lambda i, j, n: (0, 0)),
                  pl.BlockSpec((None, 1, d), lambda i, j, n: (j, 0, 0)),
                  pl.BlockSpec((None, d, tn), lambda i, j, n: (j, 0, n))],
        out_specs=pl.BlockSpec((None, tm, tn), lambda i, j, n: (j, i, n)),
        scratch_shapes=[pltpu.VMEM((tm, d), F32), pltpu.VMEM((tm, d), F32), pltpu.VMEM((tm, d), BF16)],
        compiler_params=_cparams("parallel", "arbitrary", "arbitrary"),
        name="rwkv_rkv_proj",
    )(x, x, g2d, mu[:3].reshape(3, 1, d), w_rkv)

    pad_rank = lambda w_a, w_b: (_pad_to(w_a, 1, -(-w_a.shape[1] // LANES) * LANES).astype(BF16),
                                 _pad_to(w_b, 0, -(-w_b.shape[0] // LANES) * LANES).astype(BF16))
    w1p, w2p = pad_rank(w1, w2)
    a1p, a2p = pad_rank(a1, a2)
    g1p, g2p = pad_rank(g1, g2)
    full = lambda a: pl.BlockSpec(a.shape, lambda i: (0,) * a.ndim)
    tm = _tile(seq, 256)
    tok = pl.BlockSpec((tm, d), lambda i: (i, 0))
    lora_in = [x, x, g2d, mu[3:6], row(w0), w1p, w2p, row(a0), a1p, a2p, g1p, g2p]
    lw, a_gate, gate = pl.pallas_call(
        functools.partial(_rwkv_lora_kernel, tm=tm, seq=seq),
        out_shape=(jax.ShapeDtypeStruct((t, d), F32),) * 3,
        grid=(t // tm,),
        in_specs=[tok, prev_spec(tm)] + [full(a) for a in lora_in[2:]],
        out_specs=(tok, tok, tok),
        compiler_params=_cparams("parallel"),
        name="rwkv_lora",
    )(*lora_in)

    tc = _tile(seq, 512)
    wd = _tile(d, 4 * LANES)
    tokc = pl.BlockSpec((batch, tc, wd), lambda p, c: (0, c, p))
    rkvc = lambda which: pl.BlockSpec((None, batch, tc, wd), lambda p, c: (which, 0, c, p))
    par = pl.BlockSpec((1, wd), lambda p, c: (0, p))
    n_seq = batch * (wd // LANES)
    nch = tc // RWKV_CHUNK
    bsd = lambda a: a.reshape(batch, seq, d)
    rkv4 = rkv.reshape(3, batch, seq, d)
    mixed = pl.pallas_call(
        _rwkv_core_kernel,
        out_shape=jax.ShapeDtypeStruct((batch, seq, d), BF16),
        grid=(d // wd, seq // tc),
        in_specs=[rkvc(0), rkvc(1), rkvc(2), tokc, tokc, tokc, par, par, par, par, par],
        out_specs=tokc,
        scratch_shapes=[pltpu.VMEM((n_seq, LANES, LANES), F32)]
        + [pltpu.VMEM((n_seq, nch, 4 * RWKV_CHUNK, LANES), BF16)] * 3
        + [pltpu.VMEM((n_seq, nch, 2 * RWKV_CHUNK, LANES), BF16),
           pltpu.VMEM((n_seq, nch, 1, LANES), F32),
           pltpu.VMEM((batch, tc, wd), F32)],
        compiler_params=_cparams("parallel", "arbitrary"),
        name="rwkv_chunked_state",
    )(rkv4, rkv4, rkv4, bsd(lw), bsd(a_gate), bsd(gate), row(k_k), row(k_a), row(r_k), row(ln_w), row(ln_b))
    return matmul_residual(mixed.reshape(t, d), w_o, x)


def kernel(x, norm_mix, norm_mlp, norm_f, attn_w_qkv, attn_w_o, ssm_w_in, ssm_log_dt, ssm_a_re, ssm_a_im,
           ssm_b_re, ssm_b_im, ssm_c_re, ssm_c_im, ssm_d, ssm_w_out, rwkv_mu, rwkv_w_rkv, rwkv_w0, rwkv_w1,
           rwkv_w2, rwkv_a0, rwkv_a1, rwkv_a2, rwkv_g1, rwkv_g2, rwkv_k_k, rwkv_k_a, rwkv_r_k, rwkv_ln_w,
           rwkv_ln_b, rwkv_w_o, mlp_w1, mlp_w2):
    batch, seq, d = x.shape
    depth = norm_mix.shape[0]
    bf = lambda w: w.astype(BF16)
    h = x.reshape(batch * seq, d)
    ia = ib = ic = 0
    for layer in range(depth):
        kind = layer % 3
        if kind == 0:
            h = attention_layer(h, norm_mix[layer], bf(attn_w_qkv[ia]), bf(attn_w_o[ia]), batch, seq)
            ia += 1
        elif kind == 1:
            h = s5_layer(h, norm_mix[layer], bf(ssm_w_in[ib]), ssm_log_dt[ib], ssm_a_re[ib], ssm_a_im[ib],
                         ssm_b_re[ib], ssm_b_im[ib], ssm_c_re[ib], ssm_c_im[ib], ssm_d[ib],
                         bf(ssm_w_out[ib]), batch, seq)
            ib += 1
        else:
            h = rwkv_layer(h, norm_mix[layer], rwkv_mu[ic], bf(rwkv_w_rkv[ic]), rwkv_w0[ic], rwkv_w1[ic],
                           rwkv_w2[ic], rwkv_a0[ic], rwkv_a1[ic], rwkv_a2[ic], rwkv_g1[ic], rwkv_g2[ic],
                           rwkv_k_k[ic], rwkv_k_a[ic], rwkv_r_k[ic], rwkv_ln_w[ic], rwkv_ln_b[ic],
                           bf(rwkv_w_o[ic]), batch, seq)
            ic += 1
        g_final = norm_f if layer == depth - 1 else None
        h = mlp_residual(h, norm_mlp[layer], bf(mlp_w1[layer]), bf(mlp_w2[layer]), g_final)
    return h.reshape(batch, seq, d)
```

```python
import functools

import jax
import jax.numpy as jnp
from jax import lax
from jax.experimental import pallas as pl
from jax.experimental.pallas import tpu as pltpu

F32 = jnp.float32
BF16 = jnp.bfloat16

NORM_EPS = 1e-5
LANES = 128
VMEM_LIMIT_BYTES = 56 * 2**20
MASK_VALUE = -1e30

ATTN_PATTERNS = ((128, 1), (512, 4), (2048, 16))
ATTN_BLOCK = 128
ATTN_HEAD_DIM = 128
SSM_CH = 16
SSM_CHUNK = 16
RWKV_HEAD_DIM = 64
RWKV_CHUNK = 64
RWKV_GN_EPS = RWKV_HEAD_DIM * 1e-5


def _cparams(*sem):
    return pltpu.CompilerParams(dimension_semantics=sem, vmem_limit_bytes=VMEM_LIMIT_BYTES)


def _tile(n, pref):
    t = min(n, pref)
    while n % t:
        t //= 2
    return t


def _rms(x, g):
    ms = jnp.mean(x * x, axis=-1, keepdims=True)
    return x * lax.rsqrt(ms + NORM_EPS) * g


def _dot(a, b):
    return jnp.dot(a, b, preferred_element_type=F32)


def _dot_nt(a, b):
    return lax.dot_general(a, b, (((1,), (1,)), ((), ())), preferred_element_type=F32)


def _dot_tn(a, b):
    return lax.dot_general(a, b, (((0,), (0,)), ((), ())), preferred_element_type=F32)


def _split2(x):
    hi = x.astype(BF16)
    return hi, (x - hi.astype(F32)).astype(BF16)


def _norm_matmul_strided_kernel(x_ref, g_ref, w_ref, o_ref, h_ref, *, dilation, sub):
    tm = h_ref.shape[0]
    per = sub // dilation

    @pl.when(pl.program_id(1) == 0)
    def _():
        if dilation > 1:
            new = lax.broadcasted_iota(jnp.int32, (sub, sub), 0)
            old = lax.broadcasted_iota(jnp.int32, (sub, sub), 1)
            perm = jnp.where(old == (new % per) * dilation + new // per, 1.0, 0.0).astype(BF16)
        for s in range(tm // sub):
            rows = slice(s * sub, (s + 1) * sub)
            h = _rms(x_ref[rows, :], g_ref[...]).astype(BF16)
            h_ref[rows, :] = _dot(perm, h).astype(BF16) if dilation > 1 else h

    y = _dot(h_ref[...], w_ref[...]).astype(o_ref.dtype)
    for s in range(tm // sub):
        o_ref[:, s * per:(s + 1) * per, :] = y[s * sub:(s + 1) * sub].reshape(dilation, per, y.shape[-1])


def norm_matmul_strided(x, g, w, col0, ncols, dilation, batch, seq, tm=1024, tn=2048, sub=512):
    t, d = x.shape
    tm, tn = _tile(seq, tm), _tile(ncols, tn)
    sub = min(sub, tm)
    assert col0 % tn == 0 and (sub // dilation) % 16 == 0
    nt = seq // tm
    return pl.pallas_call(
        functools.partial(_norm_matmul_strided_kernel, dilation=dilation, sub=sub),
        out_shape=jax.ShapeDtypeStruct((batch, dilation, seq // dilation, ncols), BF16),
        grid=(t // tm, ncols // tn),
        in_specs=[pl.BlockSpec((tm, d), lambda i, j: (i, 0)),
                  pl.BlockSpec((1, d), lambda i, j: (0, 0)),
                  pl.BlockSpec((d, tn), lambda i, j: (0, col0 // tn + j))],
        out_specs=pl.BlockSpec((None, dilation, tm // dilation, tn), lambda i, j: (i // nt, 0, i % nt, j)),
        scratch_shapes=[pltpu.VMEM((tm, d), BF16)],
        compiler_params=_cparams("parallel", "arbitrary"),
        name=f"norm_matmul_stride{dilation}",
    )(x, g.reshape(1, d), w)


def _matmul_res_kernel(a_ref, w_ref, r_ref, o_ref):
    o_ref[...] = r_ref[...] + _dot(a_ref[...], w_ref[...])


def matmul_residual(a, w, res, tm=1024, tn=1024):
    t, k = a.shape
    n = w.shape[1]
    tm, tn = _tile(t, tm), _tile(n, tn)
    return pl.pallas_call(
        _matmul_res_kernel,
        out_shape=jax.ShapeDtypeStruct((t, n), F32),
        grid=(t // tm, n // tn),
        in_specs=[pl.BlockSpec((tm, k), lambda i, j: (i, 0)),
                  pl.BlockSpec((k, tn), lambda i, j: (0, j)),
                  pl.BlockSpec((tm, tn), lambda i, j: (i, j))],
        out_specs=pl.BlockSpec((tm, tn), lambda i, j: (i, j)),
        compiler_params=_cparams("parallel", "parallel"),
        name="matmul_residual",
    )(a, w, res)


def _mlp_kernel(x_ref, g_ref, w1_ref, w2_ref, gf_ref, o_ref, h_ref, *, final_norm):
    f = pl.program_id(1)

    @pl.when(f == 0)
    def _():
        x = x_ref[...]
        h_ref[...] = _rms(x, g_ref[...]).astype(BF16)
        o_ref[...] = x

    a = _dot(h_ref[...], w1_ref[...])
    a = jnp.square(jnp.maximum(a, 0.0)).astype(BF16)
    o_ref[...] += _dot(a, w2_ref[...])

    if final_norm:
        @pl.when(f == pl.num_programs(1) - 1)
        def _():
            o_ref[...] = _rms(o_ref[...], gf_ref[...])


def mlp_residual(x, g, w1, w2, g_final=None, tm=1024, tf=512):
    t, d = x.shape
    ff = w1.shape[1]
    tm, tf = _tile(t, tm), _tile(ff, tf)
    final_norm = g_final is not None
    gf = (g_final if final_norm else g).reshape(1, d)
    return pl.pallas_call(
        functools.partial(_mlp_kernel, final_norm=final_norm),
        out_shape=jax.ShapeDtypeStruct((t, d), F32),
        grid=(t // tm, ff // tf),
        in_specs=[pl.BlockSpec((tm, d), lambda i, f: (i, 0)),
                  pl.BlockSpec((1, d), lambda i, f: (0, 0)),
                  pl.BlockSpec((d, tf), lambda i, f: (0, f)),
                  pl.BlockSpec((tf, d), lambda i, f: (f, 0)),
                  pl.BlockSpec((1, d), lambda i, f: (0, 0))],
        out_specs=pl.BlockSpec((tm, d), lambda i, f: (i, 0)),
        scratch_shapes=[pltpu.VMEM((tm, d), BF16)],
        compiler_params=_cparams("parallel", "arbitrary"),
        name="mlp_residual",
    )(x, g.reshape(1, d), w1, w2, gf)


def _attn_kernel(slope_ref, qkv_ref, o_ref, lse_ref, kvp_ref, *, heads, scale):
    blk, e = ATTN_BLOCK, ATTN_HEAD_DIM
    he = heads * e
    j = pl.program_id(2)

    @pl.when(j == 0)
    def _():
        kvp_ref[...] = jnp.zeros_like(kvp_ref)

    qi = lax.broadcasted_iota(jnp.int32, (blk, blk), 0)
    kj = lax.broadcasted_iota(jnp.int32, (blk, blk), 1)
    log2e, ln2 = 1.4426950408889634, 0.6931471805599453
    dist_c = (qi - kj).astype(F32) * log2e
    dist_p = dist_c + float(blk) * log2e
    mask_c = jnp.where(kj <= qi, 0.0, MASK_VALUE)
    mask_p = jnp.where(kj >= qi, 0.0, MASK_VALUE)
    mask_first = jnp.where((kj >= qi) & (j > 0), 0.0, MASK_VALUE)
    lane = lax.broadcasted_iota(jnp.int32, (blk, LANES), 1)
    together = next(n for n in (8, 4, 2, 1) if heads % n == 0)
    for sb in range(qkv_ref.shape[0] // blk):
        rows = slice(sb * blk, (sb + 1) * blk)
        q_ref, kc_ref, vc_ref = (qkv_ref.at[rows, c * he:(c + 1) * he] for c in range(3))
        if sb == 0:
            kp_ref, vp_ref, mask_prev = kvp_ref.at[:, 0:he], kvp_ref.at[:, he:2 * he], mask_first
        else:
            before = slice((sb - 1) * blk, sb * blk)
            kp_ref, vp_ref, mask_prev = qkv_ref.at[before, he:2 * he], qkv_ref.at[before, 2 * he:3 * he], mask_p
        lse_tile = jnp.zeros((blk, LANES), F32)
        outs = []
        for h0 in range(0, heads, together):
            hs = list(range(h0, h0 + together))
            sls = [slice(h * e, (h + 1) * e) for h in hs]
            qs = [q_ref[:, sl] for sl in sls]
            sc = [_dot_nt(q, kc_ref[:, sl]) for q, sl in zip(qs, sls)]
            sp = [_dot_nt(q, kp_ref[:, sl]) for q, sl in zip(qs, sls)]
            sc = [s * (scale * log2e) + (mask_c - slope_ref[h] * dist_c) for s, h in zip(sc, hs)]
            sp = [s * (scale * log2e) + (mask_prev - slope_ref[h] * dist_p) for s, h in zip(sp, hs)]
            m = [jnp.max(jnp.maximum(a, b), axis=-1, keepdims=True) for a, b in zip(sc, sp)]
            pc = [jnp.exp2(a - mm) for a, mm in zip(sc, m)]
            pp = [jnp.exp2(b - mm) for b, mm in zip(sp, m)]
            den = [jnp.sum(a + b, axis=-1, keepdims=True) for a, b in zip(pc, pp)]
            o = [_dot(a.astype(BF16), vc_ref[:, sl]) + _dot(b.astype(BF16), vp_ref[:, sl])
                 for a, b, sl in zip(pc, pp, sls)]
            for h, oo, dd, mm in zip(hs, o, den, m):
                outs.append((oo / dd).astype(o_ref.dtype))
                lse_tile = jnp.where(lane == h, mm * ln2 + jnp.log(dd), lse_tile)
        for h, oo in enumerate(outs):
            o_ref[rows, h * e:(h + 1) * e] = oo
        lse_ref[rows, :] = lse_tile
    kvp_ref[...] = qkv_ref[qkv_ref.shape[0] - blk:, he:3 * he]


def _attn_group(qkv, slopes, group, dilation, batch, seq, heads):
    e, blk = ATTN_HEAD_DIM, ATTN_BLOCK
    he = heads * e
    sub = seq // dilation
    rows = _tile(sub, 4 * blk)
    out, lse = pl.pallas_call(
        functools.partial(_attn_kernel, heads=heads, scale=e ** -0.5),
        out_shape=(jax.ShapeDtypeStruct((batch, dilation, sub, he), BF16),
                   jax.ShapeDtypeStruct((batch, dilation, sub, LANES), F32)),
        grid=(batch, dilation, sub // rows),
        in_specs=[pl.BlockSpec(memory_space=pltpu.SMEM),
                  pl.BlockSpec((None, None, rows, 3 * he), lambda b, r, j: (b, r, j, 0))],
        out_specs=(pl.BlockSpec((None, None, rows, he), lambda b, r, j: (b, r, j, 0)),
                   pl.BlockSpec((None, None, rows, LANES), lambda b, r, j: (b, r, j, 0))),
        scratch_shapes=[pltpu.VMEM((blk, 2 * he), BF16)],
        compiler_params=_cparams("arbitrary", "arbitrary", "arbitrary"),
        name=f"dilated_attn_g{group}",
    )(slopes, qkv)
    natural = lambda a: a.transpose(0, 2, 1, 3).reshape(batch * seq, a.shape[-1])
    return natural(out), natural(lse)


def _attn_out_kernel(o0_ref, o1_ref, o2_ref, l0_ref, l1_ref, l2_ref, w_ref, r_ref, out_ref, m_ref, *, heads):
    e = ATTN_HEAD_DIM

    @pl.when(pl.program_id(1) == 0)
    def _():
        l0, l1, l2 = l0_ref[...], l1_ref[...], l2_ref[...]
        mx = jnp.maximum(jnp.maximum(l0, l1), l2)
        e0, e1, e2 = jnp.exp(l0 - mx), jnp.exp(l1 - mx), jnp.exp(l2 - mx)
        inv = 1.0 / (e0 + e1 + e2)
        src = lax.broadcasted_iota(jnp.int32, (LANES, heads * e), 0)
        dst = lax.broadcasted_iota(jnp.int32, (LANES, heads * e), 1)
        spread = jnp.where(src == dst // e, 1.0, 0.0).astype(BF16)

        per_lane = lambda w: _dot(w.astype(BF16), spread)

        acc = per_lane(e0 * inv) * o0_ref[...].astype(F32)
        acc += per_lane(e1 * inv) * o1_ref[...].astype(F32)
        acc += per_lane(e2 * inv) * o2_ref[...].astype(F32)
        m_ref[...] = acc.astype(BF16)

    out_ref[...] = r_ref[...] + _dot(m_ref[...], w_ref[...])


def attention_layer(x, g, w_qkv, w_o, batch, seq):
    t, d = x.shape
    n_dil = len(ATTN_PATTERNS)
    he = w_o.shape[0]
    heads = he // ATTN_HEAD_DIM
    n_sl = n_dil * heads
    slopes = (2.0 ** (-8.0 * jnp.arange(1, n_sl + 1, dtype=F32) / n_sl)).reshape(n_dil, heads)
    outs, lses = [], []
    for grp, (window, dilation) in enumerate(ATTN_PATTERNS):
        assert window // dilation == ATTN_BLOCK and (seq // dilation) % ATTN_BLOCK == 0
        qkv = norm_matmul_strided(x, g, w_qkv, grp * 3 * he, 3 * he, dilation, batch, seq)
        o, l = _attn_group(qkv, slopes[grp] * dilation, grp, dilation, batch, seq, heads)
        outs.append(o)
        lses.append(l)
    tm, tn = _tile(t, 512), _tile(d, 1024)
    ospec = pl.BlockSpec((tm, he), lambda i, j: (i, 0))
    lspec = pl.BlockSpec((tm, LANES), lambda i, j: (i, 0))
    return pl.pallas_call(
        functools.partial(_attn_out_kernel, heads=heads),
        out_shape=jax.ShapeDtypeStruct((t, d), F32),
        grid=(t // tm, d // tn),
        in_specs=[ospec, ospec, ospec, lspec, lspec, lspec,
                  pl.BlockSpec((he, tn), lambda i, j: (0, j)),
                  pl.BlockSpec((tm, tn), lambda i, j: (i, j))],
        out_specs=pl.BlockSpec((tm, tn), lambda i, j: (i, j)),
        scratch_shapes=[pltpu.VMEM((tm, he), BF16)],
        compiler_params=_cparams("parallel", "arbitrary"),
        name="attn_merge_out_proj",
    )(*outs, *lses, w_o, x)


def _s5_chunk_operators(log_dt, a_re, a_im, b_re, b_im, c_re, c_im, d_skip, chunk):
    n_g, n_p = a_re.shape
    n_c = b_re.shape[-1]
    a_re, a_im = a_re.T, a_im.T
    b_re, b_im = b_re.transpose(1, 2, 0), b_im.transpose(1, 2, 0)
    c_re, c_im = c_re.transpose(1, 2, 0), c_im.transpose(1, 2, 0)
    dt = jnp.exp(log_dt)[None, :]
    mag = jnp.exp(dt * a_re)
    ab_re = mag * jnp.cos(dt * a_im)
    ab_im = mag * jnp.sin(dt * a_im)
    den = a_re * a_re + a_im * a_im
    zr = ab_re - 1.0
    cr = (zr * a_re + ab_im * a_im) / den
    ci = (ab_im * a_re - zr * a_im) / den
    bb_re = cr[:, None] * b_re - ci[:, None] * b_im
    bb_im = cr[:, None] * b_im + ci[:, None] * b_re
    pr, pi = [jnp.ones_like(ab_re)], [jnp.zeros_like(ab_re)]
    for _ in range(chunk):
        pr, pi = pr + [pr[-1] * ab_re - pi[-1] * ab_im], pi + [pr[-1] * ab_im + pi[-1] * ab_re]
    pr, pi = jnp.stack(pr), jnp.stack(pi)
    ce_re = c_re[None] * pr[:, None] - c_im[None] * pi[:, None]
    ce_im = c_re[None] * pi[:, None] + c_im[None] * pr[:, None]
    kern = jnp.sum(ce_re[:chunk, :, :, None] * bb_re[None, None] - ce_im[:chunk, :, :, None] * bb_im[None, None],
                   axis=2)
    kern = kern.at[0].add(jnp.eye(n_c, dtype=F32)[:, :, None] * d_skip.T[:, None, :])
    group_first = lambda a, *shape: a.astype(BF16).reshape(-1, n_g).T.reshape((n_g,) + shape)
    toep = jnp.stack([jnp.pad(kern[:chunk - l], ((l, 0), (0, 0), (0, 0), (0, 0))) for l in range(chunk)])
    m_op = group_first(toep.transpose(0, 3, 1, 2, 4), chunk * n_c, chunk * n_c)
    qr, qi = pr[chunk - 1::-1][:chunk, :, None], pi[chunk - 1::-1][:chunk, :, None]
    bo_re = (qr * bb_re[None] - qi * bb_im[None]).transpose(0, 2, 1, 3)
    bo_im = (qr * bb_im[None] + qi * bb_re[None]).transpose(0, 2, 1, 3)
    b_op = group_first(jnp.concatenate([bo_re, bo_im, bo_im, bo_re], axis=2), chunk * n_c, 4 * n_p)
    by_state = lambda a: a.transpose(2, 0, 1, 3)
    c_op = group_first(jnp.concatenate([by_state(ce_re[1:]), -by_state(ce_im[1:])], axis=0), 2 * n_p, chunk * n_c)
    gpb = LANES // n_c
    nblk = n_g // gpb
    al_re, al_im = pr[chunk].T, pi[chunk].T
    per_blk = lambda parts: jnp.concatenate(parts, axis=-1).reshape(nblk, 1, gpb * 2 * n_p)
    coef_same = jnp.concatenate([per_blk([al_re, al_re])] * 2, axis=-1)
    coef_cross = jnp.concatenate([per_blk([-al_im, al_im]), per_blk([al_im, -al_im])], axis=-1)
    return m_op, b_op, c_op, coef_same, coef_cross


def _chunk_rows(u_ref):
    return jnp.concatenate([u_ref[l] for l in range(u_ref.shape[0])], axis=-1)


def _first_visit_of_block():
    return (pl.program_id(1) == 0) & (pl.program_id(2) == 0)


def _s5_in_kernel(u_ref, b_ref, o_ref, dense_ref, *, n_c):
    @pl.when(_first_visit_of_block())
    def _():
        gpb, rows, cols = b_ref.shape
        half = cols // 2
        dense_ref[...] = jnp.zeros_like(dense_ref)
        for gi in range(gpb):
            for l in range(rows // n_c):
                r0 = l * LANES + gi * n_c
                piece = b_ref[gi, l * n_c:(l + 1) * n_c, :]
                dense_ref[r0:r0 + n_c, gi * half:(gi + 1) * half] = piece[:, :half]
                dense_ref[r0:r0 + n_c, (gpb + gi) * half:(gpb + gi + 1) * half] = piece[:, half:]

    o_ref[...] = _dot(_chunk_rows(u_ref), dense_ref[...])


def _s5_scan_kernel(xin_ref, cs_ref, cc_ref, o_ref, st_ref, *, half):
    @pl.when(pl.program_id(2) == 0)
    def _():
        st_ref[...] = jnp.zeros_like(st_ref)

    cs, cc = cs_ref[...], cc_ref[...]

    def step(n, st):
        o_ref[pl.ds(n, 1), :] = st[:, :half]
        st_sw = jnp.concatenate([st[:, half:], st[:, :half]], axis=-1)
        return cs * st + cc * st_sw + xin_ref[pl.ds(n, 1), :]

    st_ref[...] = lax.fori_loop(0, xin_ref.shape[0], step, st_ref[...])


def _s5_out_kernel(u_ref, m_ref, xp_ref, c_ref, o_ref, mdense_ref, cdense_ref, *, n_c):
    @pl.when(_first_visit_of_block())
    def _():
        gpb, rows, cols = m_ref.shape
        n_q = c_ref.shape[1]
        src = lax.broadcasted_iota(jnp.int32, (cols, mdense_ref.shape[1]), 0)
        dst = lax.broadcasted_iota(jnp.int32, (cols, mdense_ref.shape[1]), 1)
        for gi in range(gpb):
            spread = jnp.where(dst == (src // n_c) * LANES + gi * n_c + src % n_c, 1.0, 0.0).astype(BF16)
            wide = _dot(m_ref[gi], spread).astype(BF16)
            for l in range(rows // n_c):
                r0 = l * LANES + gi * n_c
                mdense_ref[r0:r0 + n_c, :] = wide[l * n_c:(l + 1) * n_c, :]
            cdense_ref[gi * n_q:(gi + 1) * n_q, :] = _dot(c_ref[gi], spread).astype(BF16)

    y = _dot(_chunk_rows(u_ref), mdense_ref[...]) + _dot(xp_ref[...].astype(BF16), cdense_ref[...])
    y = jax.nn.gelu(y).astype(o_ref.dtype)
    for step in range(o_ref.shape[0]):
        o_ref[step] = y[:, step * LANES:(step + 1) * LANES]


def _glu_out_kernel(y_ref, wa_ref, wb_ref, r_ref, o_ref, yn_ref, *, dilation, sub):
    @pl.when(pl.program_id(1) == 0)
    def _():
        tm, kdim = yn_ref.shape
        per = sub // dilation
        nat = lax.broadcasted_iota(jnp.int32, (sub, sub), 0)
        src = lax.broadcasted_iota(jnp.int32, (sub, sub), 1)
        perm = jnp.where(src == (nat % dilation) * per + nat // dilation, 1.0, 0.0).astype(BF16)
        for s in range(tm // sub):
            grouped = y_ref[:, s * per:(s + 1) * per, :].reshape(sub, kdim)
            yn_ref[s * sub:(s + 1) * sub, :] = _dot(perm, grouped).astype(BF16)

    y = yn_ref[...]
    o_ref[...] = r_ref[...] + _dot(y, wa_ref[...]) * jax.nn.sigmoid(_dot(y, wb_ref[...]))


def s5_layer(x, g, w_in, log_dt, a_re, a_im, b_re, b_im, c_re, c_im, d_skip, w_out, batch, seq):
    t, d = x.shape
    n_g, n_p = a_re.shape
    n_c = SSM_CH
    ck = SSM_CHUNK
    gc = n_g * n_c
    nblk = gc // LANES
    n_chunks = seq // ck
    sw = 4 * n_p * (LANES // n_c)
    m_op, b_op, c_op, coef_same, coef_cross = _s5_chunk_operators(
        log_dt, a_re, a_im, b_re, b_im, c_re, c_im, d_skip, ck)
    u = norm_matmul_strided(x, g, w_in, 0, gc, ck, batch, seq)
    tr = _tile(n_chunks, 512)
    u_spec = pl.BlockSpec((None, ck, tr, LANES), lambda k, b, n: (b, 0, n, k))
    gpb = LANES // n_c
    group_ops = lambda a: pl.BlockSpec((gpb,) + a.shape[1:], lambda k, b, n: (k, 0, 0))
    xin = pl.pallas_call(
        functools.partial(_s5_in_kernel, n_c=n_c),
        out_shape=jax.ShapeDtypeStruct((batch, n_chunks, nblk * sw), F32),
        grid=(nblk, batch, n_chunks // tr),
        in_specs=[u_spec, group_ops(b_op)],
        out_specs=pl.BlockSpec((None, tr, sw), lambda k, b, n: (b, n, k)),
        scratch_shapes=[pltpu.VMEM((ck * LANES, sw), BF16)],
        compiler_params=_cparams("arbitrary", "arbitrary", "arbitrary"),
        name="s5_chunk_inputs",
    )(u, b_op)
    coef_spec = pl.BlockSpec((None, 1, sw), lambda b, k, n: (k, 0, 0))
    xprev = pl.pallas_call(
        functools.partial(_s5_scan_kernel, half=sw // 2),
        out_shape=jax.ShapeDtypeStruct((batch, n_chunks, nblk * sw // 2), F32),
        grid=(batch, nblk, n_chunks // tr),
        in_specs=[pl.BlockSpec((None, tr, sw), lambda b, k, n: (b, n, k)), coef_spec, coef_spec],
        out_specs=pl.BlockSpec((None, tr, sw // 2), lambda b, k, n: (b, n, k)),
        scratch_shapes=[pltpu.VMEM((1, sw), F32)],
        compiler_params=_cparams("parallel", "parallel", "arbitrary"),
        name="s5_chunk_scan",
    )(xin, coef_same, coef_cross)
    y = pl.pallas_call(
        functools.partial(_s5_out_kernel, n_c=n_c),
        out_shape=jax.ShapeDtypeStruct((batch, ck, n_chunks, gc), BF16),
        grid=(nblk, batch, n_chunks // tr),
        in_specs=[u_spec, group_ops(m_op),
                  pl.BlockSpec((None, tr, sw // 2), lambda k, b, n: (b, n, k)), group_ops(c_op)],
        out_specs=u_spec,
        scratch_shapes=[pltpu.VMEM((ck * LANES, ck * LANES), BF16), pltpu.VMEM((sw // 2, ck * LANES), BF16)],
        compiler_params=_cparams("arbitrary", "arbitrary", "arbitrary"),
        name="s5_chunk_outputs",
    )(u, m_op, xprev, c_op)
    tm, tn2 = _tile(seq, 1024), _tile(d, 512)
    nt, nj = seq // tm, d // tn2
    return pl.pallas_call(
        functools.partial(_glu_out_kernel, dilation=ck, sub=min(tm, 512)),
        out_shape=jax.ShapeDtypeStruct((t, d), F32),
        grid=(t // tm, nj),
        in_specs=[pl.BlockSpec((None, ck, tm // ck, gc), lambda i, j: (i // nt, 0, i % nt, 0)),
                  pl.BlockSpec((gc, tn2), lambda i, j: (0, j)),
                  pl.BlockSpec((gc, tn2), lambda i, j: (0, j + nj)),
                  pl.BlockSpec((tm, tn2), lambda i, j: (i, j))],
        out_specs=pl.BlockSpec((tm, tn2), lambda i, j: (i, j)),
        scratch_shapes=[pltpu.VMEM((tm, gc), BF16)],
        compiler_params=_cparams("parallel", "arbitrary"),
        name="s5_glu_out_proj",
    )(y, w_out, w_out, x)


def _shift_norm(x_ref, xp_ref, g_ref, first):
    g = g_ref[...]
    h = _rms(x_ref[...], g)
    prev = _rms(xp_ref[7:8, :], g)
    prev = jnp.where(first, 0.0, prev)
    row = lax.broadcasted_iota(jnp.int32, h.shape, 0)
    hp = jnp.where(row == 0, prev, pltpu.roll(h, 1, 0))
    return h, hp


def _rwkv_proj_kernel(x_ref, xp_ref, g_ref, mu_ref, w_ref, o_ref, h_ref, d_ref, l_ref, *, tm, seq):
    i, j, n = pl.program_id(0), pl.program_id(1), pl.program_id(2)

    @pl.when((j == 0) & (n == 0))
    def _():
        h, hp = _shift_norm(x_ref, xp_ref, g_ref, (i * tm) % seq == 0)
        h_ref[...] = h
        d_ref[...] = hp - h

    @pl.when(n == 0)
    def _():
        l_ref[...] = (h_ref[...] + d_ref[...] * mu_ref[...]).astype(BF16)

    o_ref[...] = _dot(l_ref[...], w_ref[...])


def _softplus(z):
    return jnp.maximum(z, 0.0) + jnp.log(1.0 + jnp.exp(-jnp.abs(z)))


def _rwkv_lora_kernel(x_ref, xp_ref, g_ref, mu_ref, w0_ref, w1_ref, w2_ref, a0_ref, a1_ref, a2_ref,
                      g1_ref, g2_ref, lw_ref, a_ref, gate_ref, *, tm, seq):
    h, hp = _shift_norm(x_ref, xp_ref, g_ref, (pl.program_id(0) * tm) % seq == 0)
    dlt = hp - h
    xw = (h + dlt * mu_ref[0:1, :]).astype(BF16)
    xa = (h + dlt * mu_ref[1:2, :]).astype(BF16)
    xg = (h + dlt * mu_ref[2:3, :]).astype(BF16)
    wl = w0_ref[...] + _dot(jnp.tanh(_dot(xw, w1_ref[...])).astype(BF16), w2_ref[...])
    w = -_softplus(-wl) - 0.5
    lw_ref[...] = -jnp.exp(w)
    a_ref[...] = jax.nn.sigmoid(a0_ref[...] + _dot(_dot(xa, a1_ref[...]).astype(BF16), a2_ref[...]))
    gate_ref[...] = _dot(jax.nn.sigmoid(_dot(xg, g1_ref[...])).astype(BF16), g2_ref[...])


def _rwkv_core_kernel(r_ref, k_ref, v_ref, lw_ref, a_ref, gate_ref, kk_ref, ka_ref, rk_ref, lnw_ref, lnb_ref,
                      o_ref, s_ref, lhs_ref, rhs_ref, bk_ref, v2_ref, dec_ref, y_ref):
    ck, hd = RWKV_CHUNK, RWKV_HEAD_DIM
    nb, tc, width = r_ref.shape
    nch, ck2 = tc // ck, 2 * ck
    seqs = [(bi, slice(pi * LANES, (pi + 1) * LANES)) for bi in range(nb) for pi in range(width // LANES)]

    @pl.when(pl.program_id(1) == 0)
    def _():
        s_ref[...] = jnp.zeros_like(s_ref)

    lane = lax.broadcasted_iota(jnp.int32, (1, 1, LANES), 2)
    head_a = lane < hd
    hrow = lax.broadcasted_iota(jnp.int32, (LANES, LANES), 0) // hd
    hcol = lax.broadcasted_iota(jnp.int32, (LANES, LANES), 1) // hd
    head_ones = jnp.where(hrow == hcol, 1.0, 0.0).astype(BF16)
    trow = lax.broadcasted_iota(jnp.int32, (nch, ck2, ck), 1)
    tcol = lax.broadcasted_iota(jnp.int32, (nch, ck2, ck), 2)
    sum_ops = jnp.where((tcol <= trow) | (trow >= ck), 1.0, 0.0).astype(BF16)

    def chunk_sums(x):
        hi, lo = _split2(x)
        bdot = lambda t: lax.dot_general(sum_ops, t, (((2,), (1,)), ((0,), (0,))), preferred_element_type=F32)
        both = bdot(hi) + bdot(lo)
        return both[:, :ck], both[:, ck:]

    def head_sum(x):
        hi, lo = _split2(x)
        return _dot(hi, head_ones) + _dot(lo, head_ones)

    def stack_heads(x):
        xb = x.astype(BF16)
        zero = jnp.zeros_like(xb)
        return jnp.concatenate([jnp.where(head_a, xb, zero), jnp.where(head_a, zero, xb)], axis=1)

    for si, (bi, ls) in enumerate(seqs):
        k_all, a_all = k_ref[bi, :, ls], a_ref[bi, :, ls]
        kk = k_all * kk_ref[:, ls]
        kk = kk * lax.rsqrt(jnp.maximum(head_sum(kk * kk), 1e-24))
        k2 = k_all * (1.0 + (a_all - 1.0) * ka_ref[:, ls])
        lw = lw_ref[bi, :, ls]
        by_chunk = lambda x: x.reshape(nch, ck, LANES)
        cs, tot = chunk_sums(by_chunk(lw))
        gam_inv, gam_rem = jnp.exp(-cs), jnp.exp(tot - cs)
        atm = by_chunk(-kk) * jnp.exp(cs - by_chunk(lw))
        rm = by_chunk(r_ref[bi, :, ls]) * jnp.exp(cs)
        b3, k3 = by_chunk(kk * a_all), by_chunk(k2)
        lhs_ref[si] = jnp.concatenate([stack_heads(atm), stack_heads(rm)], axis=1)
        rhs_ref[si] = jnp.concatenate([stack_heads(b3 * gam_inv), stack_heads(k3 * gam_inv)], axis=1)
        bk_ref[si] = jnp.concatenate([stack_heads(b3 * gam_rem), stack_heads(k3 * gam_rem)], axis=1)
        v2_ref[si] = stack_heads(by_chunk(v_ref[bi, :, ls]))
        dec_ref[si] = jnp.exp(tot[:, 0:1, :])

    row = lax.broadcasted_iota(jnp.int32, (ck2, ck2), 0)
    col = lax.broadcasted_iota(jnp.int32, (ck2, ck2), 1)
    incl = col <= row
    strict = col < row
    eye = jnp.where(row == col, 1.0, 0.0)
    n_seq = len(seqs)
    each = lambda f, *lists: [f(*args) for args in zip(*lists)]

    def chunk_step(ci, carry):
        lhs = [lhs_ref[si, ci] for si in range(n_seq)]
        gram = each(lambda l, si: _dot_nt(l, rhs_ref[si, ci]), lhs, range(n_seq))
        a_ab = each(lambda g: jnp.where(strict, g[:ck2, :ck2], 0.0), gram)
        a_lo = each(lambda g: jnp.concatenate([jnp.where(strict, g[:ck2, ck2:], 0.0),
                                               jnp.where(incl, g[ck2:, ck2:], 0.0)], axis=0).astype(BF16), gram)
        a_rb = each(lambda g: jnp.where(incl, g[ck2:, :ck2], 0.0).astype(BF16), gram)
        inv = each(lambda a: eye + a, a_ab)
        pw = each(lambda a: _dot(a.astype(BF16), a.astype(BF16)), a_ab)
        m = 2
        while 2 * m < ck:
            pwb = each(lambda p: p.astype(BF16), pw)
            both = each(lambda p, t: _dot(jnp.concatenate([p, t.astype(BF16)], axis=0), p), pwb, inv)
            pw = each(lambda z: z[:ck2], both)
            inv = each(lambda t, z: t + z[ck2:], inv, both)
            m *= 2
        inv = each(lambda t, p: t + _dot(t.astype(BF16), p.astype(BF16)), inv, pw)
        v2 = [v2_ref[si, ci] for si in range(n_seq)]
        av = each(_dot, a_lo, v2)
        s = [s_ref[si] for si in range(n_seq)]
        xs = each(lambda l, st: _dot_nt(l, st.astype(BF16)), lhs, s)
        u = each(lambda t, x, w: _dot(t.astype(BF16), (x[:ck2] + w[:ck2]).astype(BF16)).astype(BF16), inv, xs, av)
        y2 = each(lambda x, w, arb, ub: x[ck2:] + w[ck2:] + _dot(arb, ub), xs, av, a_rb, u)
        sl = pl.ds(pl.multiple_of(ci * ck, ck), ck)
        for si, (bi, ls) in enumerate(seqs):
            y_ref[bi, sl, ls] = y2[si][:ck] + y2[si][ck:]
            uv = jnp.concatenate([u[si], v2[si]], axis=0)
            s_ref[si] = s[si] * dec_ref[si, ci] + _dot_tn(uv, bk_ref[si, ci])
        return carry

    lax.fori_loop(0, nch, chunk_step, 0)

    for bi, ls in seqs:
        y = y_ref[bi, :, ls]
        mean = head_sum(y) * (1.0 / hd)
        yc = y - mean
        var = _dot((yc * yc).astype(BF16), head_ones) * (1.0 / hd)
        yn = yc * lax.rsqrt(var + RWKV_GN_EPS) * lnw_ref[:, ls] + lnb_ref[:, ls]
        k2 = k_ref[bi, :, ls] * (1.0 + (a_ref[bi, :, ls] - 1.0) * ka_ref[:, ls])
        bonus = head_sum(r_ref[bi, :, ls] * k2 * rk_ref[:, ls]) * v_ref[bi, :, ls]
        o_ref[bi, :, ls] = ((yn + bonus) * gate_ref[bi, :, ls]).astype(o_ref.dtype)


def _pad_to(a, axis, size):
    pad = [(0, 0)] * a.ndim
    pad[axis] = (0, size - a.shape[axis])
    return jnp.pad(a, pad)


def rwkv_layer(x, g, mu, w_rkv, w0, w1, w2, a0, a1, a2, g1, g2, k_k, k_a, r_k, ln_w, ln_b, w_o, batch, seq):
    t, d = x.shape
    row = lambda p: p.reshape(1, d).astype(F32)
    g2d = g.reshape(1, d)
    tn = _tile(d, 512)
    prev_spec = lambda tm: pl.BlockSpec((8, d), (lambda i, *_: (jnp.maximum(i * (tm // 8) - 1, 0), 0)))
    tm = _tile(seq, 1024)
    rkv = pl.pallas_call(
        functools.partial(_rwkv_proj_kernel, tm=tm, seq=seq),
        out_shape=jax.ShapeDtypeStruct((3, t, d), F32),
        grid=(t // tm, 3, d // tn),
        in_specs=[pl.BlockSpec((tm, d), lambda i, j, n: (i, 0)),
                  prev_spec(tm),
                  pl.BlockSpec((1, d), lambda i, j, n: (0, 0)),
                  pl.BlockSpec((None, 1, d), lambda i, j, n: (j, 0, 0)),
                  pl.BlockSpec((None, d, tn), lambda i, j, n: (j, 0, n))],
        out_specs=pl.BlockSpec((None, tm, tn), lambda i, j, n: (j, i, n)),
        scratch_shapes=[pltpu.VMEM((tm, d), F32), pltpu.VMEM((tm, d), F32), pltpu.VMEM((tm, d), BF16)],
        compiler_params=_cparams("parallel", "arbitrary", "arbitrary"),
        name="rwkv_rkv_proj",
    )(x, x, g2d, mu[:3].reshape(3, 1, d), w_rkv)

    pad_rank = lambda w_a, w_b: (_pad_to(w_a, 1, -(-w_a.shape[1] // LANES) * LANES).astype(BF16),
                                 _pad_to(w_b, 0, -(-w_b.shape[0] // LANES) * LANES).astype(BF16))
    w1p, w2p = pad_rank(w1, w2)
    a1p, a2p = pad_rank(a1, a2)
    g1p, g2p = pad_rank(g1, g2)
    full = lambda a: pl.BlockSpec(a.shape, lambda i: (0,) * a.ndim)
    tm = _tile(seq, 256)
    tok = pl.BlockSpec((tm, d), lambda i: (i, 0))
    lora_in = [x, x, g2d, mu[3:6], row(w0), w1p, w2p, row(a0), a1p, a2p, g1p, g2p]
    lw, a_gate, gate = pl.pallas_call(
        functools.partial(_rwkv_lora_kernel, tm=tm, seq=seq),
        out_shape=(jax.ShapeDtypeStruct((t, d), F32),) * 3,
        grid=(t // tm,),
        in_specs=[tok, prev_spec(tm)] + [full(a) for a in lora_in[2:]],
        out_specs=(tok, tok, tok),
        compiler_params=_cparams("parallel"),
        name="rwkv_lora",
    )(*lora_in)

    tc = _tile(seq, 512)
    wd = _tile(d, 4 * LANES)
    tokc = pl.BlockSpec((batch, tc, wd), lambda p, c: (0, c, p))
    rkvc = lambda which: pl.BlockSpec((None, batch, tc, wd), lambda p, c: (which, 0, c, p))
    par = pl.BlockSpec((1, wd), lambda p, c: (0, p))
    n_seq = batch * (wd // LANES)
    nch = tc // RWKV_CHUNK
    bsd = lambda a: a.reshape(batch, seq, d)
    rkv4 = rkv.reshape(3, batch, seq, d)
    mixed = pl.pallas_call(
        _rwkv_core_kernel,
        out_shape=jax.ShapeDtypeStruct((batch, seq, d), BF16),
        grid=(d // wd, seq // tc),
        in_specs=[rkvc(0), rkvc(1), rkvc(2), tokc, tokc, tokc, par, par, par, par, par],
        out_specs=tokc,
        scratch_shapes=[pltpu.VMEM((n_seq, LANES, LANES), F32)]
        + [pltpu.VMEM((n_seq, nch, 4 * RWKV_CHUNK, LANES), BF16)] * 3
        + [pltpu.VMEM((n_seq, nch, 2 * RWKV_CHUNK, LANES), BF16),
           pltpu.VMEM((n_seq, nch, 1, LANES), F32),
           pltpu.VMEM((batch, tc, wd), F32)],
        compiler_params=_cparams("parallel", "arbitrary"),
        name="rwkv_chunked_state",
    )(rkv4, rkv4, rkv4, bsd(lw), bsd(a_gate), bsd(gate), row(k_k), row(k_a), row(r_k), row(ln_w), row(ln_b))
    return matmul_residual(mixed.reshape(t, d), w_o, x)


def kernel(x, norm_mix, norm_mlp, norm_f, attn_w_qkv, attn_w_o, ssm_w_in, ssm_log_dt, ssm_a_re, ssm_a_im,
           ssm_b_re, ssm_b_im, ssm_c_re, ssm_c_im, ssm_d, ssm_w_out, rwkv_mu, rwkv_w_rkv, rwkv_w0, rwkv_w1,
           rwkv_w2, rwkv_a0, rwkv_a1, rwkv_a2, rwkv_g1, rwkv_g2, rwkv_k_k, rwkv_k_a, rwkv_r_k, rwkv_ln_w,
           rwkv_ln_b, rwkv_w_o, mlp_w1, mlp_w2):
    batch, seq, d = x.shape
    depth = norm_mix.shape[0]
    bf = lambda w: w.astype(BF16)
    h = x.reshape(batch * seq, d)
    ia = ib = ic = 0
    for layer in range(depth):
        kind = layer % 3
        if kind == 0:
            h = attention_layer(h, norm_mix[layer], bf(attn_w_qkv[ia]), bf(attn_w_o[ia]), batch, seq)
            ia += 1
        elif kind == 1:
            h = s5_layer(h, norm_mix[layer], bf(ssm_w_in[ib]), ssm_log_dt[ib], ssm_a_re[ib], ssm_a_im[ib],
                         ssm_b_re[ib], ssm_b_im[ib], ssm_c_re[ib], ssm_c_im[ib], ssm_d[ib],
                         bf(ssm_w_out[ib]), batch, seq)
            ib += 1
        else:
            h = rwkv_layer(h, norm_mix[layer], rwkv_mu[ic], bf(rwkv_w_rkv[ic]), rwkv_w0[ic], rwkv_w1[ic],
                           rwkv_w2[ic], rwkv_a0[ic], rwkv_a1[ic], rwkv_a2[ic], rwkv_g1[ic], rwkv_g2[ic],
                           rwkv_k_k[ic], rwkv_k_a[ic], rwkv_r_k[ic], rwkv_ln_w[ic], rwkv_ln_b[ic],
                           bf(rwkv_w_o[ic]), batch, seq)
            ic += 1
        g_final = norm_f if layer == depth - 1 else None
        h = mlp_residual(h, norm_mlp[layer], bf(mlp_w1[layer]), bf(mlp_w2[layer]), g_final)
    return h.reshape(batch, seq, d)
```

```python
import functools

import jax
import jax.numpy as jnp
from jax import lax
from jax.experimental import pallas as pl
from jax.experimental.pallas import tpu as pltpu

F32 = jnp.float32
BF16 = jnp.bfloat16

NORM_EPS = 1e-5
LANES = 128
VMEM_LIMIT_BYTES = 56 * 2**20
MASK_VALUE = -1e30

ATTN_PATTERNS = ((128, 1), (512, 4), (2048, 16))
ATTN_BLOCK = 128
ATTN_HEAD_DIM = 128
SSM_CH = 16
SSM_CHUNK = 16
RWKV_HEAD_DIM = 64
RWKV_CHUNK = 64
RWKV_GN_EPS = RWKV_HEAD_DIM * 1e-5


def _cparams(*sem):
    return pltpu.CompilerParams(dimension_semantics=sem, vmem_limit_bytes=VMEM_LIMIT_BYTES)


def _tile(n, pref):
    t = min(n, pref)
    while n % t:
        t //= 2
    return t


def _rms(x, g):
    ms = jnp.mean(x * x, axis=-1, keepdims=True)
    return x * lax.rsqrt(ms + NORM_EPS) * g


def _dot(a, b):
    return jnp.dot(a, b, preferred_element_type=F32)


def _dot_nt(a, b):
    return lax.dot_general(a, b, (((1,), (1,)), ((), ())), preferred_element_type=F32)


def _dot_tn(a, b):
    return lax.dot_general(a, b, (((0,), (0,)), ((), ())), preferred_element_type=F32)


def _split2(x):
    hi = x.astype(BF16)
    return hi, (x - hi.astype(F32)).astype(BF16)


def _norm_matmul_strided_kernel(x_ref, g_ref, w_ref, o_ref, h_ref, *, dilation, sub):
    tm = h_ref.shape[0]
    per = sub // dilation

    @pl.when(pl.program_id(1) == 0)
    def _():
        if dilation > 1:
            new = lax.broadcasted_iota(jnp.int32, (sub, sub), 0)
            old = lax.broadcasted_iota(jnp.int32, (sub, sub), 1)
            perm = jnp.where(old == (new % per) * dilation + new // per, 1.0, 0.0).astype(BF16)
        for s in range(tm // sub):
            rows = slice(s * sub, (s + 1) * sub)
            h = _rms(x_ref[rows, :], g_ref[...]).astype(BF16)
            h_ref[rows, :] = _dot(perm, h).astype(BF16) if dilation > 1 else h

    y = _dot(h_ref[...], w_ref[...]).astype(o_ref.dtype)
    for s in range(tm // sub):
        o_ref[:, s * per:(s + 1) * per, :] = y[s * sub:(s + 1) * sub].reshape(dilation, per, y.shape[-1])


def norm_matmul_strided(x, g, w, col0, ncols, dilation, batch, seq, tm=1024, tn=2048, sub=512):
    t, d = x.shape
    tm, tn = _tile(seq, tm), _tile(ncols, tn)
    sub = min(sub, tm)
    assert col0 % tn == 0 and (sub // dilation) % 16 == 0
    nt = seq // tm
    return pl.pallas_call(
        functools.partial(_norm_matmul_strided_kernel, dilation=dilation, sub=sub),
        out_shape=jax.ShapeDtypeStruct((batch, dilation, seq // dilation, ncols), BF16),
        grid=(t // tm, ncols // tn),
        in_specs=[pl.BlockSpec((tm, d), lambda i, j: (i, 0)),
                  pl.BlockSpec((1, d), lambda i, j: (0, 0)),
                  pl.BlockSpec((d, tn), lambda i, j: (0, col0 // tn + j))],
        out_specs=pl.BlockSpec((None, dilation, tm // dilation, tn), lambda i, j: (i // nt, 0, i % nt, j)),
        scratch_shapes=[pltpu.VMEM((tm, d), BF16)],
        compiler_params=_cparams("parallel", "arbitrary"),
        name=f"norm_matmul_stride{dilation}",
    )(x, g.reshape(1, d), w)


def _matmul_res_kernel(a_ref, w_ref, r_ref, o_ref):
    o_ref[...] = r_ref[...] + _dot(a_ref[...], w_ref[...])


def matmul_residual(a, w, res, tm=1024, tn=1024):
    t, k = a.shape
    n = w.shape[1]
    tm, tn = _tile(t, tm), _tile(n, tn)
    return pl.pallas_call(
        _matmul_res_kernel,
        out_shape=jax.ShapeDtypeStruct((t, n), F32),
        grid=(t // tm, n // tn),
        in_specs=[pl.BlockSpec((tm, k), lambda i, j: (i, 0)),
                  pl.BlockSpec((k, tn), lambda i, j: (0, j)),
                  pl.BlockSpec((tm, tn), lambda i, j: (i, j))],
        out_specs=pl.BlockSpec((tm, tn), lambda i, j: (i, j)),
        compiler_params=_cparams("parallel", "parallel"),
        name="matmul_residual",
    )(a, w, res)


def _mlp_kernel(x_ref, g_ref, w1_ref, w2_ref, gf_ref, o_ref, h_ref, *, final_norm):
    f = pl.program_id(1)

    @pl.when(f == 0)
    def _():
        x = x_ref[...]
        h_ref[...] = _rms(x, g_ref[...]).astype(BF16)
        o_ref[...] = x

    a = _dot(h_ref[...], w1_ref[...])
    a = jnp.square(jnp.maximum(a, 0.0)).astype(BF16)
    o_ref[...] += _dot(a, w2_ref[...])

    if final_norm:
        @pl.when(f == pl.num_programs(1) - 1)
        def _():
            o_ref[...] = _rms(o_ref[...], gf_ref[...])


def mlp_residual(x, g, w1, w2, g_final=None, tm=1024, tf=512):
    t, d = x.shape
    ff = w1.shape[1]
    tm, tf = _tile(t, tm), _tile(ff, tf)
    final_norm = g_final is not None
    gf = (g_final if final_norm else g).reshape(1, d)
    return pl.pallas_call(
        functools.partial(_mlp_kernel, final_norm=final_norm),
        out_shape=jax.ShapeDtypeStruct((t, d), F32),
        grid=(t // tm, ff // tf),
        in_specs=[pl.BlockSpec((tm, d), lambda i, f: (i, 0)),
                  pl.BlockSpec((1, d), lambda i, f: (0, 0)),
                  pl.BlockSpec((d, tf), lambda i, f: (0, f)),
                  pl.BlockSpec((tf, d), lambda i, f: (f, 0)),
                  pl.BlockSpec((1, d), lambda i, f: (0, 0))],
        out_specs=pl.BlockSpec((tm, d), lambda i, f: (i, 0)),
        scratch_shapes=[pltpu.VMEM((tm, d), BF16)],
        compiler_params=_cparams("parallel", "arbitrary"),
        name="mlp_residual",
    )(x, g.reshape(1, d), w1, w2, gf)


def _attn_kernel(slope_ref, qkv_ref, o_ref, lse_ref, kvp_ref, *, heads, scale):
    blk, e = ATTN_BLOCK, ATTN_HEAD_DIM
    he = heads * e
    j = pl.program_id(2)

    @pl.when(j == 0)
    def _():
        kvp_ref[...] = jnp.zeros_like(kvp_ref)

    qi = lax.broadcasted_iota(jnp.int32, (blk, blk), 0)
    kj = lax.broadcasted_iota(jnp.int32, (blk, blk), 1)
    log2e, ln2 = 1.4426950408889634, 0.6931471805599453
    dist_c = (qi - kj).astype(F32) * log2e
    dist_p = dist_c + float(blk) * log2e
    mask_c = jnp.where(kj <= qi, 0.0, MASK_VALUE)
    mask_p = jnp.where(kj >= qi, 0.0, MASK_VALUE)
    mask_first = jnp.where((kj >= qi) & (j > 0), 0.0, MASK_VALUE)
    lane = lax.broadcasted_iota(jnp.int32, (blk, LANES), 1)
    together = next(n for n in (8, 4, 2, 1) if heads % n == 0)
    for sb in range(qkv_ref.shape[0] // blk):
        rows = slice(sb * blk, (sb + 1) * blk)
        q_ref, kc_ref, vc_ref = (qkv_ref.at[rows, c * he:(c + 1) * he] for c in range(3))
        if sb == 0:
            kp_ref, vp_ref, mask_prev = kvp_ref.at[:, 0:he], kvp_ref.at[:, he:2 * he], mask_first
        else:
            before = slice((sb - 1) * blk, sb * blk)
            kp_ref, vp_ref, mask_prev = qkv_ref.at[before, he:2 * he], qkv_ref.at[before, 2 * he:3 * he], mask_p
        lse_tile = jnp.zeros((blk, LANES), F32)
        outs = []
        for h0 in range(0, heads, together):
            hs = list(range(h0, h0 + together))
            sls = [slice(h * e, (h + 1) * e) for h in hs]
            qs = [q_ref[:, sl] for sl in sls]
            sc = [_dot_nt(q, kc_ref[:, sl]) for q, sl in zip(qs, sls)]
            sp = [_dot_nt(q, kp_ref[:, sl]) for q, sl in zip(qs, sls)]
            sc = [s * (scale * log2e) + (mask_c - slope_ref[h] * dist_c) for s, h in zip(sc, hs)]
            sp = [s * (scale * log2e) + (mask_prev - slope_ref[h] * dist_p) for s, h in zip(sp, hs)]
            m = [jnp.max(jnp.maximum(a, b), axis=-1, keepdims=True) for a, b in zip(sc, sp)]
            pc = [jnp.exp2(a - mm) for a, mm in zip(sc, m)]
            pp = [jnp.exp2(b - mm) for b, mm in zip(sp, m)]
            den = [jnp.sum(a + b, axis=-1, keepdims=True) for a, b in zip(pc, pp)]
            o = [_dot(a.astype(BF16), vc_ref[:, sl]) + _dot(b.astype(BF16), vp_ref[:, sl])
                 for a, b, sl in zip(pc, pp, sls)]
            for h, oo, dd, mm in zip(hs, o, den, m):
                outs.append((oo / dd).astype(o_ref.dtype))
                lse_tile = jnp.where(lane == h, mm * ln2 + jnp.log(dd), lse_tile)
        for h, oo in enumerate(outs):
            o_ref[rows, h * e:(h + 1) * e] = oo
        lse_ref[rows, :] = lse_tile
    kvp_ref[...] = qkv_ref[qkv_ref.shape[0] - blk:, he:3 * he]


def _attn_group(qkv, slopes, group, dilation, batch, seq, heads):
    e, blk = ATTN_HEAD_DIM, ATTN_BLOCK
    he = heads * e
    sub = seq // dilation
    rows = _tile(sub, 4 * blk)
    out, lse = pl.pallas_call(
        functools.partial(_attn_kernel, heads=heads, scale=e ** -0.5),
        out_shape=(jax.ShapeDtypeStruct((batch, dilation, sub, he), BF16),
                   jax.ShapeDtypeStruct((batch, dilation, sub, LANES), F32)),
        grid=(batch, dilation, sub // rows),
        in_specs=[pl.BlockSpec(memory_space=pltpu.SMEM),
                  pl.BlockSpec((None, None, rows, 3 * he), lambda b, r, j: (b, r, j, 0))],
        out_specs=(pl.BlockSpec((None, None, rows, he), lambda b, r, j: (b, r, j, 0)),
                   pl.BlockSpec((None, None, rows, LANES), lambda b, r, j: (b, r, j, 0))),
        scratch_shapes=[pltpu.VMEM((blk, 2 * he), BF16)],
        compiler_params=_cparams("arbitrary", "arbitrary", "arbitrary"),
        name=f"dilated_attn_g{group}",
    )(slopes, qkv)
    natural = lambda a: a.transpose(0, 2, 1, 3).reshape(batch * seq, a.shape[-1])
    return natural(out), natural(lse)


def _attn_out_kernel(o0_ref, o1_ref, o2_ref, l0_ref, l1_ref, l2_ref, w_ref, r_ref, out_ref, m_ref, *, heads):
    e = ATTN_HEAD_DIM

    @pl.when(pl.program_id(1) == 0)
    def _():
        l0, l1, l2 = l0_ref[...], l1_ref[...], l2_ref[...]
        mx = jnp.maximum(jnp.maximum(l0, l1), l2)
        e0, e1, e2 = jnp.exp(l0 - mx), jnp.exp(l1 - mx), jnp.exp(l2 - mx)
        inv = 1.0 / (e0 + e1 + e2)
        src = lax.broadcasted_iota(jnp.int32, (LANES, heads * e), 0)
        dst = lax.broadcasted_iota(jnp.int32, (LANES, heads * e), 1)
        spread = jnp.where(src == dst // e, 1.0, 0.0).astype(BF16)

        per_lane = lambda w: _dot(w.astype(BF16), spread)

        acc = per_lane(e0 * inv) * o0_ref[...].astype(F32)
        acc += per_lane(e1 * inv) * o1_ref[...].astype(F32)
        acc += per_lane(e2 * inv) * o2_ref[...].astype(F32)
        m_ref[...] = acc.astype(BF16)

    out_ref[...] = r_ref[...] + _dot(m_ref[...], w_ref[...])


def attention_layer(x, g, w_qkv, w_o, batch, seq):
    t, d = x.shape
    n_dil = len(ATTN_PATTERNS)
    he = w_o.shape[0]
    heads = he // ATTN_HEAD_DIM
    n_sl = n_dil * heads
    slopes = (2.0 ** (-8.0 * jnp.arange(1, n_sl + 1, dtype=F32) / n_sl)).reshape(n_dil, heads)
    outs, lses = [], []
    for grp, (window, dilation) in enumerate(ATTN_PATTERNS):
        assert window // dilation == ATTN_BLOCK and (seq // dilation) % ATTN_BLOCK == 0
        qkv = norm_matmul_strided(x, g, w_qkv, grp * 3 * he, 3 * he, dilation, batch, seq)
        o, l = _attn_group(qkv, slopes[grp] * dilation, grp, dilation, batch, seq, heads)
        outs.append(o)
        lses.append(l)
    tm, tn = _tile(t, 512), _tile(d, 1024)
    ospec = pl.BlockSpec((tm, he), lambda i, j: (i, 0))
    lspec = pl.BlockSpec((tm, LANES), lambda i, j: (i, 0))
    return pl.pallas_call(
        functools.partial(_attn_out_kernel, heads=heads),
        out_shape=jax.ShapeDtypeStruct((t, d), F32),
        grid=(t // tm, d // tn),
        in_specs=[ospec, ospec, ospec, lspec, lspec, lspec,
                  pl.BlockSpec((he, tn), lambda i, j: (0, j)),
                  pl.BlockSpec((tm, tn), lambda i, j: (i, j))],
        out_specs=pl.BlockSpec((tm, tn), lambda i, j: (i, j)),
        scratch_shapes=[pltpu.VMEM((tm, he), BF16)],
        compiler_params=_cparams("parallel", "arbitrary"),
        name="attn_merge_out_proj",
    )(*outs, *lses, w_o, x)


def _s5_chunk_operators(log_dt, a_re, a_im, b_re, b_im, c_re, c_im, d_skip, chunk):
    n_g, n_p = a_re.shape
    n_c = b_re.shape[-1]
    a_re, a_im = a_re.T, a_im.T
    b_re, b_im = b_re.transpose(1, 2, 0), b_im.transpose(1, 2, 0)
    c_re, c_im = c_re.transpose(1, 2, 0), c_im.transpose(1, 2, 0)
    dt = jnp.exp(log_dt)[None, :]
    mag = jnp.exp(dt * a_re)
    ab_re = mag * jnp.cos(dt * a_im)
    ab_im = mag * jnp.sin(dt * a_im)
    den = a_re * a_re + a_im * a_im
    zr = ab_re - 1.0
    cr = (zr * a_re + ab_im * a_im) / den
    ci = (ab_im * a_re - zr * a_im) / den
    bb_re = cr[:, None] * b_re - ci[:, None] * b_im
    bb_im = cr[:, None] * b_im + ci[:, None] * b_re
    pr, pi = [jnp.ones_like(ab_re)], [jnp.zeros_like(ab_re)]
    for _ in range(chunk):
        pr, pi = pr + [pr[-1] * ab_re - pi[-1] * ab_im], pi + [pr[-1] * ab_im + pi[-1] * ab_re]
    pr, pi = jnp.stack(pr), jnp.stack(pi)
    ce_re = c_re[None] * pr[:, None] - c_im[None] * pi[:, None]
    ce_im = c_re[None] * pi[:, None] + c_im[None] * pr[:, None]
    kern = jnp.sum(ce_re[:chunk, :, :, None] * bb_re[None, None] - ce_im[:chunk, :, :, None] * bb_im[None, None],
                   axis=2)
    kern = kern.at[0].add(jnp.eye(n_c, dtype=F32)[:, :, None] * d_skip.T[:, None, :])
    group_first = lambda a, *shape: a.astype(BF16).reshape(-1, n_g).T.reshape((n_g,) + shape)
    toep = jnp.stack([jnp.pad(kern[:chunk - l], ((l, 0), (0, 0), (0, 0), (0, 0))) for l in range(chunk)])
    m_op = group_first(toep.transpose(0, 3, 1, 2, 4), chunk * n_c, chunk * n_c)
    qr, qi = pr[chunk - 1::-1][:chunk, :, None], pi[chunk - 1::-1][:chunk, :, None]
    bo_re = (qr * bb_re[None] - qi * bb_im[None]).transpose(0, 2, 1, 3)
    bo_im = (qr * bb_im[None] + qi * bb_re[None]).transpose(0, 2, 1, 3)
    b_op = group_first(jnp.concatenate([bo_re, bo_im, bo_im, bo_re], axis=2), chunk * n_c, 4 * n_p)
    by_state = lambda a: a.transpose(2, 0, 1, 3)
    c_op = group_first(jnp.concatenate([by_state(ce_re[1:]), -by_state(ce_im[1:])], axis=0), 2 * n_p, chunk * n_c)
    gpb = LANES // n_c
    nblk = n_g // gpb
    al_re, al_im = pr[chunk].T, pi[chunk].T
    per_blk = lambda parts: jnp.concatenate(parts, axis=-1).reshape(nblk, 1, gpb * 2 * n_p)
    coef_same = jnp.concatenate([per_blk([al_re, al_re])] * 2, axis=-1)
    coef_cross = jnp.concatenate([per_blk([-al_im, al_im]), per_blk([al_im, -al_im])], axis=-1)
    return m_op, b_op, c_op, coef_same, coef_cross


def _chunk_rows(u_ref):
    return jnp.concatenate([jnp.concatenate([u_ref[b, l] for l in range(u_ref.shape[1])], axis=-1)
                            for b in range(u_ref.shape[0])], axis=0)


def _first_visit_of_block():
    return pl.program_id(1) == 0


def _s5_in_kernel(u_ref, b_ref, o_ref, dense_ref, *, n_c):
    @pl.when(_first_visit_of_block())
    def _():
        gpb, rows, cols = b_ref.shape
        half = cols // 2
        dense_ref[...] = jnp.zeros_like(dense_ref)
        for gi in range(gpb):
            for l in range(rows // n_c):
                r0 = l * LANES + gi * n_c
                piece = b_ref[gi, l * n_c:(l + 1) * n_c, :]
                dense_ref[r0:r0 + n_c, gi * half:(gi + 1) * half] = piece[:, :half]
                dense_ref[r0:r0 + n_c, (gpb + gi) * half:(gpb + gi + 1) * half] = piece[:, half:]

    o_ref[...] = _dot(_chunk_rows(u_ref), dense_ref[...]).reshape(o_ref.shape)


def _s5_scan_kernel(xin_ref, cs_ref, cc_ref, o_ref, st_ref, *, half):
    @pl.when(pl.program_id(2) == 0)
    def _():
        st_ref[...] = jnp.zeros_like(st_ref)

    cs, cc = cs_ref[...], cc_ref[...]

    def step(n, st):
        o_ref[pl.ds(n, 1), :] = st[:, :half]
        st_sw = jnp.concatenate([st[:, half:], st[:, :half]], axis=-1)
        return cs * st + cc * st_sw + xin_ref[pl.ds(n, 1), :]

    st_ref[...] = lax.fori_loop(0, xin_ref.shape[0], step, st_ref[...])


def _s5_out_kernel(u_ref, m_ref, xp_ref, c_ref, o_ref, mdense_ref, cdense_ref, *, n_c):
    @pl.when(_first_visit_of_block())
    def _():
        gpb, rows, cols = m_ref.shape
        n_q = c_ref.shape[1]
        src = lax.broadcasted_iota(jnp.int32, (cols, mdense_ref.shape[1]), 0)
        dst = lax.broadcasted_iota(jnp.int32, (cols, mdense_ref.shape[1]), 1)
        for gi in range(gpb):
            spread = jnp.where(dst == (src // n_c) * LANES + gi * n_c + src % n_c, 1.0, 0.0).astype(BF16)
            wide = _dot(m_ref[gi], spread).astype(BF16)
            for l in range(rows // n_c):
                r0 = l * LANES + gi * n_c
                mdense_ref[r0:r0 + n_c, :] = wide[l * n_c:(l + 1) * n_c, :]
            cdense_ref[gi * n_q:(gi + 1) * n_q, :] = _dot(c_ref[gi], spread).astype(BF16)

    nb, tr, n_q2 = xp_ref.shape
    xp = xp_ref[...].reshape(nb * tr, n_q2).astype(BF16)
    y = _dot(_chunk_rows(u_ref), mdense_ref[...]) + _dot(xp, cdense_ref[...])
    y = jax.nn.gelu(y).astype(o_ref.dtype)
    for b in range(nb):
        for step in range(o_ref.shape[1]):
            o_ref[b, step] = y[b * tr:(b + 1) * tr, step * LANES:(step + 1) * LANES]


def _glu_out_kernel(y_ref, wa_ref, wb_ref, r_ref, o_ref, yn_ref, *, dilation, sub):
    @pl.when(pl.program_id(1) == 0)
    def _():
        tm, kdim = yn_ref.shape
        per = sub // dilation
        nat = lax.broadcasted_iota(jnp.int32, (sub, sub), 0)
        src = lax.broadcasted_iota(jnp.int32, (sub, sub), 1)
        perm = jnp.where(src == (nat % dilation) * per + nat // dilation, 1.0, 0.0).astype(BF16)
        for s in range(tm // sub):
            grouped = y_ref[:, s * per:(s + 1) * per, :].reshape(sub, kdim)
            yn_ref[s * sub:(s + 1) * sub, :] = _dot(perm, grouped).astype(BF16)

    y = yn_ref[...]
    o_ref[...] = r_ref[...] + _dot(y, wa_ref[...]) * jax.nn.sigmoid(_dot(y, wb_ref[...]))


def s5_layer(x, g, w_in, log_dt, a_re, a_im, b_re, b_im, c_re, c_im, d_skip, w_out, batch, seq):
    t, d = x.shape
    n_g, n_p = a_re.shape
    n_c = SSM_CH
    ck = SSM_CHUNK
    gc = n_g * n_c
    nblk = gc // LANES
    n_chunks = seq // ck
    sw = 4 * n_p * (LANES // n_c)
    m_op, b_op, c_op, coef_same, coef_cross = _s5_chunk_operators(
        log_dt, a_re, a_im, b_re, b_im, c_re, c_im, d_skip, ck)
    u = norm_matmul_strided(x, g, w_in, 0, gc, ck, batch, seq)
    tr = _tile(n_chunks, 512)
    u_spec = pl.BlockSpec((batch, ck, tr, LANES), lambda k, n: (0, 0, n, k))
    gpb = LANES // n_c
    group_ops = lambda a: pl.BlockSpec((gpb,) + a.shape[1:], lambda k, n: (k, 0, 0))
    xin = pl.pallas_call(
        functools.partial(_s5_in_kernel, n_c=n_c),
        out_shape=jax.ShapeDtypeStruct((batch, n_chunks, nblk * sw), F32),
        grid=(nblk, n_chunks // tr),
        in_specs=[u_spec, group_ops(b_op)],
        out_specs=pl.BlockSpec((batch, tr, sw), lambda k, n: (0, n, k)),
        scratch_shapes=[pltpu.VMEM((ck * LANES, sw), BF16)],
        compiler_params=_cparams("arbitrary", "arbitrary"),
        name="s5_chunk_inputs",
    )(u, b_op)
    coef_spec = pl.BlockSpec((None, 1, sw), lambda b, k, n: (k, 0, 0))
    xprev = pl.pallas_call(
        functools.partial(_s5_scan_kernel, half=sw // 2),
        out_shape=jax.ShapeDtypeStruct((batch, n_chunks, nblk * sw // 2), F32),
        grid=(batch, nblk, n_chunks // tr),
        in_specs=[pl.BlockSpec((None, tr, sw), lambda b, k, n: (b, n, k)), coef_spec, coef_spec],
        out_specs=pl.BlockSpec((None, tr, sw // 2), lambda b, k, n: (b, n, k)),
        scratch_shapes=[pltpu.VMEM((1, sw), F32)],
        compiler_params=_cparams("parallel", "parallel", "arbitrary"),
        name="s5_chunk_scan",
    )(xin, coef_same, coef_cross)
    y = pl.pallas_call(
        functools.partial(_s5_out_kernel, n_c=n_c),
        out_shape=jax.ShapeDtypeStruct((batch, ck, n_chunks, gc), BF16),
        grid=(nblk, n_chunks // tr),
        in_specs=[u_spec, group_ops(m_op),
                  pl.BlockSpec((batch, tr, sw // 2), lambda k, n: (0, n, k)), group_ops(c_op)],
        out_specs=u_spec,
        scratch_shapes=[pltpu.VMEM((ck * LANES, ck * LANES), BF16), pltpu.VMEM((sw // 2, ck * LANES), BF16)],
        compiler_params=_cparams("arbitrary", "arbitrary"),
        name="s5_chunk_outputs",
    )(u, m_op, xprev, c_op)
    tm, tn2 = _tile(seq, 1024), _tile(d, 512)
    nt, nj = seq // tm, d // tn2
    return pl.pallas_call(
        functools.partial(_glu_out_kernel, dilation=ck, sub=min(tm, 512)),
        out_shape=jax.ShapeDtypeStruct((t, d), F32),
        grid=(t // tm, nj),
        in_specs=[pl.BlockSpec((None, ck, tm // ck, gc), lambda i, j: (i // nt, 0, i % nt, 0)),
                  pl.BlockSpec((gc, tn2), lambda i, j: (0, j)),
                  pl.BlockSpec((gc, tn2), lambda i, j: (0, j + nj)),
                  pl.BlockSpec((tm, tn2), lambda i, j: (i, j))],
        out_specs=pl.BlockSpec((tm, tn2), lambda i, j: (i, j)),
        scratch_shapes=[pltpu.VMEM((tm, gc), BF16)],
        compiler_params=_cparams("parallel", "arbitrary"),
        name="s5_glu_out_proj",
    )(y, w_out, w_out, x)


def _shift_norm(x_ref, xp_ref, g_ref, first):
    g = g_ref[...]
    h = _rms(x_ref[...], g)
    prev = _rms(xp_ref[7:8, :], g)
    prev = jnp.where(first, 0.0, prev)
    row = lax.broadcasted_iota(jnp.int32, h.shape, 0)
    hp = jnp.where(row == 0, prev, pltpu.roll(h, 1, 0))
    return h, hp


def _rwkv_proj_kernel(x_ref, xp_ref, g_ref, mu_ref, w_ref, o_ref, h_ref, d_ref, l_ref, *, tm, seq):
    i, j, n = pl.program_id(0), pl.program_id(1), pl.program_id(2)

    @pl.when((j == 0) & (n == 0))
    def _():
        h, hp = _shift_norm(x_ref, xp_ref, g_ref, (i * tm) % seq == 0)
        h_ref[...] = h
        d_ref[...] = hp - h

    @pl.when(n == 0)
    def _():
        l_ref[...] = (h_ref[...] + d_ref[...] * mu_ref[...]).astype(BF16)

    o_ref[...] = _dot(l_ref[...], w_ref[...])


def _softplus(z):
    return jnp.maximum(z, 0.0) + jnp.log(1.0 + jnp.exp(-jnp.abs(z)))


def _rwkv_lora_kernel(x_ref, xp_ref, g_ref, mu_ref, w0_ref, w1_ref, w2_ref, a0_ref, a1_ref, a2_ref,
                      g1_ref, g2_ref, lw_ref, a_ref, gate_ref, *, tm, seq):
    h, hp = _shift_norm(x_ref, xp_ref, g_ref, (pl.program_id(0) * tm) % seq == 0)
    dlt = hp - h
    xw = (h + dlt * mu_ref[0:1, :]).astype(BF16)
    xa = (h + dlt * mu_ref[1:2, :]).astype(BF16)
    xg = (h + dlt * mu_ref[2:3, :]).astype(BF16)
    wl = w0_ref[...] + _dot(jnp.tanh(_dot(xw, w1_ref[...])).astype(BF16), w2_ref[...])
    w = -_softplus(-wl) - 0.5
    lw_ref[...] = -jnp.exp(w)
    a_ref[...] = jax.nn.sigmoid(a0_ref[...] + _dot(_dot(xa, a1_ref[...]).astype(BF16), a2_ref[...]))
    gate_ref[...] = _dot(jax.nn.sigmoid(_dot(xg, g1_ref[...])).astype(BF16), g2_ref[...])


def _rwkv_core_kernel(r_ref, k_ref, v_ref, lw_ref, a_ref, gate_ref, kk_ref, ka_ref, rk_ref, lnw_ref, lnb_ref,
                      o_ref, s_ref, lhs_ref, rhs_ref, bk_ref, v2_ref, dec_ref, y_ref):
    ck, hd = RWKV_CHUNK, RWKV_HEAD_DIM
    nb, tc, width = r_ref.shape
    nch, ck2 = tc // ck, 2 * ck
    seqs = [(bi, slice(pi * LANES, (pi + 1) * LANES)) for bi in range(nb) for pi in range(width // LANES)]

    @pl.when(pl.program_id(1) == 0)
    def _():
        s_ref[...] = jnp.zeros_like(s_ref)

    lane = lax.broadcasted_iota(jnp.int32, (1, 1, LANES), 2)
    head_a = lane < hd
    hrow = lax.broadcasted_iota(jnp.int32, (LANES, LANES), 0) // hd
    hcol = lax.broadcasted_iota(jnp.int32, (LANES, LANES), 1) // hd
    head_ones = jnp.where(hrow == hcol, 1.0, 0.0).astype(BF16)
    trow = lax.broadcasted_iota(jnp.int32, (nch, ck2, ck), 1)
    tcol = lax.broadcasted_iota(jnp.int32, (nch, ck2, ck), 2)
    sum_ops = jnp.where((tcol <= trow) | (trow >= ck), 1.0, 0.0).astype(BF16)

    def chunk_sums(x):
        hi, lo = _split2(x)
        bdot = lambda t: lax.dot_general(sum_ops, t, (((2,), (1,)), ((0,), (0,))), preferred_element_type=F32)
        both = bdot(hi) + bdot(lo)
        return both[:, :ck], both[:, ck:]

    def head_sum(x):
        hi, lo = _split2(x)
        return _dot(hi, head_ones) + _dot(lo, head_ones)

    def stack_heads(x):
        xb = x.astype(BF16)
        zero = jnp.zeros_like(xb)
        return jnp.concatenate([jnp.where(head_a, xb, zero), jnp.where(head_a, zero, xb)], axis=1)

    for si, (bi, ls) in enumerate(seqs):
        k_all, a_all = k_ref[bi, :, ls], a_ref[bi, :, ls]
        kk = k_all * kk_ref[:, ls]
        kk = kk * lax.rsqrt(jnp.maximum(head_sum(kk * kk), 1e-24))
        k2 = k_all * (1.0 + (a_all - 1.0) * ka_ref[:, ls])
        lw = lw_ref[bi, :, ls]
        by_chunk = lambda x: x.reshape(nch, ck, LANES)
        cs, tot = chunk_sums(by_chunk(lw))
        gam_inv, gam_rem = jnp.exp(-cs), jnp.exp(tot - cs)
        atm = by_chunk(-kk) * jnp.exp(cs - by_chunk(lw))
        rm = by_chunk(r_ref[bi, :, ls]) * jnp.exp(cs)
        b3, k3 = by_chunk(kk * a_all), by_chunk(k2)
        lhs_ref[si] = jnp.concatenate([stack_heads(atm), stack_heads(rm)], axis=1)
        rhs_ref[si] = jnp.concatenate([stack_heads(b3 * gam_inv), stack_heads(k3 * gam_inv)], axis=1)
        bk_ref[si] = jnp.concatenate([stack_heads(b3 * gam_rem), stack_heads(k3 * gam_rem)], axis=1)
        v2_ref[si] = stack_heads(by_chunk(v_ref[bi, :, ls]))
        dec_ref[si] = jnp.exp(tot[:, 0:1, :])

    row = lax.broadcasted_iota(jnp.int32, (ck2, ck2), 0)
    col = lax.broadcasted_iota(jnp.int32, (ck2, ck2), 1)
    incl = col <= row
    strict = col < row
    eye = jnp.where(row == col, 1.0, 0.0)
    n_seq = len(seqs)
    each = lambda f, *lists: [f(*args) for args in zip(*lists)]

    def chunk_step(ci, carry):
        lhs = [lhs_ref[si, ci] for si in range(n_seq)]
        gram = each(lambda l, si: _dot_nt(l, rhs_ref[si, ci]), lhs, range(n_seq))
        a_ab = each(lambda g: jnp.where(strict, g[:ck2, :ck2], 0.0), gram)
        a_lo = each(lambda g: jnp.concatenate([jnp.where(strict, g[:ck2, ck2:], 0.0),
                                               jnp.where(incl, g[ck2:, ck2:], 0.0)], axis=0).astype(BF16), gram)
        a_rb = each(lambda g: jnp.where(incl, g[ck2:, :ck2], 0.0).astype(BF16), gram)
        inv = each(lambda a: eye + a, a_ab)
        pw = each(lambda a: _dot(a.astype(BF16), a.astype(BF16)), a_ab)
        m = 2
        while 2 * m < ck:
            pwb = each(lambda p: p.astype(BF16), pw)
            both = each(lambda p, t: _dot(jnp.concatenate([p, t.astype(BF16)], axis=0), p), pwb, inv)
            pw = each(lambda z: z[:ck2], both)
            inv = each(lambda t, z: t + z[ck2:], inv, both)
            m *= 2
        inv = each(lambda t, p: t + _dot(t.astype(BF16), p.astype(BF16)), inv, pw)
        v2 = [v2_ref[si, ci] for si in range(n_seq)]
        av = each(_dot, a_lo, v2)
        s = [s_ref[si] for si in range(n_seq)]
        xs = each(lambda l, st: _dot_nt(l, st.astype(BF16)), lhs, s)
        u = each(lambda t, x, w: _dot(t.astype(BF16), (x[:ck2] + w[:ck2]).astype(BF16)).astype(BF16), inv, xs, av)
        y2 = each(lambda x, w, arb, ub: x[ck2:] + w[ck2:] + _dot(arb, ub), xs, av, a_rb, u)
        sl = pl.ds(pl.multiple_of(ci * ck, ck), ck)
        for si, (bi, ls) in enumerate(seqs):
            y_ref[bi, sl, ls] = y2[si][:ck] + y2[si][ck:]
            uv = jnp.concatenate([u[si], v2[si]], axis=0)
            s_ref[si] = s[si] * dec_ref[si, ci] + _dot_tn(uv, bk_ref[si, ci])
        return carry

    lax.fori_loop(0, nch, chunk_step, 0)

    for bi, ls in seqs:
        y = y_ref[bi, :, ls]
        mean = head_sum(y) * (1.0 / hd)
        yc = y - mean
        var = _dot((yc * yc).astype(BF16), head_ones) * (1.0 / hd)
        yn = yc * lax.rsqrt(var + RWKV_GN_EPS) * lnw_ref[:, ls] + lnb_ref[:, ls]
        k2 = k_ref[bi, :, ls] * (1.0 + (a_ref[bi, :, ls] - 1.0) * ka_ref[:, ls])
        bonus = head_sum(r_ref[bi, :, ls] * k2 * rk_ref[:, ls]) * v_ref[bi, :, ls]
        o_ref[bi, :, ls] = ((yn + bonus) * gate_ref[bi, :, ls]).astype(o_ref.dtype)


def _pad_to(a, axis, size):
    pad = [(0, 0)] * a.ndim
    pad[axis] = (0, size - a.shape[axis])
    return jnp.pad(a, pad)


def rwkv_layer(x, g, mu, w_rkv, w0, w1, w2, a0, a1, a2, g1, g2, k_k, k_a, r_k, ln_w, ln_b, w_o, batch, seq):
    t, d = x.shape
    row = lambda p: p.reshape(1, d).astype(F32)
    g2d = g.reshape(1, d)
    tn = _tile(d, 512)
    prev_spec = lambda tm: pl.BlockSpec((8, d), (lambda i, *_: (jnp.maximum(i * (tm // 8) - 1, 0), 0)))
    tm = _tile(seq, 1024)
    rkv = pl.pallas_call(
        functools.partial(_rwkv_proj_kernel, tm=tm, seq=seq),
        out_shape=jax.ShapeDtypeStruct((3, t, d), F32),
        grid=(t // tm, 3, d // tn),
        in_specs=[pl.BlockSpec((tm, d), lambda i, j, n: (i, 0)),
                  prev_spec(tm),
                  pl.BlockSpec((1, d), lambda i, j, n: (0, 0)),
                  pl.BlockSpec((None, 1, d), lambda i, j, n: (j, 0, 0)),
                  pl.BlockSpec((None, d, tn), lambda i, j, n: (j, 0, n))],
        out_specs=pl.BlockSpec((None, tm, tn), lambda i, j, n: (j, i, n)),
        scratch_shapes=[pltpu.VMEM((tm, d), F32), pltpu.VMEM((tm, d), F32), pltpu.VMEM((tm, d), BF16)],
        compiler_params=_cparams("parallel", "arbitrary", "arbitrary"),
        name="rwkv_rkv_proj",
    )(x, x, g2d, mu[:3].reshape(3, 1, d), w_rkv)

    pad_rank = lambda w_a, w_b: (_pad_to(w_a, 1, -(-w_a.shape[1] // LANES) * LANES).astype(BF16),
                                 _pad_to(w_b, 0, -(-w_b.shape[0] // LANES) * LANES).astype(BF16))
    w1p, w2p = pad_rank(w1, w2)
    a1p, a2p = pad_rank(a1, a2)
    g1p, g2p = pad_rank(g1, g2)
    full = lambda a: pl.BlockSpec(a.shape, lambda i: (0,) * a.ndim)
    tm = _tile(seq, 256)
    tok = pl.BlockSpec((tm, d), lambda i: (i, 0))
    lora_in = [x, x, g2d, mu[3:6], row(w0), w1p, w2p, row(a0), a1p, a2p, g1p, g2p]
    lw, a_gate, gate = pl.pallas_call(
        functools.partial(_rwkv_lora_kernel, tm=tm, seq=seq),
        out_shape=(jax.ShapeDtypeStruct((t, d), F32),) * 3,
        grid=(t // tm,),
        in_specs=[tok, prev_spec(tm)] + [full(a) for a in lora_in[2:]],
        out_specs=(tok, tok, tok),
        compiler_params=_cparams("parallel"),
        name="rwkv_lora",
    )(*lora_in)

    tc = _tile(seq, 512)
    wd = _tile(d, 4 * LANES)
    tokc = pl.BlockSpec((batch, tc, wd), lambda p, c: (0, c, p))
    rkvc = lambda which: pl.BlockSpec((None, batch, tc, wd), lambda p, c: (which, 0, c, p))
    par = pl.BlockSpec((1, wd), lambda p, c: (0, p))
    n_seq = batch * (wd // LANES)
    nch = tc // RWKV_CHUNK
    bsd = lambda a: a.reshape(batch, seq, d)
    rkv4 = rkv.reshape(3, batch, seq, d)
    mixed = pl.pallas_call(
        _rwkv_core_kernel,
        out_shape=jax.ShapeDtypeStruct((batch, seq, d), BF16),
        grid=(d // wd, seq // tc),
        in_specs=[rkvc(0), rkvc(1), rkvc(2), tokc, tokc, tokc, par, par, par, par, par],
        out_specs=tokc,
        scratch_shapes=[pltpu.VMEM((n_seq, LANES, LANES), F32)]
        + [pltpu.VMEM((n_seq, nch, 4 * RWKV_CHUNK, LANES), BF16)] * 3
        + [pltpu.VMEM((n_seq, nch, 2 * RWKV_CHUNK, LANES), BF16),
           pltpu.VMEM((n_seq, nch, 1, LANES), F32),
           pltpu.VMEM((batch, tc, wd), F32)],
        compiler_params=_cparams("parallel", "arbitrary"),
        name="rwkv_chunked_state",
    )(rkv4, rkv4, rkv4, bsd(lw), bsd(a_gate), bsd(gate), row(k_k), row(k_a), row(r_k), row(ln_w), row(ln_b))
    return matmul_residual(mixed.reshape(t, d), w_o, x)


def kernel(x, norm_mix, norm_mlp, norm_f, attn_w_qkv, attn_w_o, ssm_w_in, ssm_log_dt, ssm_a_re, ssm_a_im,
           ssm_b_re, ssm_b_im, ssm_c_re, ssm_c_im, ssm_d, ssm_w_out, rwkv_mu, rwkv_w_rkv, rwkv_w0, rwkv_w1,
           rwkv_w2, rwkv_a0, rwkv_a1, rwkv_a2, rwkv_g1, rwkv_g2, rwkv_k_k, rwkv_k_a, rwkv_r_k, rwkv_ln_w,
           rwkv_ln_b, rwkv_w_o, mlp_w1, mlp_w2):
    batch, seq, d = x.shape
    depth = norm_mix.shape[0]
    bf = lambda w: w.astype(BF16)
    h = x.reshape(batch * seq, d)
    ia = ib = ic = 0
    for layer in range(depth):
        kind = layer % 3
        if kind == 0:
            h = attention_layer(h, norm_mix[layer], bf(attn_w_qkv[ia]), bf(attn_w_o[ia]), batch, seq)
            ia += 1
        elif kind == 1:
            h = s5_layer(h, norm_mix[layer], bf(ssm_w_in[ib]), ssm_log_dt[ib], ssm_a_re[ib], ssm_a_im[ib],
                         ssm_b_re[ib], ssm_b_im[ib], ssm_c_re[ib], ssm_c_im[ib], ssm_d[ib],
                         bf(ssm_w_out[ib]), batch, seq)
            ib += 1
        else:
            h = rwkv_layer(h, norm_mix[layer], rwkv_mu[ic], bf(rwkv_w_rkv[ic]), rwkv_w0[ic], rwkv_w1[ic],
                           rwkv_w2[ic], rwkv_a0[ic], rwkv_a1[ic], rwkv_a2[ic], rwkv_g1[ic], rwkv_g2[ic],
                           rwkv_k_k[ic], rwkv_k_a[ic], rwkv_r_k[ic], rwkv_ln_w[ic], rwkv_ln_b[ic],
                           bf(rwkv_w_o[ic]), batch, seq)
            ic += 1
        g_final = norm_f if layer == depth - 1 else None
        h = mlp_residual(h, norm_mlp[layer], bf(mlp_w1[layer]), bf(mlp_w2[layer]), g_final)
    return h.reshape(batch, seq, d)
```
